```python
import jax
import jax.numpy as jnp
from jax import lax
import numpy as np

D_MODEL = 1024
BATCH = 8
SEQ = 2048
DEPTH = 2
DEC_BATCH = 128
DEC_SEQ = 8
PAST_LEN = 2048
PAGE_SIZE = 128

HEAD_DIM = 64
N_HEADS = (D_MODEL // 2) // HEAD_DIM
KV_HEADS = 2
IDX_HEADS = 8
IDX_DIM = 64
TOPK_MAX = 256
Q_BLOCK = 128
HG_KDIM = 128
HG_VDIM = 128
HG_HEADS = (D_MODEL // 2) // HG_VDIM
HG_CHUNK = 64
D_FF = 2816
P_DIM = 256
ROPE_THETA = 10000.0
EPS = 1e-6
ATT_WIDTH = N_HEADS * HEAD_DIM
HG_WIDTH = HG_HEADS * HG_VDIM
MIX_WIDTH = ATT_WIDTH + HG_WIDTH
SPLITS = (ATT_WIDTH, KV_HEADS * HEAD_DIM, KV_HEADS * HEAD_DIM,
          IDX_HEADS * IDX_DIM, IDX_DIM, IDX_HEADS,
          HG_HEADS * HG_KDIM, HG_HEADS * HG_KDIM, HG_WIDTH, HG_WIDTH)
IN_WIDTH = sum(SPLITS)

kernel_name = 'hymba_hgrn2_dsa_macaron_step'


def rmsnorm(x, g):
    xf = x.astype(jnp.float32)
    y = xf * lax.rsqrt(jnp.mean(xf * xf, axis=-1, keepdims=True) + EPS)
    return (y * g.astype(jnp.float32)).astype(x.dtype)


def swiglu(x, w_gate, w_up, w_down):
    return (jax.nn.silu(x @ w_gate) * (x @ w_up)) @ w_down


def rope(x, pos):
    half = x.shape[-1] // 2
    inv_freq = ROPE_THETA ** (-jnp.arange(half, dtype=jnp.float32) / half)
    ang = pos.astype(jnp.float32)[:, None] * inv_freq[None, :]
    cos = jnp.cos(ang)[None, :, None, :]
    sin = jnp.sin(ang)[None, :, None, :]
    xf = x.astype(jnp.float32)
    x1, x2 = xf[..., :half], xf[..., half:]
    return jnp.concatenate([x1 * cos - x2 * sin, x2 * cos + x1 * sin], axis=-1).astype(x.dtype)


def gather_pages(pool, page_table):
    g = pool[page_table]
    return g.reshape((g.shape[0], g.shape[1] * g.shape[2]) + g.shape[3:])


def hgrn_lower_bounds(logits):
    sm = jax.nn.softmax(logits.astype(jnp.float32), axis=0)
    return jnp.cumsum(sm, axis=0) - sm[0:1]


def hgrn2(q, f_logit, i, lb, s0):
    B, T, _ = q.shape
    f = lb + (1.0 - lb) * jax.nn.sigmoid(f_logit.astype(jnp.float32))
    logf = jnp.log(f)
    k = 1.0 - f
    qf = jax.nn.silu(q.astype(jnp.float32)) * (HG_KDIM ** -0.5)
    v = i.astype(jnp.float32)
    C = min(HG_CHUNK, T)
    nc = -(-T // C)
    pad = nc * C - T

    def heads(a, d):
        a = a.reshape(B, T, HG_HEADS, d)
        a = jnp.pad(a, ((0, 0), (0, pad), (0, 0), (0, 0)))
        return a.reshape(B, nc, C, HG_HEADS, d).transpose(1, 0, 3, 2, 4)

    qs, ks, gs, vs = heads(qf, HG_KDIM), heads(k, HG_KDIM), heads(logf, HG_KDIM), heads(v, HG_VDIM)
    causal = jnp.tril(jnp.ones((C, C), dtype=bool))[:, :, None]

    def step(S, xs):
        qc, kc, gc, vc = xs
        b = jnp.cumsum(gc, axis=2)
        o_inter = jnp.einsum('bhtd,bhde->bhte', qc * jnp.exp(b), S)
        diff = b[:, :, :, None, :] - b[:, :, None, :, :]
        decay = jnp.where(causal, jnp.exp(jnp.where(causal, diff, 0.0)), 0.0)
        att = jnp.einsum('bhtd,bhsd,bhtsd->bhts', qc, kc, decay)
        o = o_inter + jnp.einsum('bhts,bhse->bhte', att, vc)
        b_last = b[:, :, -1:, :]
        S_new = jnp.exp(b_last[:, :, 0, :])[..., None] * S + jnp.einsum(
            'bhsd,bhse->bhde', kc * jnp.exp(b_last - b), vc)
        return S_new, o

    s_fin, outs = lax.scan(step, s0, (qs, ks, gs, vs))
    o = outs.transpose(1, 0, 3, 2, 4).reshape(B, nc * C, HG_HEADS, HG_VDIM)[:, :T]
    return o, s_fin


def dsa_attention(q, k_all, v_all, iq, ik_all, iw, q_pos):
    B, Tq = q.shape[:2]
    L = k_all.shape[1]
    topk = min(TOPK_MAX, L // 4)
    qb = min(Q_BLOCK, Tq)
    nb = -(-Tq // qb)
    pad = nb * qb - Tq
    key_pos = jnp.arange(L, dtype=jnp.int32)
    bidx = jnp.arange(B)[:, None, None]
    ikf = ik_all.astype(jnp.float32)
    group = N_HEADS // KV_HEADS

    def blocks(a):
        a = jnp.pad(a, [(0, 0), (0, pad)] + [(0, 0)] * (a.ndim - 2))
        return jnp.moveaxis(a.reshape((B, nb, qb) + a.shape[2:]), 1, 0)

    pos_b = jnp.pad(q_pos, (0, pad), mode='edge').reshape(nb, qb)

    def one_block(xs):
        qx, iqx, iwx, px = xs
        logits = jnp.einsum('bqhd,bsd->bqhs', iqx.astype(jnp.float32), ikf) * (IDX_DIM ** -0.5)
        score = jnp.einsum('bqh,bqhs->bqs', iwx.astype(jnp.float32), jax.nn.relu(logits))
        admissible = key_pos[None, None, :] <= px[None, :, None]
        score = jnp.where(admissible, score, -jnp.inf)
        _, sel = lax.top_k(score, topk)
        valid = sel <= px[None, :, None]
        k_sel = k_all[bidx, sel].astype(jnp.float32)
        v_sel = v_all[bidx, sel].astype(jnp.float32)
        qg = qx.astype(jnp.float32).reshape(B, qb, KV_HEADS, group, HEAD_DIM)
        s = jnp.einsum('bqhgd,bqjhd->bqhgj', qg, k_sel) * (HEAD_DIM ** -0.5)
        s = jnp.where(valid[:, :, None, None, :], s, -jnp.inf)
        p = jax.nn.softmax(s, axis=-1)
        o = jnp.einsum('bqhgj,bqjhd->bqhgd', p, v_sel)
        return o.reshape(B, qb, ATT_WIDTH).astype(q.dtype)

    out = lax.map(one_block, (blocks(q), blocks(iq), blocks(iw), pos_b))
    return jnp.moveaxis(out, 0, 1).reshape(B, nb * qb, ATT_WIDTH)[:, :Tq]


def decoder_layer(x, p_emb, q_pos, past_k, past_v, past_ik, s0, lb,
                  n_f1_pre, n_f1_post, w_f1_gate, w_f1_up, w_f1_down,
                  n_mix_pre, n_mix_post, w_in, hg_norm, w_out,
                  n_f2_pre, n_f2_post, w_f2_gate, w_f2_up, w_f2_down,
                  n_ple_pre, n_ple_post, w_ple_proj, w_ple_gate):
    B, T, _ = x.shape
    h = x + 0.5 * rmsnorm(swiglu(rmsnorm(x, n_f1_pre), w_f1_gate, w_f1_up, w_f1_down), n_f1_post)
    u = rmsnorm(h, n_mix_pre)
    proj = u @ w_in
    cuts = []
    acc = 0
    for width in SPLITS[:-1]:
        acc += width
        cuts.append(acc)
    q, k, v, iq, ik, iw, hq, hf, hi, hgate = jnp.split(proj, cuts, axis=-1)
    q = rope(q.reshape(B, T, N_HEADS, HEAD_DIM), q_pos)
    k = rope(k.reshape(B, T, KV_HEADS, HEAD_DIM), q_pos)
    v = v.reshape(B, T, KV_HEADS, HEAD_DIM)
    iq = rope(iq.reshape(B, T, IDX_HEADS, IDX_DIM), q_pos)
    ik = rope(ik.reshape(B, T, 1, IDX_DIM), q_pos)[:, :, 0]
    iw = iw * (IDX_HEADS ** -0.5)
    if past_k is None:
        k_all, v_all, ik_all = k, v, ik
    else:
        k_all = jnp.concatenate([past_k.astype(k.dtype), k], axis=1)
        v_all = jnp.concatenate([past_v.astype(v.dtype), v], axis=1)
        ik_all = jnp.concatenate([past_ik.astype(ik.dtype), ik], axis=1)
    att = dsa_attention(q, k_all, v_all, iq, ik_all, iw, q_pos)
    o_hg, s_fin = hgrn2(hq, hf, hi, lb, s0)
    o_hg = rmsnorm(o_hg, hg_norm) * jax.nn.silu(hgate.reshape(B, T, HG_HEADS, HG_VDIM).astype(jnp.float32))
    mix = jnp.concatenate([att, o_hg.reshape(B, T, HG_WIDTH).astype(att.dtype)], axis=-1) @ w_out
    h = h + rmsnorm(mix, n_mix_post)
    h = h + 0.5 * rmsnorm(swiglu(rmsnorm(h, n_f2_pre), w_f2_gate, w_f2_up, w_f2_down), n_f2_post)
    gate = jax.nn.sigmoid(rmsnorm(h, n_ple_pre) @ w_ple_gate)
    h = h + rmsnorm((p_emb.astype(h.dtype) @ w_ple_proj) * gate, n_ple_post)
    return h, k, v, ik, s_fin


def _normal(k, shape, scale):
    return scale * jax.random.normal(k, shape, jnp.float32)


def _gain(k, shape):
    return 1.0 + 0.02 * jax.random.normal(k, shape, jnp.float32)


def setup_inputs(seed: int = 0) -> dict:
    key = jax.random.key(seed)
    k = jax.random.split(key, 32)
    n_pages = PAST_LEN // PAGE_SIZE
    n_used = DEC_BATCH * n_pages
    n_pool = n_used + max(1, n_used // 4)
    page_table = jax.random.permutation(k[8], n_pool)[:n_used].reshape(DEC_BATCH, n_pages).astype(jnp.int32)
    d = D_MODEL
    return {
        'x_prompt': _normal(k[0], (BATCH, SEQ, d), 1.0),
        'x_sample': _normal(k[1], (DEC_BATCH, DEC_SEQ, d), 1.0),
        'p_prompt': _normal(k[2], (DEPTH, BATCH, SEQ, P_DIM), 1.0),
        'p_sample': _normal(k[3], (DEPTH, DEC_BATCH, DEC_SEQ, P_DIM), 1.0),
        'cache_k': _normal(k[4], (DEPTH, n_pool, PAGE_SIZE, KV_HEADS, HEAD_DIM), 1.0),
        'cache_v': _normal(k[5], (DEPTH, n_pool, PAGE_SIZE, KV_HEADS, HEAD_DIM), 1.0),
        'cache_idx_k': _normal(k[6], (DEPTH, n_pool, PAGE_SIZE, IDX_DIM), 1.0),
        'state_hgrn': _normal(k[7], (DEPTH, DEC_BATCH, HG_HEADS, HG_KDIM, HG_VDIM), 1.0),
        'page_table': page_table,
        'n_f1_pre': _gain(k[9], (DEPTH, d)),
        'n_f1_post': _gain(k[10], (DEPTH, d)),
        'w_f1_gate': _normal(k[11], (DEPTH, d, D_FF), d ** -0.5),
        'w_f1_up': _normal(k[12], (DEPTH, d, D_FF), d ** -0.5),
        'w_f1_down': _normal(k[13], (DEPTH, D_FF, d), D_FF ** -0.5),
        'n_mix_pre': _gain(k[14], (DEPTH, d)),
        'n_mix_post': _gain(k[15], (DEPTH, d)),
        'w_in': _normal(k[16], (DEPTH, d, IN_WIDTH), d ** -0.5),
        'hg_lb_logits': _normal(k[17], (DEPTH, HG_HEADS * HG_KDIM), 0.5),
        'hg_norm': _gain(k[18], (DEPTH, HG_VDIM)),
        'w_out': _normal(k[19], (DEPTH, MIX_WIDTH, d), MIX_WIDTH ** -0.5),
        'n_f2_pre': _gain(k[20], (DEPTH, d)),
        'n_f2_post': _gain(k[21], (DEPTH, d)),
        'w_f2_gate': _normal(k[22], (DEPTH, d, D_FF), d ** -0.5),
        'w_f2_up': _normal(k[23], (DEPTH, d, D_FF), d ** -0.5),
        'w_f2_down': _normal(k[24], (DEPTH, D_FF, d), D_FF ** -0.5),
        'n_ple_pre': _gain(k[25], (DEPTH, d)),
        'n_ple_post': _gain(k[26], (DEPTH, d)),
        'w_ple_proj': _normal(k[27], (DEPTH, P_DIM, d), P_DIM ** -0.5),
        'w_ple_gate': _normal(k[28], (DEPTH, d, d), d ** -0.5),
    }


def reference(x_prompt, x_sample, p_prompt, p_sample, cache_k, cache_v, cache_idx_k, state_hgrn, page_table,
              n_f1_pre, n_f1_post, w_f1_gate, w_f1_up, w_f1_down,
              n_mix_pre, n_mix_post, w_in, hg_lb_logits, hg_norm, w_out,
              n_f2_pre, n_f2_post, w_f2_gate, w_f2_up, w_f2_down,
              n_ple_pre, n_ple_post, w_ple_proj, w_ple_gate):
    bp, tp = x_prompt.shape[:2]
    ts = x_sample.shape[1]
    past_len = page_table.shape[1] * cache_k.shape[2]
    pos_p = jnp.arange(tp, dtype=jnp.int32)
    pos_s = past_len + jnp.arange(ts, dtype=jnp.int32)
    lbs = hgrn_lower_bounds(hg_lb_logits)
    s0_prompt = jnp.zeros((bp, HG_HEADS, HG_KDIM, HG_VDIM), jnp.float32)
    hp, hs = x_prompt, x_sample
    kp_l, vp_l, ikp_l, sp_l = [], [], [], []
    ks_l, vs_l, iks_l, ss_l = [], [], [], []
    for i in range(DEPTH):
        lw = (n_f1_pre[i], n_f1_post[i], w_f1_gate[i], w_f1_up[i], w_f1_down[i],
              n_mix_pre[i], n_mix_post[i], w_in[i], hg_norm[i], w_out[i],
              n_f2_pre[i], n_f2_post[i], w_f2_gate[i], w_f2_up[i], w_f2_down[i],
              n_ple_pre[i], n_ple_post[i], w_ple_proj[i], w_ple_gate[i])
        hp, k1, v1, ik1, s1 = decoder_layer(hp, p_prompt[i], pos_p, None, None, None,
                                            s0_prompt, lbs[i], *lw)
        hs, k2, v2, ik2, s2 = decoder_layer(hs, p_sample[i], pos_s,
                                            gather_pages(cache_k[i], page_table),
                                            gather_pages(cache_v[i], page_table),
                                            gather_pages(cache_idx_k[i], page_table),
                                            state_hgrn[i].astype(jnp.float32), lbs[i], *lw)
        kp_l.append(k1); vp_l.append(v1); ikp_l.append(ik1); sp_l.append(s1.astype(x_prompt.dtype))
        ks_l.append(k2); vs_l.append(v2); iks_l.append(ik2); ss_l.append(s2.astype(state_hgrn.dtype))
    k_prompt = jnp.stack(kp_l)
    v_prompt = jnp.stack(vp_l)
    idx_k_prompt = jnp.stack(ikp_l)
    hgrn_prompt = jnp.stack(sp_l)
    k_sample = jnp.stack(ks_l)
    v_sample = jnp.stack(vs_l)
    idx_k_sample = jnp.stack(iks_l)
    hgrn_sample = jnp.stack(ss_l)
    return (hp, hs, k_prompt, v_prompt, idx_k_prompt, hgrn_prompt, k_sample, v_sample, idx_k_sample, hgrn_sample)
```

```python
import functools

import jax
import jax.numpy as jnp
from jax import lax
from jax.experimental import pallas as pl
from jax.experimental.pallas import tpu as pltpu

HEAD_DIM = 64
N_HEADS = 8
KV_HEADS = 2
IDX_HEADS = 8
IDX_DIM = 64
TOPK_MAX = 256
HG_KDIM = 128
HG_VDIM = 128
HG_HEADS = 4
HG_CHUNK = 64
HG_SUB = 16
ROPE_THETA = 10000.0
EPS = 1e-6
ATT_WIDTH = N_HEADS * HEAD_DIM
HG_WIDTH = HG_HEADS * HG_VDIM

LANES = 128
SUBLANES = 8
VMEM_BUDGET_BYTES = 56 * 1024 * 1024

NEG_BIG = -1e30
INT32_MIN = -2147483648

F32 = jnp.float32
BF16 = jnp.bfloat16


def _cparams(sem, vmem_bytes):
    return pltpu.CompilerParams(dimension_semantics=sem,
                                vmem_limit_bytes=int(min(max(vmem_bytes, 16 << 20), VMEM_BUDGET_BYTES)))


def _rms(x, g):
    return x * lax.rsqrt(jnp.mean(x * x, axis=-1, keepdims=True) + EPS) * g


def _silu(x):
    return x * jax.nn.sigmoid(x)


def _dot(a, b):
    return jnp.dot(a, b, preferred_element_type=F32)


def _dot_nt(a, b):
    return lax.dot_general(a, b, (((1,), (1,)), ((), ())), preferred_element_type=F32)


def _dot_tn(a, b):
    return lax.dot_general(a, b, (((0,), (0,)), ((), ())), preferred_element_type=F32)


def _ffn_body(x_ref, gpre_ref, gpost_ref, wg_ref, wu_ref, wd_ref, o_ref, xn_ref, acc_ref):
    j = pl.program_id(1)

    @pl.when(j == 0)
    def _():
        xn_ref[...] = _rms(x_ref[...], gpre_ref[...]).astype(BF16)
        acc_ref[...] = jnp.zeros_like(acc_ref)

    xn = xn_ref[...]
    a = _dot(xn, wg_ref[...])
    b = _dot(xn, wu_ref[...])
    acc_ref[...] += _dot((_silu(a) * b).astype(BF16), wd_ref[...])

    @pl.when(j == pl.num_programs(1) - 1)
    def _():
        o_ref[...] = x_ref[...] + 0.5 * _rms(acc_ref[...], gpost_ref[...])


def _ffn(x, g_pre, g_post, wg, wu, wd, *, tm, tf):
    n, d = x.shape
    dff = wg.shape[1]
    vmem = 2 * (2 * tm * d * 4 + 3 * d * tf * 2) + tm * d * 6 + 3 * tm * tf * 4 + (4 << 20)
    return pl.pallas_call(
        _ffn_body,
        grid=(n // tm, dff // tf),
        in_specs=[
            pl.BlockSpec((tm, d), lambda i, j: (i, 0)),
            pl.BlockSpec((1, d), lambda i, j: (0, 0)),
            pl.BlockSpec((1, d), lambda i, j: (0, 0)),
            pl.BlockSpec((d, tf), lambda i, j: (0, j)),
            pl.BlockSpec((d, tf), lambda i, j: (0, j)),
            pl.BlockSpec((tf, d), lambda i, j: (j, 0)),
        ],
        out_specs=pl.BlockSpec((tm, d), lambda i, j: (i, 0)),
        out_shape=jax.ShapeDtypeStruct((n, d), F32),
        scratch_shapes=[pltpu.VMEM((tm, d), BF16), pltpu.VMEM((tm, d), F32)],
        compiler_params=_cparams(("parallel", "arbitrary"), vmem),
        name="ffn",
    )(x, g_pre, g_post, wg, wu, wd)


_C_Q, _C_K, _C_IQ, _C_IKW, _C_V, _C_HQ, _C_HF, _C_HI, _C_HG = 0, 512, 640, 1152, 1280, 1408, 1920, 2432, 2944
_C_QR, _C_KR, _C_IQR, _C_IKR, _C_END = 3456, 3968, 4096, 4608, 4736


def _proj_body(h_ref, g_ref, w_ref, cos_ref, sin_ref,
               q_ref, k_ref, v_ref, iq_ref, ikw_ref, hq_ref, hf_ref, hi_ref, hg_ref):
    u = _rms(h_ref[...], g_ref[...]).astype(BF16)

    def mm(lo, hi):
        return _dot(u, w_ref[:, lo:hi])

    cos = cos_ref[...]
    sin = sin_ref[...]
    cos4 = jnp.concatenate([cos] * 4, axis=1)
    sin4 = jnp.concatenate([sin] * 4, axis=1)
    q_ref[...] = mm(_C_Q, _C_K) * cos4 + mm(_C_QR, _C_KR) * sin4
    k_ref[...] = mm(_C_K, _C_IQ) * cos + mm(_C_KR, _C_IQR) * sin
    iq_ref[...] = mm(_C_IQ, _C_IKW) * cos4 + mm(_C_IQR, _C_IKR) * sin4
    lane = lax.broadcasted_iota(jnp.int32, cos.shape, 1)
    cos_ikw = jnp.where(lane < IDX_DIM, cos, IDX_HEADS ** -0.5)
    ikw_ref[...] = mm(_C_IKW, _C_V) * cos_ikw + mm(_C_IKR, _C_END) * sin
    v_ref[...] = mm(_C_V, _C_HQ)
    hq_ref[...] = mm(_C_HQ, _C_HF)
    hf_ref[...] = mm(_C_HF, _C_HI)
    hi_ref[...] = mm(_C_HI, _C_HG)
    hg_ref[...] = mm(_C_HG, _C_QR)


def _proj(h, g, w_cat, cos_tab, sin_tab, *, tm, tiles_per_seq, n_prompt_tiles):
    n, d = h.shape
    widths = (512, 128, 128, 512, 128, 512, 512, 512, 512)

    def tab_map(i):
        return (jnp.where(i < n_prompt_tiles, i % tiles_per_seq, tiles_per_seq), 0, 0)

    out_cols = sum(widths)
    vmem = 2 * (tm * d * 4 + d * _C_END * 2 + tm * out_cols * 4 + 2 * tm * LANES * 4) + 6 * tm * 512 * 4 + (4 << 20)
    return pl.pallas_call(
        _proj_body,
        grid=(n // tm,),
        in_specs=[
            pl.BlockSpec((tm, d), lambda i: (i, 0)),
            pl.BlockSpec((1, d), lambda i: (0, 0)),
            pl.BlockSpec((d, _C_END), lambda i: (0, 0)),
            pl.BlockSpec((None, tm, LANES), tab_map),
            pl.BlockSpec((None, tm, LANES), tab_map),
        ],
        out_specs=[pl.BlockSpec((tm, w), lambda i: (i, 0)) for w in widths],
        out_shape=[jax.ShapeDtypeStruct((n, w), F32) for w in widths],
        compiler_params=_cparams(("parallel",), vmem),
        name="proj",
    )(h, g, w_cat, cos_tab, sin_tab)


def _prefix_count(eq, tri):
    r, l = eq.shape
    carry = jnp.zeros((r, 1), F32)
    outs = []
    for c in range(l // LANES):
        e = jnp.where(eq[:, c * LANES:(c + 1) * LANES], 1.0, 0.0).astype(BF16)
        w = _dot(e, tri)
        outs.append(w + carry)
        carry = carry + w[:, LANES - 1:LANES]
    return jnp.concatenate(outs, axis=1)


def _topk_mask(score, adm, k, tri):
    score = jnp.where(score == 0.0, 0.0, score)
    score = jnp.where(adm, score, -jnp.inf)
    bits = lax.bitcast_convert_type(score, jnp.int32)
    key = bits ^ ((bits >> 31) & 0x7FFFFFFF)
    rows = score.shape[0]

    def body(i, t):
        cand = t + lax.shift_left(jnp.int32(1), 31 - i)
        cnt = jnp.sum(jnp.where(key >= cand, 1.0, 0.0), axis=-1, keepdims=True)
        return jnp.where(cnt >= k, cand, t)

    t = lax.fori_loop(0, 32, body, jnp.full((rows, 1), INT32_MIN, jnp.int32))
    gt = key > t
    eq = key == t
    need = k - jnp.sum(jnp.where(gt, 1.0, 0.0), axis=-1, keepdims=True)
    pref = _prefix_count(eq, tri)
    return jnp.logical_and(jnp.logical_or(gt, jnp.logical_and(eq, pref <= need)), adm)


def _masked_attention(qh, kh, vh, sel):
    s = jnp.where(sel, _dot_nt(qh, kh), NEG_BIG)
    m = jnp.max(s, axis=-1, keepdims=True)
    p = jnp.exp(s - m)
    l = jnp.sum(p, axis=-1, keepdims=True)
    return _dot(p.astype(BF16), vh) / l


def _tri_incl(n):
    a = lax.broadcasted_iota(jnp.int32, (n, n), 0)
    b = lax.broadcasted_iota(jnp.int32, (n, n), 1)
    return a, b


def _dsa_prompt_body(q_ref, iq_ref, iwq_ref, k_ref, v_ref, ikk_ref, o_ref, *, qb, topk):
    qi = pl.program_id(1)
    l = k_ref.shape[0]
    a, b = _tri_incl(LANES)
    tri = jnp.where(a <= b, 1.0, 0.0).astype(BF16)

    iq = iq_ref[...].astype(BF16)
    ik = ikk_ref[:, 0:IDX_DIM].astype(BF16)
    w = iwq_ref[:, IDX_DIM:IDX_DIM + IDX_HEADS]
    score = jnp.zeros((qb, l), F32)
    for h in range(IDX_HEADS):
        lg = _dot_nt(iq[:, h * IDX_DIM:(h + 1) * IDX_DIM], ik) * (IDX_DIM ** -0.5)
        score = score + w[:, h:h + 1] * jnp.maximum(lg, 0.0)

    qpos = qi * qb + lax.broadcasted_iota(jnp.int32, (qb, l), 0)
    kpos = lax.broadcasted_iota(jnp.int32, (qb, l), 1)
    sel = _topk_mask(score, kpos <= qpos, topk, tri)

    q = (q_ref[...] * (HEAD_DIM ** -0.5)).astype(BF16)
    kk = k_ref[...].astype(BF16)
    vv = v_ref[...].astype(BF16)
    group = N_HEADS // KV_HEADS
    for h in range(N_HEADS):
        g = h // group
        o = _masked_attention(q[:, h * HEAD_DIM:(h + 1) * HEAD_DIM],
                              kk[:, g * HEAD_DIM:(g + 1) * HEAD_DIM],
                              vv[:, g * HEAD_DIM:(g + 1) * HEAD_DIM], sel)
        o_ref[:, h * HEAD_DIM:(h + 1) * HEAD_DIM] = o


def _dsa_prompt(q, iq, ikw, k, v, *, batch, seq, qb):
    topk = min(TOPK_MAX, seq // 4)
    nq = seq // qb
    kv_w = KV_HEADS * HEAD_DIM
    vmem = 2 * (3 * qb * 512 * 4 + qb * LANES * 4 + 3 * seq * LANES * 4) + 16 * qb * seq * 4 + (4 << 20)
    return pl.pallas_call(
        functools.partial(_dsa_prompt_body, qb=qb, topk=topk),
        grid=(batch, nq),
        in_specs=[
            pl.BlockSpec((qb, ATT_WIDTH), lambda bi, qi: (bi * nq + qi, 0)),
            pl.BlockSpec((qb, IDX_HEADS * IDX_DIM), lambda bi, qi: (bi * nq + qi, 0)),
            pl.BlockSpec((qb, LANES), lambda bi, qi: (bi * nq + qi, 0)),
            pl.BlockSpec((seq, kv_w), lambda bi, qi: (bi, 0)),
            pl.BlockSpec((seq, kv_w), lambda bi, qi: (bi, 0)),
            pl.BlockSpec((seq, LANES), lambda bi, qi: (bi, 0)),
        ],
        out_specs=pl.BlockSpec((qb, ATT_WIDTH), lambda bi, qi: (bi * nq + qi, 0)),
        out_shape=jax.ShapeDtypeStruct((batch * seq, ATT_WIDTH), F32),
        compiler_params=_cparams(("parallel", "parallel"), vmem),
        name="dsa_prompt",
    )(q, iq, ikw, k, v, ikw)


def _dsa_sample_body(pt_ref, q_ref, iq_ref, ikw_ref, kn_ref, vn_ref, *rest, n_pages, page, tq, topk):
    del pt_ref
    kp_refs = rest[0:n_pages]
    vp_refs = rest[n_pages:2 * n_pages]
    ip_refs = rest[2 * n_pages:3 * n_pages]
    o_ref = rest[3 * n_pages]
    past = n_pages * page
    l = past + LANES
    a, b = _tri_incl(LANES)
    tri = jnp.where(a <= b, 1.0, 0.0).astype(BF16)

    def pad_rows(x):
        return jnp.concatenate([x, jnp.zeros((LANES - tq, x.shape[1]), x.dtype)], axis=0)

    ikw = ikw_ref[...]
    ik_all = jnp.concatenate([r[...] for r in ip_refs] + [pad_rows(ikw[:, 0:IDX_DIM])], axis=0).astype(BF16)
    k_all = jnp.concatenate([r[...] for r in kp_refs] + [pad_rows(kn_ref[...])], axis=0).astype(BF16)
    v_all = jnp.concatenate([r[...] for r in vp_refs] + [pad_rows(vn_ref[...])], axis=0).astype(BF16)

    iq = iq_ref[...]
    iq_hm = jnp.concatenate([iq[:, h * IDX_DIM:(h + 1) * IDX_DIM] for h in range(IDX_HEADS)], axis=0).astype(BF16)
    lg = jnp.maximum(_dot_nt(iq_hm, ik_all) * (IDX_DIM ** -0.5), 0.0)
    w = ikw[:, IDX_DIM:IDX_DIM + IDX_HEADS]
    score = jnp.zeros((tq, l), F32)
    for h in range(IDX_HEADS):
        score = score + w[:, h:h + 1] * lg[h * tq:(h + 1) * tq, :]

    row = lax.broadcasted_iota(jnp.int32, (tq, l), 0)
    col = lax.broadcasted_iota(jnp.int32, (tq, l), 1)
    adm = col <= past + row
    adm = jnp.logical_and(adm, col < past + tq)
    sel = _topk_mask(score, adm, topk, tri)

    q = q_ref[...] * (HEAD_DIM ** -0.5)
    group = N_HEADS // KV_HEADS
    sel_g = jnp.concatenate([sel] * group, axis=0)
    for g in range(KV_HEADS):
        qg = jnp.concatenate([q[:, h * HEAD_DIM:(h + 1) * HEAD_DIM] for h in range(g * group, (g + 1) * group)],
                             axis=0).astype(BF16)
        o = _masked_attention(qg, k_all[:, g * HEAD_DIM:(g + 1) * HEAD_DIM],
                              v_all[:, g * HEAD_DIM:(g + 1) * HEAD_DIM], sel_g)
        for j in range(group):
            h = g * group + j
            o_ref[:, h * HEAD_DIM:(h + 1) * HEAD_DIM] = o[j * tq:(j + 1) * tq, :]


def _dsa_sample(page_table, q, iq, ikw, k, v, cache_k, cache_v, cache_ik, *, layer, row0, batch, tq):
    n_pages = page_table.shape[1]
    page = cache_k.shape[2]
    kv_w = KV_HEADS * HEAD_DIM
    past = n_pages * page
    topk = min(TOPK_MAX, (past + tq) // 4)
    blk0 = row0 // tq
    pt = page_table.reshape(-1)

    def tok(w):
        return pl.BlockSpec((tq, w), lambda bi, pt_ref: (blk0 + bi, 0))

    def page_spec(w, p):
        return pl.BlockSpec((None, None, page, w), lambda bi, pt_ref: (layer, pt_ref[bi * n_pages + p], 0, 0))

    in_specs = [tok(ATT_WIDTH), tok(IDX_HEADS * IDX_DIM), tok(LANES), tok(kv_w), tok(kv_w)]
    in_specs += [page_spec(kv_w, p) for p in range(n_pages)]
    in_specs += [page_spec(kv_w, p) for p in range(n_pages)]
    in_specs += [page_spec(IDX_DIM, p) for p in range(n_pages)]
    vmem = 2 * n_pages * page * (2 * kv_w + LANES) * 4 + 12 * (past + LANES) * LANES * 4 + (8 << 20)
    return pl.pallas_call(
        functools.partial(_dsa_sample_body, n_pages=n_pages, page=page, tq=tq, topk=topk),
        grid_spec=pltpu.PrefetchScalarGridSpec(
            num_scalar_prefetch=1,
            grid=(batch,),
            in_specs=in_specs,
            out_specs=pl.BlockSpec((tq, ATT_WIDTH), lambda bi, pt_ref: (bi, 0)),
        ),
        out_shape=jax.ShapeDtypeStruct((batch * tq, ATT_WIDTH), F32),
        compiler_params=_cparams(("parallel",), vmem),
        name="dsa_sample",
    )(pt, q, iq, ikw, k, v, *([cache_k] * n_pages), *([cache_v] * n_pages), *([cache_ik] * n_pages))


def _hgrn_chunk(xq, xf, xi, lb, state, sub):
    c = xq.shape[0]
    f = lb + (1.0 - lb) * jax.nn.sigmoid(xf)
    logf = jnp.log(f)
    kk = 1.0 - f
    qf = _silu(xq) * (HG_KDIM ** -0.5)
    ta, tb = _tri_incl(c)
    tril = jnp.where(tb <= ta, 1.0, 0.0)
    b = jnp.dot(tril, logf, precision=lax.Precision.HIGHEST, preferred_element_type=F32)

    o_inter = _dot((qf * jnp.exp(b)).astype(BF16), state.astype(BF16))
    xi16 = xi.astype(BF16)
    rsub = lax.broadcasted_iota(jnp.int32, (sub, 1), 0)
    outs = []
    for i in range(c // sub):
        r0 = i * sub
        qi = qf[r0:r0 + sub]
        bi = b[r0:r0 + sub]
        od = jnp.zeros((sub, xi.shape[1]), F32)
        for s in range(sub):
            bs = b[r0 + s:r0 + s + 1]
            e = jnp.exp(jnp.minimum(bi - bs, 0.0))
            a = jnp.sum(qi * kk[r0 + s:r0 + s + 1] * e, axis=-1, keepdims=True)
            od = od + jnp.where(rsub >= s, a, 0.0) * xi[r0 + s:r0 + s + 1]
        if i > 0:
            b0 = b[r0 - 1:r0]
            qh = (qi * jnp.exp(bi - b0)).astype(BF16)
            kh = (kk[0:r0] * jnp.exp(b0 - b[0:r0])).astype(BF16)
            od = od + _dot(_dot_nt(qh, kh).astype(BF16), xi16[0:r0])
        outs.append(od)
    o = o_inter + (jnp.concatenate(outs, axis=0) if len(outs) > 1 else outs[0])

    blast = b[c - 1:c]
    kd = (kk * jnp.exp(blast - b)).astype(BF16)
    da, db = _tri_incl(state.shape[0])
    dcol = jnp.sum(jnp.where(da == db, jnp.exp(blast), 0.0), axis=-1, keepdims=True)
    new_state = state * dcol + _dot_tn(kd, xi16)
    return o, new_state


def _hgrn_gate(o, gnorm, xg):
    return _rms(o, gnorm) * _silu(xg)


def _hgrn_prompt_body(hq_ref, hf_ref, hi_ref, hg_ref, lb_ref, gn_ref, o_ref, s_ref, st_ref, *, chunk, sub):
    gi = pl.program_id(1)

    @pl.when(gi == 0)
    def _():
        st_ref[...] = jnp.zeros_like(st_ref)

    n_chunks = hq_ref.shape[0] // chunk
    gn = gn_ref[...]

    def step(ci, carry):
        rows = pl.ds(pl.multiple_of(ci * chunk, chunk), chunk)
        for h in range(HG_HEADS):
            cols = slice(h * HG_KDIM, (h + 1) * HG_KDIM)
            o, st = _hgrn_chunk(hq_ref[rows, cols], hf_ref[rows, cols], hi_ref[rows, cols],
                                lb_ref[:, cols], st_ref[h], sub)
            st_ref[h] = st
            o_ref[rows, cols] = _hgrn_gate(o, gn, hg_ref[rows, cols])
        return carry

    lax.fori_loop(0, n_chunks, step, 0)

    @pl.when(gi == pl.num_programs(1) - 1)
    def _():
        s_ref[...] = st_ref[...]


def _hgrn_prompt(hq, hf, hi, hg, lb, gnorm, *, batch, seq, rows_per_step):
    ng = seq // rows_per_step
    chunk = min(HG_CHUNK, seq)
    sub = min(HG_SUB, chunk)

    def tok():
        return pl.BlockSpec((rows_per_step, HG_WIDTH), lambda bi, gi: (bi * ng + gi, 0))

    vmem = 2 * 5 * rows_per_step * HG_WIDTH * 4 + 3 * HG_HEADS * HG_KDIM * HG_VDIM * 4 + (8 << 20)
    return pl.pallas_call(
        functools.partial(_hgrn_prompt_body, chunk=chunk, sub=sub),
        grid=(batch, ng),
        in_specs=[tok(), tok(), tok(), tok(),
                  pl.BlockSpec((1, HG_WIDTH), lambda bi, gi: (0, 0)),
                  pl.BlockSpec((1, HG_VDIM), lambda bi, gi: (0, 0))],
        out_specs=[tok(), pl.BlockSpec((None, HG_HEADS, HG_KDIM, HG_VDIM), lambda bi, gi: (bi, 0, 0, 0))],
        out_shape=[jax.ShapeDtypeStruct((batch * seq, HG_WIDTH), F32),
                   jax.ShapeDtypeStruct((batch, HG_HEADS, HG_KDIM, HG_VDIM), F32)],
        scratch_shapes=[pltpu.VMEM((HG_HEADS, HG_KDIM, HG_VDIM), F32)],
        compiler_params=_cparams(("parallel", "arbitrary"), vmem),
        name="hgrn_prompt",
    )(hq, hf, hi, hg, lb, gnorm)


def _hgrn_sample_body(hq_ref, hf_ref, hi_ref, hg_ref, lb_ref, gn_ref, s0_ref, o_ref, s_ref, *, tq, nb):
    gn = gn_ref[...]

    def step(bi, carry):
        rows = pl.ds(pl.multiple_of(bi * tq, tq), tq)
        for h in range(HG_HEADS):
            cols = slice(h * HG_KDIM, (h + 1) * HG_KDIM)
            o, st = _hgrn_chunk(hq_ref[rows, cols], hf_ref[rows, cols], hi_ref[rows, cols],
                                lb_ref[:, cols], s0_ref[bi, h], tq)
            s_ref[bi, h] = st
            o_ref[rows, cols] = _hgrn_gate(o, gn, hg_ref[rows, cols])
        return carry

    lax.fori_loop(0, nb, step, 0)


def _hgrn_sample(hq, hf, hi, hg, lb, gnorm, state, *, layer, row0, batch, tq, nb):
    assert tq <= HG_CHUNK
    blk0 = row0 // (nb * tq)

    def tok():
        return pl.BlockSpec((nb * tq, HG_WIDTH), lambda gi: (blk0 + gi, 0))

    st_blk = (nb, HG_HEADS, HG_KDIM, HG_VDIM)
    vmem = 2 * (5 * nb * tq * HG_WIDTH * 4 + 2 * nb * HG_HEADS * HG_KDIM * HG_VDIM * 4) + (8 << 20)
    return pl.pallas_call(
        functools.partial(_hgrn_sample_body, tq=tq, nb=nb),
        grid=(batch // nb,),
        in_specs=[tok(), tok(), tok(), tok(),
                  pl.BlockSpec((1, HG_WIDTH), lambda gi: (0, 0)),
                  pl.BlockSpec((1, HG_VDIM), lambda gi: (0, 0)),
                  pl.BlockSpec((None,) + st_blk, lambda gi: (layer, gi, 0, 0, 0))],
        out_specs=[pl.BlockSpec((nb * tq, HG_WIDTH), lambda gi: (gi, 0)),
                   pl.BlockSpec(st_blk, lambda gi: (gi, 0, 0, 0))],
        out_shape=[jax.ShapeDtypeStruct((batch * tq, HG_WIDTH), F32),
                   jax.ShapeDtypeStruct((batch, HG_HEADS, HG_KDIM, HG_VDIM), F32)],
        compiler_params=_cparams(("parallel",), vmem),
        name="hgrn_sample",
    )(hq, hf, hi, hg, lb, gnorm, state)


def _mix_body(att_ref, ohg_ref, h_ref, g_ref, w_ref, o_ref):
    mix = _dot(att_ref[...].astype(BF16), w_ref[0:ATT_WIDTH, :]) + \
        _dot(ohg_ref[...].astype(BF16), w_ref[ATT_WIDTH:ATT_WIDTH + HG_WIDTH, :])
    o_ref[...] = h_ref[...] + _rms(mix, g_ref[...])


def _mix(att, ohg, h, g, w_out, *, tm):
    n, d = h.shape
    vmem = 2 * (2 * tm * 512 * 4 + 2 * tm * d * 4 + (ATT_WIDTH + HG_WIDTH) * d * 2) + 2 * tm * d * 4 + (4 << 20)
    return pl.pallas_call(
        _mix_body,
        grid=(n // tm,),
        in_specs=[pl.BlockSpec((tm, ATT_WIDTH), lambda i: (i, 0)),
                  pl.BlockSpec((tm, HG_WIDTH), lambda i: (i, 0)),
                  pl.BlockSpec((tm, d), lambda i: (i, 0)),
                  pl.BlockSpec((1, d), lambda i: (0, 0)),
                  pl.BlockSpec((ATT_WIDTH + HG_WIDTH, d), lambda i: (0, 0))],
        out_specs=pl.BlockSpec((tm, d), lambda i: (i, 0)),
        out_shape=jax.ShapeDtypeStruct((n, d), F32),
        compiler_params=_cparams(("parallel",), vmem),
        name="mix",
    )(att, ohg, h, g, w_out)


def _ple_body(h_ref, p_ref, gpre_ref, gpost_ref, wgate_ref, wproj_ref, o_ref):
    h = h_ref[...]
    gate = jax.nn.sigmoid(_dot(_rms(h, gpre_ref[...]).astype(BF16), wgate_ref[...]))
    pe = _dot(p_ref[...].astype(BF16), wproj_ref[...])
    o_ref[...] = h + _rms(pe * gate, gpost_ref[...])


def _ple(h, p, g_pre, g_post, w_gate, w_proj, *, tm):
    n, d = h.shape
    pd = p.shape[1]
    vmem = 2 * (2 * tm * d * 4 + tm * pd * 4 + d * d * 2 + pd * d * 2) + 3 * tm * d * 4 + (4 << 20)
    return pl.pallas_call(
        _ple_body,
        grid=(n // tm,),
        in_specs=[pl.BlockSpec((tm, d), lambda i: (i, 0)),
                  pl.BlockSpec((tm, pd), lambda i: (i, 0)),
                  pl.BlockSpec((1, d), lambda i: (0, 0)),
                  pl.BlockSpec((1, d), lambda i: (0, 0)),
                  pl.BlockSpec((d, d), lambda i: (0, 0)),
                  pl.BlockSpec((pd, d), lambda i: (0, 0))],
        out_specs=pl.BlockSpec((tm, d), lambda i: (i, 0)),
        out_shape=jax.ShapeDtypeStruct((n, d), F32),
        compiler_params=_cparams(("parallel",), vmem),
        name="ple",
    )(h, p, g_pre, g_post, w_gate, w_proj)


def _rot_cols(w, heads):
    d = w.shape[0]
    w4 = w.reshape(d, heads, 2, HEAD_DIM // 2)
    return jnp.concatenate([-w4[:, :, 1:2], w4[:, :, 0:1]], axis=2).reshape(d, heads * HEAD_DIM)


def _pack_w_in(w):
    d = w.shape[0]
    widths = (ATT_WIDTH, KV_HEADS * HEAD_DIM, KV_HEADS * HEAD_DIM, IDX_HEADS * IDX_DIM, IDX_DIM, IDX_HEADS,
              HG_HEADS * HG_KDIM, HG_HEADS * HG_KDIM, HG_WIDTH, HG_WIDTH)
    parts = []
    acc = 0
    for wd in widths:
        parts.append(w[:, acc:acc + wd])
        acc += wd
    q, k, v, iq, ik, iw, hq, hf, hi, hg = parts
    z = lambda n: jnp.zeros((d, n), w.dtype)
    ikw = jnp.concatenate([ik, iw, z(LANES - IDX_DIM - IDX_HEADS)], axis=1)
    ikr = jnp.concatenate([_rot_cols(ik, 1), z(LANES - IDX_DIM)], axis=1)
    cat = jnp.concatenate([q, k, iq, ikw, v, hq, hf, hi, hg,
                           _rot_cols(q, N_HEADS), _rot_cols(k, KV_HEADS), _rot_cols(iq, IDX_HEADS), ikr], axis=1)
    assert cat.shape[1] == _C_END
    return cat.astype(BF16)


def _rope_tables(seq, past_len, tq, tm):
    half = HEAD_DIM // 2
    inv_freq = ROPE_THETA ** (-jnp.arange(half, dtype=F32) / half)
    pos_p = jnp.arange(seq, dtype=jnp.int32)
    pos_s = past_len + (jnp.arange(tm, dtype=jnp.int32) % tq)
    pos = jnp.concatenate([pos_p, pos_s]).astype(F32)
    ang = pos[:, None] * inv_freq[None, :]
    reps = LANES // half
    cos = jnp.tile(jnp.cos(ang), (1, reps)).reshape(seq // tm + 1, tm, LANES)
    sin = jnp.tile(jnp.sin(ang), (1, reps)).reshape(seq // tm + 1, tm, LANES)
    return cos, sin


def _lower_bounds(logits):
    sm = jax.nn.softmax(logits.astype(F32), axis=0)
    return jnp.cumsum(sm, axis=0) - sm[0:1]


TM_DENSE = 1024
TM_PROJ = 512
TF_FFN = 256
Q_BLOCK = 128
HG_ROWS = 512
HG_SEQS = 8


def kernel(x_prompt, x_sample, p_prompt, p_sample, cache_k, cache_v, cache_idx_k, state_hgrn, page_table,
           n_f1_pre, n_f1_post, w_f1_gate, w_f1_up, w_f1_down,
           n_mix_pre, n_mix_post, w_in, hg_lb_logits, hg_norm, w_out,
           n_f2_pre, n_f2_post, w_f2_gate, w_f2_up, w_f2_down,
           n_ple_pre, n_ple_post, w_ple_proj, w_ple_gate):
    bp, tp, d = x_prompt.shape
    bs, ts, _ = x_sample.shape
    depth = w_in.shape[0]
    n_p, n_s = bp * tp, bs * ts
    page = cache_k.shape[2]
    past_len = page_table.shape[1] * page
    pool = cache_k.shape[1]
    kv_w = KV_HEADS * HEAD_DIM

    h = jnp.concatenate([x_prompt.reshape(n_p, d), x_sample.reshape(n_s, d)], axis=0)
    ck = cache_k.reshape(depth, pool, page, kv_w)
    cv = cache_v.reshape(depth, pool, page, kv_w)
    cos_tab, sin_tab = _rope_tables(tp, past_len, ts, TM_PROJ)
    lbs = _lower_bounds(hg_lb_logits)
    row = lambda a: a.reshape(1, -1)
    bf = lambda a: a.astype(BF16)

    outs = {name: [] for name in ("kp", "vp", "ikp", "sp", "ks", "vs", "iks", "ss")}
    for i in range(depth):
        h1 = _ffn(h, row(n_f1_pre[i]), row(n_f1_post[i]), bf(w_f1_gate[i]), bf(w_f1_up[i]), bf(w_f1_down[i]),
                  tm=TM_DENSE, tf=TF_FFN)
        q, k, v, iq, ikw, hq, hf, hi, hg = _proj(
            h1, row(n_mix_pre[i]), _pack_w_in(w_in[i]), cos_tab, sin_tab,
            tm=TM_PROJ, tiles_per_seq=tp // TM_PROJ, n_prompt_tiles=n_p // TM_PROJ)

        att_p = _dsa_prompt(q, iq, ikw, k, v, batch=bp, seq=tp, qb=min(Q_BLOCK, tp))
        att_s = _dsa_sample(page_table, q, iq, ikw, k, v, ck, cv, cache_idx_k,
                            layer=i, row0=n_p, batch=bs, tq=ts)
        lb = row(lbs[i])
        gn = row(hg_norm[i])
        ohg_p, s_p = _hgrn_prompt(hq, hf, hi, hg, lb, gn, batch=bp, seq=tp, rows_per_step=HG_ROWS)
        ohg_s, s_s = _hgrn_sample(hq, hf, hi, hg, lb, gn, state_hgrn,
                                  layer=i, row0=n_p, batch=bs, tq=ts, nb=HG_SEQS)

        att = jnp.concatenate([att_p, att_s], axis=0)
        ohg = jnp.concatenate([ohg_p, ohg_s], axis=0)
        h2 = _mix(att, ohg, h1, row(n_mix_post[i]), bf(w_out[i]), tm=TM_DENSE)
        h3 = _ffn(h2, row(n_f2_pre[i]), row(n_f2_post[i]), bf(w_f2_gate[i]), bf(w_f2_up[i]), bf(w_f2_down[i]),
                  tm=TM_DENSE, tf=TF_FFN)
        p_emb = jnp.concatenate([p_prompt[i].reshape(n_p, -1), p_sample[i].reshape(n_s, -1)], axis=0)
        h = _ple(h3, p_emb, row(n_ple_pre[i]), row(n_ple_post[i]), bf(w_ple_gate[i]), bf(w_ple_proj[i]),
                 tm=TM_DENSE)

        outs["kp"].append(k[:n_p].reshape(bp, tp, KV_HEADS, HEAD_DIM))
        outs["vp"].append(v[:n_p].reshape(bp, tp, KV_HEADS, HEAD_DIM))
        outs["ikp"].append(ikw[:n_p, :IDX_DIM].reshape(bp, tp, IDX_DIM))
        outs["sp"].append(s_p)
        outs["ks"].append(k[n_p:].reshape(bs, ts, KV_HEADS, HEAD_DIM))
        outs["vs"].append(v[n_p:].reshape(bs, ts, KV_HEADS, HEAD_DIM))
        outs["iks"].append(ikw[n_p:, :IDX_DIM].reshape(bs, ts, IDX_DIM))
        outs["ss"].append(s_s.astype(state_hgrn.dtype))

    st = lambda name: jnp.stack(outs[name])
    return (h[:n_p].reshape(bp, tp, d), h[n_p:].reshape(bs, ts, d),
            st("kp"), st("vp"), st("ikp"), st("sp"), st("ks"), st("vs"), st("iks"), st("ss"))
```

```python
import functools

import jax
import jax.numpy as jnp
from jax import lax
from jax.experimental import pallas as pl
from jax.experimental.pallas import tpu as pltpu

HEAD_DIM = 64
N_HEADS = 8
KV_HEADS = 2
IDX_HEADS = 8
IDX_DIM = 64
TOPK_MAX = 256
HG_KDIM = 128
HG_VDIM = 128
HG_HEADS = 4
HG_CHUNK = 64
HG_SUB = 16
ROPE_THETA = 10000.0
EPS = 1e-6
ATT_WIDTH = N_HEADS * HEAD_DIM
KV_WIDTH = KV_HEADS * HEAD_DIM
HG_WIDTH = HG_HEADS * HG_VDIM
QK_SCALE = HEAD_DIM ** -0.5
IDX_SCALE = IDX_DIM ** -0.5

LANES = 128
SUBLANES = 8
VMEM_BUDGET_BYTES = 56 * 1024 * 1024

NEG_BIG = -1e30
INT32_MIN = -2147483648

F32 = jnp.float32
BF16 = jnp.bfloat16


def _cparams(sem, vmem_bytes):
    return pltpu.CompilerParams(dimension_semantics=sem,
                                vmem_limit_bytes=int(min(max(vmem_bytes, 16 << 20), VMEM_BUDGET_BYTES)))


def _rms(x, g):
    return x * lax.rsqrt(jnp.mean(x * x, axis=-1, keepdims=True) + EPS) * g


def _silu(x):
    return x * jax.nn.sigmoid(x)


def _dot(a, b):
    return jnp.dot(a, b, preferred_element_type=F32)


def _dot_nt(a, b):
    return lax.dot_general(a, b, (((1,), (1,)), ((), ())), preferred_element_type=F32)


def _dot_tn(a, b):
    return lax.dot_general(a, b, (((0,), (0,)), ((), ())), preferred_element_type=F32)


def _ffn_body(x_ref, gpre_ref, gpost_ref, wg_ref, wu_ref, wd_ref, o_ref, xn_ref, acc_ref):
    j = pl.program_id(1)

    @pl.when(j == 0)
    def _():
        xn_ref[...] = _rms(x_ref[...], gpre_ref[...]).astype(BF16)
        acc_ref[...] = jnp.zeros_like(acc_ref)

    xn = xn_ref[...]
    a = _dot(xn, wg_ref[...])
    b = _dot(xn, wu_ref[...])
    acc_ref[...] += _dot((_silu(a) * b).astype(BF16), wd_ref[...])

    @pl.when(j == pl.num_programs(1) - 1)
    def _():
        o_ref[...] = x_ref[...] + 0.5 * _rms(acc_ref[...], gpost_ref[...])


def _ffn(x, g_pre, g_post, wg, wu, wd, *, tm, tf):
    n, d = x.shape
    dff = wg.shape[1]
    vmem = 2 * (2 * tm * d * 4 + 3 * d * tf * 2) + tm * d * 6 + 3 * tm * tf * 4 + (4 << 20)
    return pl.pallas_call(
        _ffn_body,
        grid=(n // tm, dff // tf),
        in_specs=[
            pl.BlockSpec((tm, d), lambda i, j: (i, 0)),
            pl.BlockSpec((1, d), lambda i, j: (0, 0)),
            pl.BlockSpec((1, d), lambda i, j: (0, 0)),
            pl.BlockSpec((d, tf), lambda i, j: (0, j)),
            pl.BlockSpec((d, tf), lambda i, j: (0, j)),
            pl.BlockSpec((tf, d), lambda i, j: (j, 0)),
        ],
        out_specs=pl.BlockSpec((tm, d), lambda i, j: (i, 0)),
        out_shape=jax.ShapeDtypeStruct((n, d), F32),
        scratch_shapes=[pltpu.VMEM((tm, d), BF16), pltpu.VMEM((tm, d), F32)],
        compiler_params=_cparams(("parallel", "arbitrary"), vmem),
        name="ffn",
    )(x, g_pre, g_post, wg, wu, wd)


_C_Q, _C_K, _C_IQ, _C_IKW, _C_V, _C_HQ, _C_HF, _C_HI, _C_HG = 0, 512, 640, 1152, 1280, 1408, 1920, 2432, 2944
_C_QR, _C_KR, _C_IQR, _C_IKR, _C_END = 3456, 3968, 4096, 4608, 4736


def _proj_body(h_ref, g_ref, w_ref, cos_ref, sin_ref,
               q_ref, k_ref, v_ref, iq_ref, ikw_ref, hq_ref, hf_ref, hi_ref, hg_ref):
    u = _rms(h_ref[...], g_ref[...]).astype(BF16)

    def mm(lo, hi):
        return _dot(u, w_ref[:, lo:hi])

    cos = cos_ref[...]
    sin = sin_ref[...]
    cos4 = jnp.concatenate([cos] * 4, axis=1)
    sin4 = jnp.concatenate([sin] * 4, axis=1)
    q_ref[...] = ((mm(_C_Q, _C_K) * cos4 + mm(_C_QR, _C_KR) * sin4) * QK_SCALE).astype(BF16)
    k_ref[...] = mm(_C_K, _C_IQ) * cos + mm(_C_KR, _C_IQR) * sin
    iq_ref[...] = ((mm(_C_IQ, _C_IKW) * cos4 + mm(_C_IQR, _C_IKR) * sin4) * IDX_SCALE).astype(BF16)
    lane = lax.broadcasted_iota(jnp.int32, cos.shape, 1)
    cos_ikw = jnp.where(lane < IDX_DIM, cos, IDX_HEADS ** -0.5)
    ikw_ref[...] = mm(_C_IKW, _C_V) * cos_ikw + mm(_C_IKR, _C_END) * sin
    v_ref[...] = mm(_C_V, _C_HQ)
    hq_ref[...] = mm(_C_HQ, _C_HF)
    hf_ref[...] = mm(_C_HF, _C_HI)
    hi_ref[...] = mm(_C_HI, _C_HG)
    hg_ref[...] = mm(_C_HG, _C_QR)


def _proj(h, g, w_cat, cos_tab, sin_tab, *, tm, tiles_per_seq, n_prompt_tiles):
    n, d = h.shape
    outs = ((512, BF16), (128, F32), (128, F32), (512, BF16), (128, F32),
            (512, F32), (512, F32), (512, F32), (512, F32))

    def tab_map(i):
        return (jnp.where(i < n_prompt_tiles, i % tiles_per_seq, tiles_per_seq), 0, 0)

    out_bytes = sum(w * jnp.dtype(t).itemsize for w, t in outs)
    vmem = 2 * (tm * d * 4 + d * _C_END * 2 + tm * out_bytes + 2 * tm * LANES * 4) + 6 * tm * 512 * 4 + (4 << 20)
    return pl.pallas_call(
        _proj_body,
        grid=(n // tm,),
        in_specs=[
            pl.BlockSpec((tm, d), lambda i: (i, 0)),
            pl.BlockSpec((1, d), lambda i: (0, 0)),
            pl.BlockSpec((d, _C_END), lambda i: (0, 0)),
            pl.BlockSpec((None, tm, LANES), tab_map),
            pl.BlockSpec((None, tm, LANES), tab_map),
        ],
        out_specs=[pl.BlockSpec((tm, w), lambda i: (i, 0)) for w, _ in outs],
        out_shape=[jax.ShapeDtypeStruct((n, w), t) for w, t in outs],
        compiler_params=_cparams(("parallel",), vmem),
        name="proj",
    )(h, g, w_cat, cos_tab, sin_tab)


_R_K, _R_KR, _R_V, _R_IK, _R_IKR, _R_END = 0, 128, 256, 384, 448, 512


def _proj_t_body(h_ref, g_ref, w_ref, cos_ref, sin_ref, k_ref, v_ref, ik_ref, k16_ref, v16_ref, ik16_ref):
    u = _rms(h_ref[...], g_ref[...]).astype(BF16)
    r = _dot_nt(w_ref[...], u)
    cos = cos_ref[...]
    sin = sin_ref[...]
    k = r[_R_K:_R_KR] * cos + r[_R_KR:_R_V] * sin
    v = r[_R_V:_R_IK]
    ik = r[_R_IK:_R_IKR] * cos[0:IDX_DIM] + r[_R_IKR:_R_END] * sin[0:IDX_DIM]
    k_ref[...] = k
    v_ref[...] = v
    ik_ref[...] = ik
    k16_ref[...] = k.astype(BF16)
    v16_ref[...] = v.astype(BF16)
    ik16_ref[...] = ik.astype(BF16)


def _proj_t(h, g, w_t, cos_t, sin_t, *, batch, seq, tm):
    d = h.shape[1]
    nt = seq // tm
    outs = ((KV_WIDTH, F32), (KV_WIDTH, F32), (IDX_DIM, F32), (KV_WIDTH, BF16), (KV_WIDTH, BF16), (IDX_DIM, BF16))
    vmem = 2 * (tm * d * 4 + _R_END * d * 2 + 2 * LANES * tm * 4 + 3 * LANES * tm * 6) + 4 * _R_END * tm * 4 + (4 << 20)
    return pl.pallas_call(
        _proj_t_body,
        grid=(batch, nt),
        in_specs=[
            pl.BlockSpec((tm, d), lambda b, t: (b * nt + t, 0)),
            pl.BlockSpec((1, d), lambda b, t: (0, 0)),
            pl.BlockSpec((_R_END, d), lambda b, t: (0, 0)),
            pl.BlockSpec((None, LANES, tm), lambda b, t: (t, 0, 0)),
            pl.BlockSpec((None, LANES, tm), lambda b, t: (t, 0, 0)),
        ],
        out_specs=[pl.BlockSpec((None, w, tm), lambda b, t: (b, 0, t)) for w, _ in outs],
        out_shape=[jax.ShapeDtypeStruct((batch, w, seq), t) for w, t in outs],
        compiler_params=_cparams(("parallel", "parallel"), vmem),
        name="proj_t",
    )(h, g, w_t, cos_t, sin_t)


def _prefix_count(eq, tri):
    r, l = eq.shape
    carry = jnp.zeros((r, 1), F32)
    outs = []
    for c in range(l // LANES):
        e = jnp.where(eq[:, c * LANES:(c + 1) * LANES], 1.0, 0.0).astype(BF16)
        w = _dot(e, tri)
        outs.append(w + carry)
        carry = carry + w[:, LANES - 1:LANES]
    return jnp.concatenate(outs, axis=1)


def _topk_mask(score, adm, k, tri):
    score = jnp.where(score == 0.0, 0.0, score)
    score = jnp.where(adm, score, -jnp.inf)
    bits = lax.bitcast_convert_type(score, jnp.int32)
    key = bits ^ ((bits >> 31) & 0x7FFFFFFF)
    rows = score.shape[0]

    def body(i, t):
        cand = t + lax.shift_left(jnp.int32(1), 31 - i)
        cnt = jnp.sum(jnp.where(key >= cand, 1.0, 0.0), axis=-1, keepdims=True)
        return jnp.where(cnt >= k, cand, t)

    t = lax.fori_loop(0, 32, body, jnp.full((rows, 1), INT32_MIN, jnp.int32))
    gt = key > t
    eq = key == t
    need = k - jnp.sum(jnp.where(gt, 1.0, 0.0), axis=-1, keepdims=True)
    pref = _prefix_count(eq, tri)
    return jnp.logical_and(jnp.logical_or(gt, jnp.logical_and(eq, pref <= need)), adm)


def _softmax_rows(s, sel):
    s = jnp.where(sel, s, NEG_BIG)
    p = jnp.exp(s - jnp.max(s, axis=-1, keepdims=True))
    return p, jnp.sum(p, axis=-1, keepdims=True)


def _tri_incl(n):
    a = lax.broadcasted_iota(jnp.int32, (n, n), 0)
    b = lax.broadcasted_iota(jnp.int32, (n, n), 1)
    return a, b


def _count_tri():
    a, b = _tri_incl(LANES)
    return jnp.where(a <= b, 1.0, 0.0).astype(BF16)


def _dsa_prompt_tile(q_ref, iq_ref, iw_ref, kt_ref, vt_ref, ikt_ref, o_ref, *, q0, l, topk):
    qb = q_ref.shape[0]
    tri = _count_tri()
    iq = iq_ref[...]
    ikt = ikt_ref[:, 0:l]
    w = iw_ref[:, IDX_DIM:IDX_DIM + IDX_HEADS]
    score = jnp.zeros((qb, l), F32)
    for h in range(IDX_HEADS):
        lg = _dot(iq[:, h * IDX_DIM:(h + 1) * IDX_DIM], ikt)
        score = score + w[:, h:h + 1] * jnp.maximum(lg, 0.0)

    qpos = q0 + lax.broadcasted_iota(jnp.int32, (qb, l), 0)
    kpos = lax.broadcasted_iota(jnp.int32, (qb, l), 1)
    sel = _topk_mask(score, kpos <= qpos, topk, tri)

    q = q_ref[...]
    group = N_HEADS // KV_HEADS
    for h in range(N_HEADS):
        g = h // group
        s = _dot(q[:, h * HEAD_DIM:(h + 1) * HEAD_DIM], kt_ref[g * HEAD_DIM:(g + 1) * HEAD_DIM, 0:l])
        p, den = _softmax_rows(s, sel)
        o = _dot_nt(p.astype(BF16), vt_ref[g * HEAD_DIM:(g + 1) * HEAD_DIM, 0:l])
        o_ref[:, h * HEAD_DIM:(h + 1) * HEAD_DIM] = o / den


def _dsa_prompt_body(q_ref, iq_ref, iw_ref, kt_ref, vt_ref, ikt_ref, o_ref, *, topk, n_len):
    qi = pl.program_id(1)
    nq = pl.num_programs(1)
    qb = q_ref.shape[0]
    seq = kt_ref.shape[1]
    per = nq // n_len
    for v in range(n_len):
        l = (v + 1) * per * qb

        @pl.when(qi // per == v)
        def _(l=l):
            _dsa_prompt_tile(q_ref, iq_ref, iw_ref, kt_ref, vt_ref, ikt_ref, o_ref,
                             q0=qi * qb, l=min(l, seq), topk=topk)


def _dsa_prompt(q16, iq16, ikw, kt16, vt16, ikt16, *, batch, seq, qb, n_len):
    topk = min(TOPK_MAX, seq // 4)
    nq = seq // qb
    n_len = n_len if nq % n_len == 0 else 1
    vmem = 2 * (2 * qb * 512 * 2 + qb * LANES * 4 + qb * 512 * 4 + seq * (2 * KV_WIDTH + IDX_DIM) * 2) \
        + 16 * qb * seq * 4 + (4 << 20)
    return pl.pallas_call(
        functools.partial(_dsa_prompt_body, topk=topk, n_len=n_len),
        grid=(batch, nq),
        in_specs=[
            pl.BlockSpec((qb, ATT_WIDTH), lambda bi, qi: (bi * nq + qi, 0)),
            pl.BlockSpec((qb, IDX_HEADS * IDX_DIM), lambda bi, qi: (bi * nq + qi, 0)),
            pl.BlockSpec((qb, LANES), lambda bi, qi: (bi * nq + qi, 0)),
            pl.BlockSpec((None, KV_WIDTH, seq), lambda bi, qi: (bi, 0, 0)),
            pl.BlockSpec((None, KV_WIDTH, seq), lambda bi, qi: (bi, 0, 0)),
            pl.BlockSpec((None, IDX_DIM, seq), lambda bi, qi: (bi, 0, 0)),
        ],
        out_specs=pl.BlockSpec((qb, ATT_WIDTH), lambda bi, qi: (bi * nq + qi, 0)),
        out_shape=jax.ShapeDtypeStruct((batch * seq, ATT_WIDTH), F32),
        compiler_params=_cparams(("parallel", "parallel"), vmem),
        name="dsa_prompt",
    )(q16, iq16, ikw, kt16, vt16, ikt16)


def _dsa_sample_body(pt_ref, q_ref, iq_ref, ikw_ref, kn_ref, vn_ref, *rest, n_pages, page, tq, nb, topk):
    del pt_ref
    npg = nb * n_pages
    kp_refs, vp_refs, ip_refs = rest[0:npg], rest[npg:2 * npg], rest[2 * npg:3 * npg]
    o_ref = rest[3 * npg]
    past = n_pages * page
    l = past + LANES
    tri = _count_tri()
    group = N_HEADS // KV_HEADS

    def pad_rows(x):
        return jnp.concatenate([x, jnp.zeros((LANES - tq, x.shape[1]), x.dtype)], axis=0)

    def pages(refs, bj):
        return jnp.concatenate([refs[bj * n_pages + p][...] for p in range(n_pages)], axis=1).astype(BF16)

    iq_all = iq_ref[...].astype(F32)
    q_all = q_ref[...].astype(F32)
    ikw_all = ikw_ref[...]

    scores = []
    for bj in range(nb):
        rows = slice(bj * tq, (bj + 1) * tq)
        iq = iq_all[rows]
        iq_hm = jnp.concatenate([iq[:, h * IDX_DIM:(h + 1) * IDX_DIM] for h in range(IDX_HEADS)],
                                axis=0).astype(BF16)
        ik_new = pad_rows(ikw_all[rows, 0:IDX_DIM]).astype(BF16)
        lg = jnp.concatenate([_dot(iq_hm, pages(ip_refs, bj)), _dot_nt(iq_hm, ik_new)], axis=1)
        lg = jnp.maximum(lg, 0.0)
        w = ikw_all[rows, IDX_DIM:IDX_DIM + IDX_HEADS]
        sc = jnp.zeros((tq, l), F32)
        for h in range(IDX_HEADS):
            sc = sc + w[:, h:h + 1] * lg[h * tq:(h + 1) * tq, :]
        scores.append(sc)
    score = jnp.concatenate(scores, axis=0) if nb > 1 else scores[0]

    row = lax.broadcasted_iota(jnp.int32, (nb * tq, l), 0) % tq
    col = lax.broadcasted_iota(jnp.int32, (nb * tq, l), 1)
    adm = col <= past + row
    adm = jnp.logical_and(adm, col < past + tq)
    sel = _topk_mask(score, adm, topk, tri)

    for bj in range(nb):
        rows = slice(bj * tq, (bj + 1) * tq)
        q = q_all[rows]
        sel_g = jnp.concatenate([sel[rows]] * group, axis=0)
        kt = pages(kp_refs, bj)
        vt = pages(vp_refs, bj)
        k_new = pad_rows(kn_ref[rows, :]).astype(BF16)
        v_new = pad_rows(vn_ref[rows, :]).astype(BF16)
        for g in range(KV_HEADS):
            fs = slice(g * HEAD_DIM, (g + 1) * HEAD_DIM)
            qg = jnp.concatenate([q[:, h * HEAD_DIM:(h + 1) * HEAD_DIM]
                                  for h in range(g * group, (g + 1) * group)], axis=0).astype(BF16)
            s = jnp.concatenate([_dot(qg, kt[fs]), _dot_nt(qg, k_new[:, fs])], axis=1)
            p, den = _softmax_rows(s, sel_g)
            p = p.astype(BF16)
            o = (_dot_nt(p[:, 0:past], vt[fs]) + _dot(p[:, past:l], v_new[:, fs])) / den
            for j in range(group):
                h = g * group + j
                o_ref[rows, h * HEAD_DIM:(h + 1) * HEAD_DIM] = o[j * tq:(j + 1) * tq, :]


def _dsa_sample(page_table, q16, iq16, ikw, k, v, cache_kt, cache_vt, cache_ikt, *, layer, row0, batch, tq, nb):
    n_pages = page_table.shape[1]
    page = cache_kt.shape[3]
    past = n_pages * page
    topk = min(TOPK_MAX, (past + tq) // 4)
    blk0 = row0 // (nb * tq)
    pt = page_table.reshape(-1)

    def tok(w):
        return pl.BlockSpec((nb * tq, w), lambda gi, pt_ref: (blk0 + gi, 0))

    def page_spec(w, bj, p):
        return pl.BlockSpec((None, None, w, page),
                            lambda gi, pt_ref: (layer, pt_ref[(gi * nb + bj) * n_pages + p], 0, 0))

    in_specs = [tok(ATT_WIDTH), tok(IDX_HEADS * IDX_DIM), tok(LANES), tok(KV_WIDTH), tok(KV_WIDTH)]
    for w in (KV_WIDTH, KV_WIDTH, IDX_DIM):
        in_specs += [page_spec(w, bj, p) for bj in range(nb) for p in range(n_pages)]
    vmem = 2 * nb * n_pages * page * (2 * KV_WIDTH + IDX_DIM) * 4 + 16 * nb * (past + LANES) * LANES * 4 + (8 << 20)
    npg = nb * n_pages
    return pl.pallas_call(
        functools.partial(_dsa_sample_body, n_pages=n_pages, page=page, tq=tq, nb=nb, topk=topk),
        grid_spec=pltpu.PrefetchScalarGridSpec(
            num_scalar_prefetch=1,
            grid=(batch // nb,),
            in_specs=in_specs,
            out_specs=pl.BlockSpec((nb * tq, ATT_WIDTH), lambda gi, pt_ref: (gi, 0)),
        ),
        out_shape=jax.ShapeDtypeStruct((batch * tq, ATT_WIDTH), F32),
        compiler_params=_cparams(("parallel",), vmem),
        name="dsa_sample",
    )(pt, q16, iq16, ikw, k, v, *([cache_kt] * npg), *([cache_vt] * npg), *([cache_ikt] * npg))


def _hgrn_chunk(xq, xf, xi, lb, state, sub):
    c = xq.shape[0]
    f = lb + (1.0 - lb) * jax.nn.sigmoid(xf)
    logf = jnp.log(f)
    kk = 1.0 - f
    qf = _silu(xq) * (HG_KDIM ** -0.5)
    ta, tb = _tri_incl(c)
    tril = jnp.where(tb <= ta, 1.0, 0.0)
    b = jnp.dot(tril, logf, precision=lax.Precision.HIGHEST, preferred_element_type=F32)

    o_inter = _dot((qf * jnp.exp(b)).astype(BF16), state.astype(BF16))
    xi16 = xi.astype(BF16)
    rsub = lax.broadcasted_iota(jnp.int32, (sub, 1), 0)
    outs = []
    for i in range(c // sub):
        r0 = i * sub
        qi = qf[r0:r0 + sub]
        bi = b[r0:r0 + sub]
        od = jnp.zeros((sub, xi.shape[1]), F32)
        for s in range(sub):
            bs = b[r0 + s:r0 + s + 1]
            e = jnp.exp(jnp.minimum(bi - bs, 0.0))
            a = jnp.sum(qi * kk[r0 + s:r0 + s + 1] * e, axis=-1, keepdims=True)
            od = od + jnp.where(rsub >= s, a, 0.0) * xi[r0 + s:r0 + s + 1]
        if i > 0:
            b0 = b[r0 - 1:r0]
            qh = (qi * jnp.exp(bi - b0)).astype(BF16)
            kh = (kk[0:r0] * jnp.exp(b0 - b[0:r0])).astype(BF16)
            od = od + _dot(_dot_nt(qh, kh).astype(BF16), xi16[0:r0])
        outs.append(od)
    o = o_inter + (jnp.concatenate(outs, axis=0) if len(outs) > 1 else outs[0])

    blast = b[c - 1:c]
    kd = (kk * jnp.exp(blast - b)).astype(BF16)
    da, db = _tri_incl(state.shape[0])
    dcol = jnp.sum(jnp.where(da == db, jnp.exp(blast), 0.0), axis=-1, keepdims=True)
    new_state = state * dcol + _dot_tn(kd, xi16)
    return o, new_state


def _hgrn_gate(o, gnorm, xg):
    return _rms(o, gnorm) * _silu(xg)


def _hgrn_prompt_body(hq_ref, hf_ref, hi_ref, hg_ref, lb_ref, gn_ref, o_ref, s_ref, st_ref, *, chunk, sub):
    gi = pl.program_id(1)

    @pl.when(gi == 0)
    def _():
        st_ref[...] = jnp.zeros_like(st_ref)

    n_chunks = hq_ref.shape[0] // chunk
    gn = gn_ref[...]

    def step(ci, carry):
        rows = pl.ds(pl.multiple_of(ci * chunk, chunk), chunk)
        for h in range(HG_HEADS):
            cols = slice(h * HG_KDIM, (h + 1) * HG_KDIM)
            o, st = _hgrn_chunk(hq_ref[rows, cols], hf_ref[rows, cols], hi_ref[rows, cols],
                                lb_ref[:, cols], st_ref[h], sub)
            st_ref[h] = st
            o_ref[rows, cols] = _hgrn_gate(o, gn, hg_ref[rows, cols])
        return carry

    lax.fori_loop(0, n_chunks, step, 0)

    @pl.when(gi == pl.num_programs(1) - 1)
    def _():
        s_ref[...] = st_ref[...]


def _hgrn_prompt(hq, hf, hi, hg, lb, gnorm, *, batch, seq, rows_per_step):
    ng = seq // rows_per_step
    chunk = min(HG_CHUNK, seq)
    sub = min(HG_SUB, chunk)

    def tok():
        return pl.BlockSpec((rows_per_step, HG_WIDTH), lambda bi, gi: (bi * ng + gi, 0))

    vmem = 2 * 5 * rows_per_step * HG_WIDTH * 4 + 3 * HG_HEADS * HG_KDIM * HG_VDIM * 4 + (8 << 20)
    return pl.pallas_call(
        functools.partial(_hgrn_prompt_body, chunk=chunk, sub=sub),
        grid=(batch, ng),
        in_specs=[tok(), tok(), tok(), tok(),
                  pl.BlockSpec((1, HG_WIDTH), lambda bi, gi: (0, 0)),
                  pl.BlockSpec((1, HG_VDIM), lambda bi, gi: (0, 0))],
        out_specs=[tok(), pl.BlockSpec((None, HG_HEADS, HG_KDIM, HG_VDIM), lambda bi, gi: (bi, 0, 0, 0))],
        out_shape=[jax.ShapeDtypeStruct((batch * seq, HG_WIDTH), F32),
                   jax.ShapeDtypeStruct((batch, HG_HEADS, HG_KDIM, HG_VDIM), F32)],
        scratch_shapes=[pltpu.VMEM((HG_HEADS, HG_KDIM, HG_VDIM), F32)],
        compiler_params=_cparams(("parallel", "arbitrary"), vmem),
        name="hgrn_prompt",
    )(hq, hf, hi, hg, lb, gnorm)


def _hgrn_sample_body(hq_ref, hf_ref, hi_ref, hg_ref, lb_ref, gn_ref, s0_ref, o_ref, s_ref, *, tq, nb):
    gn = gn_ref[...]

    def step(bi, carry):
        rows = pl.ds(pl.multiple_of(bi * tq, tq), tq)
        for h in range(HG_HEADS):
            cols = slice(h * HG_KDIM, (h + 1) * HG_KDIM)
            o, st = _hgrn_chunk(hq_ref[rows, cols], hf_ref[rows, cols], hi_ref[rows, cols],
                                lb_ref[:, cols], s0_ref[bi, h], tq)
            s_ref[bi, h] = st
            o_ref[rows, cols] = _hgrn_gate(o, gn, hg_ref[rows, cols])
        return carry

    lax.fori_loop(0, nb, step, 0)


def _hgrn_sample(hq, hf, hi, hg, lb, gnorm, state, *, layer, row0, batch, tq, nb):
    assert tq <= HG_CHUNK
    blk0 = row0 // (nb * tq)

    def tok():
        return pl.BlockSpec((nb * tq, HG_WIDTH), lambda gi: (blk0 + gi, 0))

    st_blk = (nb, HG_HEADS, HG_KDIM, HG_VDIM)
    vmem = 2 * (5 * nb * tq * HG_WIDTH * 4 + 2 * nb * HG_HEADS * HG_KDIM * HG_VDIM * 4) + (8 << 20)
    return pl.pallas_call(
        functools.partial(_hgrn_sample_body, tq=tq, nb=nb),
        grid=(batch // nb,),
        in_specs=[tok(), tok(), tok(), tok(),
                  pl.BlockSpec((1, HG_WIDTH), lambda gi: (0, 0)),
                  pl.BlockSpec((1, HG_VDIM), lambda gi: (0, 0)),
                  pl.BlockSpec((None,) + st_blk, lambda gi: (layer, gi, 0, 0, 0))],
        out_specs=[pl.BlockSpec((nb * tq, HG_WIDTH), lambda gi: (gi, 0)),
                   pl.BlockSpec(st_blk, lambda gi: (gi, 0, 0, 0))],
        out_shape=[jax.ShapeDtypeStruct((batch * tq, HG_WIDTH), F32),
                   jax.ShapeDtypeStruct((batch, HG_HEADS, HG_KDIM, HG_VDIM), F32)],
        compiler_params=_cparams(("parallel",), vmem),
        name="hgrn_sample",
    )(hq, hf, hi, hg, lb, gnorm, state)


def _mix_body(attp_ref, atts_ref, ohgp_ref, ohgs_ref, h_ref, g_ref, w_ref, o_ref, *, n_prompt_tiles):
    is_prompt = pl.program_id(0) < n_prompt_tiles
    att = jnp.where(is_prompt, attp_ref[...], atts_ref[...]).astype(BF16)
    ohg = jnp.where(is_prompt, ohgp_ref[...], ohgs_ref[...]).astype(BF16)
    mix = _dot(att, w_ref[0:ATT_WIDTH, :]) + _dot(ohg, w_ref[ATT_WIDTH:ATT_WIDTH + HG_WIDTH, :])
    o_ref[...] = h_ref[...] + _rms(mix, g_ref[...])


def _mix(att_p, att_s, ohg_p, ohg_s, h, g, w_out, *, tm):
    n, d = h.shape
    ntp = att_p.shape[0] // tm
    nts = att_s.shape[0] // tm
    assert att_p.shape[0] % tm == 0 and att_s.shape[0] % tm == 0 and ntp + nts == n // tm
    p_map = lambda i: (jnp.minimum(i, ntp - 1), 0)
    s_map = lambda i: (jnp.maximum(i - ntp, 0), 0)
    vmem = 2 * (4 * tm * 512 * 4 + 2 * tm * d * 4 + (ATT_WIDTH + HG_WIDTH) * d * 2) + 3 * tm * d * 4 + (4 << 20)
    return pl.pallas_call(
        functools.partial(_mix_body, n_prompt_tiles=ntp),
        grid=(n // tm,),
        in_specs=[pl.BlockSpec((tm, ATT_WIDTH), p_map),
                  pl.BlockSpec((tm, ATT_WIDTH), s_map),
                  pl.BlockSpec((tm, HG_WIDTH), p_map),
                  pl.BlockSpec((tm, HG_WIDTH), s_map),
                  pl.BlockSpec((tm, d), lambda i: (i, 0)),
                  pl.BlockSpec((1, d), lambda i: (0, 0)),
                  pl.BlockSpec((ATT_WIDTH + HG_WIDTH, d), lambda i: (0, 0))],
        out_specs=pl.BlockSpec((tm, d), lambda i: (i, 0)),
        out_shape=jax.ShapeDtypeStruct((n, d), F32),
        compiler_params=_cparams(("parallel",), vmem),
        name="mix",
    )(att_p, att_s, ohg_p, ohg_s, h, g, w_out)


def _ple_body(h_ref, p_ref, gpre_ref, gpost_ref, wgate_ref, wproj_ref, o_ref):
    h = h_ref[...]
    gate = jax.nn.sigmoid(_dot(_rms(h, gpre_ref[...]).astype(BF16), wgate_ref[...]))
    pe = _dot(p_ref[...].astype(BF16), wproj_ref[...])
    o_ref[...] = h + _rms(pe * gate, gpost_ref[...])


def _ple(h, p, g_pre, g_post, w_gate, w_proj, *, tm):
    n, d = h.shape
    pd = p.shape[1]
    vmem = 2 * (2 * tm * d * 4 + tm * pd * 4 + d * d * 2 + pd * d * 2) + 3 * tm * d * 4 + (4 << 20)
    return pl.pallas_call(
        _ple_body,
        grid=(n // tm,),
        in_specs=[pl.BlockSpec((tm, d), lambda i: (i, 0)),
                  pl.BlockSpec((tm, pd), lambda i: (i, 0)),
                  pl.BlockSpec((1, d), lambda i: (0, 0)),
                  pl.BlockSpec((1, d), lambda i: (0, 0)),
                  pl.BlockSpec((d, d), lambda i: (0, 0)),
                  pl.BlockSpec((pd, d), lambda i: (0, 0))],
        out_specs=pl.BlockSpec((tm, d), lambda i: (i, 0)),
        out_shape=jax.ShapeDtypeStruct((n, d), F32),
        compiler_params=_cparams(("parallel",), vmem),
        name="ple",
    )(h, p, g_pre, g_post, w_gate, w_proj)


def _rot_cols(w, heads):
    d = w.shape[0]
    w4 = w.reshape(d, heads, 2, HEAD_DIM // 2)
    return jnp.concatenate([-w4[:, :, 1:2], w4[:, :, 0:1]], axis=2).reshape(d, heads * HEAD_DIM)


def _pack_w_in(w):
    d = w.shape[0]
    widths = (ATT_WIDTH, KV_WIDTH, KV_WIDTH, IDX_HEADS * IDX_DIM, IDX_DIM, IDX_HEADS,
              HG_HEADS * HG_KDIM, HG_HEADS * HG_KDIM, HG_WIDTH, HG_WIDTH)
    parts = []
    acc = 0
    for wd in widths:
        parts.append(w[:, acc:acc + wd])
        acc += wd
    q, k, v, iq, ik, iw, hq, hf, hi, hg = parts
    z = lambda n: jnp.zeros((d, n), w.dtype)
    ikw = jnp.concatenate([ik, iw, z(LANES - IDX_DIM - IDX_HEADS)], axis=1)
    ikr = jnp.concatenate([_rot_cols(ik, 1), z(LANES - IDX_DIM)], axis=1)
    cat = jnp.concatenate([q, k, iq, ikw, v, hq, hf, hi, hg,
                           _rot_cols(q, N_HEADS), _rot_cols(k, KV_HEADS), _rot_cols(iq, IDX_HEADS), ikr], axis=1)
    assert cat.shape[1] == _C_END
    w_t = jnp.concatenate([k, _rot_cols(k, KV_HEADS), v, ik, _rot_cols(ik, 1)], axis=1).T
    assert w_t.shape[0] == _R_END
    return cat.astype(BF16), w_t.astype(BF16)


def _rope_tables(seq, past_len, tq, tm):
    half = HEAD_DIM // 2
    inv_freq = ROPE_THETA ** (-jnp.arange(half, dtype=F32) / half)
    pos_p = jnp.arange(seq, dtype=jnp.int32)
    pos_s = past_len + (jnp.arange(tm, dtype=jnp.int32) % tq)
    pos = jnp.concatenate([pos_p, pos_s]).astype(F32)
    ang = pos[:, None] * inv_freq[None, :]
    reps = LANES // half
    cos = jnp.tile(jnp.cos(ang), (1, reps)).reshape(seq // tm + 1, tm, LANES)
    sin = jnp.tile(jnp.sin(ang), (1, reps)).reshape(seq // tm + 1, tm, LANES)
    nt = seq // tm
    cos_t = jnp.swapaxes(cos[:nt], 1, 2)
    sin_t = jnp.swapaxes(sin[:nt], 1, 2)
    return cos, sin, cos_t, sin_t


def _lower_bounds(logits):
    sm = jax.nn.softmax(logits.astype(F32), axis=0)
    return jnp.cumsum(sm, axis=0) - sm[0:1]


def _feature_major(cache, width):
    depth, pool, page = cache.shape[:3]
    c = cache.reshape(depth, pool, page, width)
    return jnp.swapaxes(c, 2, 3)


TM_DENSE = 1024
TM_PROJ = 512
TF_FFN = 256
Q_BLOCK = 128
N_KEY_LEN = 4
DSA_SEQS = 4
HG_ROWS = 512
HG_SEQS = 8


def kernel(x_prompt, x_sample, p_prompt, p_sample, cache_k, cache_v, cache_idx_k, state_hgrn, page_table,
           n_f1_pre, n_f1_post, w_f1_gate, w_f1_up, w_f1_down,
           n_mix_pre, n_mix_post, w_in, hg_lb_logits, hg_norm, w_out,
           n_f2_pre, n_f2_post, w_f2_gate, w_f2_up, w_f2_down,
           n_ple_pre, n_ple_post, w_ple_proj, w_ple_gate):
    bp, tp, d = x_prompt.shape
    bs, ts, _ = x_sample.shape
    depth = w_in.shape[0]
    n_p, n_s = bp * tp, bs * ts
    page = cache_k.shape[2]
    past_len = page_table.shape[1] * page

    h = jnp.concatenate([x_prompt.reshape(n_p, d), x_sample.reshape(n_s, d)], axis=0)
    ckt = _feature_major(cache_k, KV_WIDTH)
    cvt = _feature_major(cache_v, KV_WIDTH)
    cikt = _feature_major(cache_idx_k, IDX_DIM)
    cos_tab, sin_tab, cos_t, sin_t = _rope_tables(tp, past_len, ts, TM_PROJ)
    lbs = _lower_bounds(hg_lb_logits)
    row = lambda a: a.reshape(1, -1)
    bf = lambda a: a.astype(BF16)

    outs = {name: [] for name in ("kp", "vp", "ikp", "sp", "ks", "vs", "iks", "ss")}
    for i in range(depth):
        h1 = _ffn(h, row(n_f1_pre[i]), row(n_f1_post[i]), bf(w_f1_gate[i]), bf(w_f1_up[i]), bf(w_f1_down[i]),
                  tm=TM_DENSE, tf=TF_FFN)
        w_cat, w_t = _pack_w_in(w_in[i])
        g_mix = row(n_mix_pre[i])
        q16, k, v, iq16, ikw, hq, hf, hi, hg = _proj(
            h1, g_mix, w_cat, cos_tab, sin_tab,
            tm=TM_PROJ, tiles_per_seq=tp // TM_PROJ, n_prompt_tiles=n_p // TM_PROJ)
        kt, vt, ikt, kt16, vt16, ikt16 = _proj_t(h1, g_mix, w_t, cos_t, sin_t, batch=bp, seq=tp, tm=TM_PROJ)

        att_p = _dsa_prompt(q16, iq16, ikw, kt16, vt16, ikt16, batch=bp, seq=tp, qb=min(Q_BLOCK, tp),
                            n_len=N_KEY_LEN)
        att_s = _dsa_sample(page_table, q16, iq16, ikw, k, v, ckt, cvt, cikt,
                            layer=i, row0=n_p, batch=bs, tq=ts, nb=DSA_SEQS)
        lb = row(lbs[i])
        gn = row(hg_norm[i])
        ohg_p, s_p = _hgrn_prompt(hq, hf, hi, hg, lb, gn, batch=bp, seq=tp, rows_per_step=HG_ROWS)
        ohg_s, s_s = _hgrn_sample(hq, hf, hi, hg, lb, gn, state_hgrn,
                                  layer=i, row0=n_p, batch=bs, tq=ts, nb=HG_SEQS)

        h2 = _mix(att_p, att_s, ohg_p, ohg_s, h1, row(n_mix_post[i]), bf(w_out[i]), tm=TM_DENSE)
        h3 = _ffn(h2, row(n_f2_pre[i]), row(n_f2_post[i]), bf(w_f2_gate[i]), bf(w_f2_up[i]), bf(w_f2_down[i]),
                  tm=TM_DENSE, tf=TF_FFN)
        p_emb = jnp.concatenate([p_prompt[i].reshape(n_p, -1), p_sample[i].reshape(n_s, -1)], axis=0)
        h = _ple(h3, p_emb, row(n_ple_pre[i]), row(n_ple_post[i]), bf(w_ple_gate[i]), bf(w_ple_proj[i]),
                 tm=TM_DENSE)

        outs["kp"].append(jnp.transpose(kt.reshape(bp, KV_HEADS, HEAD_DIM, tp), (0, 3, 1, 2)))
        outs["vp"].append(jnp.transpose(vt.reshape(bp, KV_HEADS, HEAD_DIM, tp), (0, 3, 1, 2)))
        outs["ikp"].append(jnp.swapaxes(ikt, 1, 2))
        outs["sp"].append(s_p)
        outs["ks"].append(k[n_p:].reshape(bs, ts, KV_HEADS, HEAD_DIM))
        outs["vs"].append(v[n_p:].reshape(bs, ts, KV_HEADS, HEAD_DIM))
        outs["iks"].append(ikw[n_p:, :IDX_DIM].reshape(bs, ts, IDX_DIM))
        outs["ss"].append(s_s.astype(state_hgrn.dtype))

    st = lambda name: jnp.stack(outs[name])
    return (h[:n_p].reshape(bp, tp, d), h[n_p:].reshape(bs, ts, d),
            st("kp"), st("vp"), st("ikp"), st("sp"), st("ks"), st("vs"), st("iks"), st("ss"))
```

```python
import functools

import jax
import jax.numpy as jnp
from jax import lax
from jax.experimental import pallas as pl
from jax.experimental.pallas import tpu as pltpu

HEAD_DIM = 64
N_HEADS = 8
KV_HEADS = 2
IDX_HEADS = 8
IDX_DIM = 64
TOPK_MAX = 256
HG_KDIM = 128
HG_VDIM = 128
HG_HEADS = 4
HG_CHUNK = 64
HG_SUB = 8
ROPE_THETA = 10000.0
EPS = 1e-6
ATT_WIDTH = N_HEADS * HEAD_DIM
KV_WIDTH = KV_HEADS * HEAD_DIM
HG_WIDTH = HG_HEADS * HG_VDIM
QK_SCALE = HEAD_DIM ** -0.5
IDX_SCALE = IDX_DIM ** -0.5

LANES = 128
SUBLANES = 8
VMEM_BUDGET_BYTES = 56 * 1024 * 1024

NEG_BIG = -1e30
INT32_MIN = -2147483648

F32 = jnp.float32
BF16 = jnp.bfloat16


def _cparams(sem, vmem_bytes):
    return pltpu.CompilerParams(dimension_semantics=sem,
                                vmem_limit_bytes=int(min(max(vmem_bytes, 16 << 20), VMEM_BUDGET_BYTES)))


def _rms(x, g):
    return x * lax.rsqrt(jnp.mean(x * x, axis=-1, keepdims=True) + EPS) * g


def _silu(x):
    return x * jax.nn.sigmoid(x)


def _dot(a, b):
    return jnp.dot(a, b, preferred_element_type=F32)


def _dot_nt(a, b):
    return lax.dot_general(a, b, (((1,), (1,)), ((), ())), preferred_element_type=F32)


def _dot_tn(a, b):
    return lax.dot_general(a, b, (((0,), (0,)), ((), ())), preferred_element_type=F32)


def _ffn_body(x_ref, gpre_ref, gpost_ref, wg_ref, wu_ref, wd_ref, o_ref, xn_ref, acc_ref):
    j = pl.program_id(1)

    @pl.when(j == 0)
    def _():
        xn_ref[...] = _rms(x_ref[...], gpre_ref[...]).astype(BF16)
        acc_ref[...] = jnp.zeros_like(acc_ref)

    xn = xn_ref[...]
    a = _dot(xn, wg_ref[...])
    b = _dot(xn, wu_ref[...])
    acc_ref[...] += _dot((_silu(a) * b).astype(BF16), wd_ref[...])

    @pl.when(j == pl.num_programs(1) - 1)
    def _():
        o_ref[...] = x_ref[...] + 0.5 * _rms(acc_ref[...], gpost_ref[...])


def _ffn(x, g_pre, g_post, wg, wu, wd, *, tm, tf):
    n, d = x.shape
    dff = wg.shape[1]
    vmem = 2 * (2 * tm * d * 4 + 3 * d * tf * 2) + tm * d * 6 + 3 * tm * tf * 4 + (4 << 20)
    return pl.pallas_call(
        _ffn_body,
        grid=(n // tm, dff // tf),
        in_specs=[
            pl.BlockSpec((tm, d), lambda i, j: (i, 0)),
            pl.BlockSpec((1, d), lambda i, j: (0, 0)),
            pl.BlockSpec((1, d), lambda i, j: (0, 0)),
            pl.BlockSpec((d, tf), lambda i, j: (0, j)),
            pl.BlockSpec((d, tf), lambda i, j: (0, j)),
            pl.BlockSpec((tf, d), lambda i, j: (j, 0)),
        ],
        out_specs=pl.BlockSpec((tm, d), lambda i, j: (i, 0)),
        out_shape=jax.ShapeDtypeStruct((n, d), F32),
        scratch_shapes=[pltpu.VMEM((tm, d), BF16), pltpu.VMEM((tm, d), F32)],
        compiler_params=_cparams(("parallel", "arbitrary"), vmem),
        name="ffn",
    )(x, g_pre, g_post, wg, wu, wd)


_C_Q, _C_K, _C_IQ, _C_IKW, _C_V, _C_HQ, _C_HF, _C_HI, _C_HG = 0, 512, 640, 1152, 1280, 1408, 1920, 2432, 2944
_C_QR, _C_KR, _C_IQR, _C_IKR, _C_END = 3456, 3968, 4096, 4608, 4736


def _proj_body(h_ref, g_ref, w_ref, cos_ref, sin_ref,
               q_ref, k_ref, v_ref, iq_ref, ikw_ref, hq_ref, hf_ref, hi_ref, hg_ref, ikw16_ref):
    u = _rms(h_ref[...], g_ref[...]).astype(BF16)

    def mm(lo, hi):
        return _dot(u, w_ref[:, lo:hi])

    cos = cos_ref[...]
    sin = sin_ref[...]
    cos4 = jnp.concatenate([cos] * 4, axis=1)
    sin4 = jnp.concatenate([sin] * 4, axis=1)
    q_ref[...] = ((mm(_C_Q, _C_K) * cos4 + mm(_C_QR, _C_KR) * sin4) * QK_SCALE).astype(BF16)
    k_ref[...] = mm(_C_K, _C_IQ) * cos + mm(_C_KR, _C_IQR) * sin
    iq_ref[...] = ((mm(_C_IQ, _C_IKW) * cos4 + mm(_C_IQR, _C_IKR) * sin4) * IDX_SCALE).astype(BF16)
    lane = lax.broadcasted_iota(jnp.int32, cos.shape, 1)
    cos_ikw = jnp.where(lane < IDX_DIM, cos, IDX_HEADS ** -0.5)
    ikw = mm(_C_IKW, _C_V) * cos_ikw + mm(_C_IKR, _C_END) * sin
    ikw_ref[...] = ikw
    ikw16_ref[...] = ikw.astype(BF16)
    v_ref[...] = mm(_C_V, _C_HQ)
    hq_ref[...] = mm(_C_HQ, _C_HF)
    hf_ref[...] = mm(_C_HF, _C_HI)
    hi_ref[...] = mm(_C_HI, _C_HG)
    hg_ref[...] = mm(_C_HG, _C_QR)


def _proj(h, g, w_cat, cos_tab, sin_tab, *, tm, tiles_per_seq, n_prompt_tiles):
    n, d = h.shape
    outs = ((512, BF16), (128, F32), (128, F32), (512, BF16), (128, F32),
            (512, F32), (512, F32), (512, F32), (512, F32), (128, BF16))

    def tab_map(i):
        return (jnp.where(i < n_prompt_tiles, i % tiles_per_seq, tiles_per_seq), 0, 0)

    out_bytes = sum(w * jnp.dtype(t).itemsize for w, t in outs)
    vmem = 2 * (tm * d * 4 + d * _C_END * 2 + tm * out_bytes + 2 * tm * LANES * 4) + 6 * tm * 512 * 4 + (4 << 20)
    return pl.pallas_call(
        _proj_body,
        grid=(n // tm,),
        in_specs=[
            pl.BlockSpec((tm, d), lambda i: (i, 0)),
            pl.BlockSpec((1, d), lambda i: (0, 0)),
            pl.BlockSpec((d, _C_END), lambda i: (0, 0)),
            pl.BlockSpec((None, tm, LANES), tab_map),
            pl.BlockSpec((None, tm, LANES), tab_map),
        ],
        out_specs=[pl.BlockSpec((tm, w), lambda i: (i, 0)) for w, _ in outs],
        out_shape=[jax.ShapeDtypeStruct((n, w), t) for w, t in outs],
        compiler_params=_cparams(("parallel",), vmem),
        name="proj",
    )(h, g, w_cat, cos_tab, sin_tab)


_R_K, _R_KR, _R_V, _R_IK, _R_IKR, _R_END = 0, 128, 256, 384, 448, 512


def _proj_t_body(h_ref, g_ref, w_ref, cos_ref, sin_ref, k_ref, v_ref, ik_ref, k16_ref, v16_ref):
    u = _rms(h_ref[...], g_ref[...]).astype(BF16)
    r = _dot_nt(w_ref[...], u)
    cos = cos_ref[...]
    sin = sin_ref[...]
    k = r[_R_K:_R_KR] * cos + r[_R_KR:_R_V] * sin
    v = r[_R_V:_R_IK]
    ik = r[_R_IK:_R_IKR] * cos[0:IDX_DIM] + r[_R_IKR:_R_END] * sin[0:IDX_DIM]
    k_ref[...] = k
    v_ref[...] = v
    ik_ref[...] = ik
    k16_ref[...] = k.astype(BF16)
    v16_ref[...] = v.astype(BF16)


def _proj_t(h, g, w_t, cos_t, sin_t, *, batch, seq, tm):
    d = h.shape[1]
    nt = seq // tm
    outs = ((KV_WIDTH, F32), (KV_WIDTH, F32), (IDX_DIM, F32), (KV_WIDTH, BF16), (KV_WIDTH, BF16))
    vmem = 2 * (tm * d * 4 + _R_END * d * 2 + 2 * LANES * tm * 4 + 3 * LANES * tm * 6) + 4 * _R_END * tm * 4 + (4 << 20)
    return pl.pallas_call(
        _proj_t_body,
        grid=(batch, nt),
        in_specs=[
            pl.BlockSpec((tm, d), lambda b, t: (b * nt + t, 0)),
            pl.BlockSpec((1, d), lambda b, t: (0, 0)),
            pl.BlockSpec((_R_END, d), lambda b, t: (0, 0)),
            pl.BlockSpec((None, LANES, tm), lambda b, t: (t, 0, 0)),
            pl.BlockSpec((None, LANES, tm), lambda b, t: (t, 0, 0)),
        ],
        out_specs=[pl.BlockSpec((None, w, tm), lambda b, t: (b, 0, t)) for w, _ in outs],
        out_shape=[jax.ShapeDtypeStruct((batch, w, seq), t) for w, t in outs],
        compiler_params=_cparams(("parallel", "parallel"), vmem),
        name="proj_t",
    )(h, g, w_t, cos_t, sin_t)


def _prefix_count(eq, tri):
    r, l = eq.shape
    carry = jnp.zeros((r, 1), F32)
    outs = []
    for c in range(l // LANES):
        e = jnp.where(eq[:, c * LANES:(c + 1) * LANES], 1.0, 0.0).astype(BF16)
        w = _dot(e, tri)
        outs.append(w + carry)
        carry = carry + w[:, LANES - 1:LANES]
    return jnp.concatenate(outs, axis=1)


def _topk_mask(score, adm, k, tri):
    score = jnp.where(score == 0.0, 0.0, score)
    score = jnp.where(adm, score, -jnp.inf)
    bits = lax.bitcast_convert_type(score, jnp.int32)
    key = bits ^ ((bits >> 31) & 0x7FFFFFFF)
    rows = score.shape[0]

    def body(i, t):
        cand = t + lax.shift_left(jnp.int32(1), 31 - i)
        cnt = jnp.sum(jnp.where(key >= cand, 1.0, 0.0), axis=-1, keepdims=True)
        return jnp.where(cnt >= k, cand, t)

    t = lax.fori_loop(0, 32, body, jnp.full((rows, 1), INT32_MIN, jnp.int32))
    gt = key > t
    eq = key == t
    need = k - jnp.sum(jnp.where(gt, 1.0, 0.0), axis=-1, keepdims=True)
    pref = _prefix_count(eq, tri)
    return jnp.logical_and(jnp.logical_or(gt, jnp.logical_and(eq, pref <= need)), adm)


def _softmax_rows(s, sel):
    s = jnp.where(sel, s, NEG_BIG)
    p = jnp.exp(s - jnp.max(s, axis=-1, keepdims=True))
    return p, jnp.sum(p, axis=-1, keepdims=True)


def _tri_incl(n):
    a = lax.broadcasted_iota(jnp.int32, (n, n), 0)
    b = lax.broadcasted_iota(jnp.int32, (n, n), 1)
    return a, b


def _count_tri():
    a, b = _tri_incl(LANES)
    return jnp.where(a <= b, 1.0, 0.0).astype(BF16)


I16_MIN = -32768
I16_ROWS = 16


def _tree_sum(xs):
    xs = list(xs)
    while len(xs) > 1:
        nxt = [xs[i] + xs[i + 1] for i in range(0, len(xs) - 1, 2)]
        if len(xs) % 2:
            nxt.append(xs[-1])
        xs = nxt
    return xs[0]


def _count_cols(mask, dtype):
    rows = I16_ROWS if dtype == jnp.int16 else SUBLANES
    one = jnp.where(mask, jnp.ones((), dtype), jnp.zeros((), dtype))
    part = _tree_sum([one[c * rows:(c + 1) * rows] for c in range(mask.shape[0] // rows)])
    return jnp.sum(part.astype(F32), axis=0, keepdims=True)


def _search16(x16, k):
    def body(i, t):
        cand = t + lax.shift_left(jnp.int32(1), 15 - i)
        cnt = _count_cols(x16 >= cand.astype(jnp.int16), jnp.int16)
        return jnp.where(cnt >= k, cand, t)

    return lax.fori_loop(0, 16, body, jnp.full(k.shape, I16_MIN, jnp.int32))


def _topk_mask_t(score, adm, k):
    l, n = score.shape
    score = jnp.where(score == 0.0, 0.0, score)
    score = jnp.where(adm, score, -jnp.inf)
    bits = lax.bitcast_convert_type(score, jnp.int32)
    key = bits ^ ((bits >> 31) & 0x7FFFFFFF)
    kf = jnp.full((1, n), float(k), F32)
    hi = (key >> 16).astype(jnp.int16)
    t_hi = _search16(hi, kf)
    t_hi16 = t_hi.astype(jnp.int16)
    above = _count_cols(hi > t_hi16, jnp.int16)
    lo = ((key & 0xFFFF) + I16_MIN).astype(jnp.int16)
    lo = jnp.where(hi == t_hi16, lo, jnp.int16(I16_MIN))
    t_lo = _search16(lo, kf - above)
    t = lax.shift_left(t_hi, 16) | (t_lo - I16_MIN)
    gt = key > t
    eq = key == t
    need = kf - _count_cols(gt, jnp.float32)
    a, b = _tri_incl(LANES)
    tril = jnp.where(b <= a, 1.0, 0.0).astype(BF16)
    carry = jnp.zeros((1, n), F32)
    sel = []
    for c in range(l // LANES):
        rows = slice(c * LANES, (c + 1) * LANES)
        pref = _dot(tril, jnp.where(eq[rows], 1.0, 0.0).astype(BF16)) + carry
        carry = pref[LANES - 1:LANES]
        take = jnp.logical_or(gt[rows], jnp.logical_and(eq[rows], pref <= need))
        sel.append(jnp.where(jnp.logical_and(take, adm[rows]), 1.0, 0.0))
    return jnp.concatenate(sel, axis=0) if len(sel) > 1 else sel[0]


def _dsa_prompt_tile(q_ref, iq_ref, iw_ref, kt_ref, vt_ref, ik_ref, o_ref, *, q0, l, topk):
    qb = q_ref.shape[0]
    iq_t = iq_ref[...].astype(F32).T.astype(BF16)
    w_t = iw_ref[...].T[IDX_DIM:IDX_DIM + IDX_HEADS]
    ik = ik_ref[0:l, 0:IDX_DIM]
    score = jnp.zeros((l, qb), F32)
    for h in range(IDX_HEADS):
        lg = _dot(ik, iq_t[h * IDX_DIM:(h + 1) * IDX_DIM])
        score = score + w_t[h:h + 1] * jnp.maximum(lg, 0.0)

    kpos = lax.broadcasted_iota(jnp.int32, (l, qb), 0)
    qpos = q0 + lax.broadcasted_iota(jnp.int32, (l, qb), 1)
    sel_t = _topk_mask_t(score, kpos <= qpos, topk)
    bias = ((sel_t - 1.0) * (-NEG_BIG)).T

    q = q_ref[...]
    group = N_HEADS // KV_HEADS
    for h in range(N_HEADS):
        g = h // group
        s = _dot(q[:, h * HEAD_DIM:(h + 1) * HEAD_DIM], kt_ref[g * HEAD_DIM:(g + 1) * HEAD_DIM, 0:l]) + bias
        p = jnp.exp(s - jnp.max(s, axis=-1, keepdims=True))
        den = jnp.sum(p, axis=-1, keepdims=True)
        o = _dot_nt(p.astype(BF16), vt_ref[g * HEAD_DIM:(g + 1) * HEAD_DIM, 0:l])
        o_ref[:, h * HEAD_DIM:(h + 1) * HEAD_DIM] = o / den


def _dsa_prompt_body(q_ref, iq_ref, iw_ref, kt_ref, vt_ref, ik_ref, o_ref, *, topk, n_len):
    qi = pl.program_id(1)
    nq = pl.num_programs(1)
    qb = q_ref.shape[0]
    seq = kt_ref.shape[1]
    per = nq // n_len
    for v in range(n_len):
        l = (v + 1) * per * qb

        @pl.when(qi // per == v)
        def _(l=l):
            _dsa_prompt_tile(q_ref, iq_ref, iw_ref, kt_ref, vt_ref, ik_ref, o_ref,
                             q0=qi * qb, l=min(l, seq), topk=topk)


def _dsa_prompt(q16, iq16, ikw, ikw16, kt16, vt16, *, batch, seq, qb, n_len):
    topk = min(TOPK_MAX, seq // 4)
    nq = seq // qb
    n_len = n_len if nq % n_len == 0 else 1
    vmem = 2 * (2 * qb * 512 * 2 + qb * LANES * 4 + qb * 512 * 4 + seq * (2 * KV_WIDTH + LANES) * 2) \
        + 16 * qb * seq * 4 + (4 << 20)
    return pl.pallas_call(
        functools.partial(_dsa_prompt_body, topk=topk, n_len=n_len),
        grid=(batch, nq),
        in_specs=[
            pl.BlockSpec((qb, ATT_WIDTH), lambda bi, qi: (bi * nq + qi, 0)),
            pl.BlockSpec((qb, IDX_HEADS * IDX_DIM), lambda bi, qi: (bi * nq + qi, 0)),
            pl.BlockSpec((qb, LANES), lambda bi, qi: (bi * nq + qi, 0)),
            pl.BlockSpec((None, KV_WIDTH, seq), lambda bi, qi: (bi, 0, 0)),
            pl.BlockSpec((None, KV_WIDTH, seq), lambda bi, qi: (bi, 0, 0)),
            pl.BlockSpec((seq, LANES), lambda bi, qi: (bi, 0)),
        ],
        out_specs=pl.BlockSpec((qb, ATT_WIDTH), lambda bi, qi: (bi * nq + qi, 0)),
        out_shape=jax.ShapeDtypeStruct((batch * seq, ATT_WIDTH), F32),
        compiler_params=_cparams(("parallel", "parallel"), vmem),
        name="dsa_prompt",
    )(q16, iq16, ikw, kt16, vt16, ikw16)


def _dsa_sample_body(pt_ref, q_ref, iq_ref, ikw_ref, kn_ref, vn_ref, *rest, n_pages, page, tq, nb, topk):
    del pt_ref
    npg = nb * n_pages
    kp_refs, vp_refs, ip_refs = rest[0:npg], rest[npg:2 * npg], rest[2 * npg:3 * npg]
    o_ref = rest[3 * npg]
    past = n_pages * page
    l = past + LANES
    tri = _count_tri()
    group = N_HEADS // KV_HEADS

    def pad_rows(x):
        return jnp.concatenate([x, jnp.zeros((LANES - tq, x.shape[1]), x.dtype)], axis=0)

    def pages(refs, bj):
        return jnp.concatenate([refs[bj * n_pages + p][...] for p in range(n_pages)], axis=1).astype(BF16)

    iq_all = iq_ref[...].astype(F32)
    q_all = q_ref[...].astype(F32)
    ikw_all = ikw_ref[...]

    scores = []
    for bj in range(nb):
        rows = slice(bj * tq, (bj + 1) * tq)
        iq = iq_all[rows]
        iq_hm = jnp.concatenate([iq[:, h * IDX_DIM:(h + 1) * IDX_DIM] for h in range(IDX_HEADS)],
                                axis=0).astype(BF16)
        ik_new = pad_rows(ikw_all[rows, 0:IDX_DIM]).astype(BF16)
        lg = jnp.concatenate([_dot(iq_hm, pages(ip_refs, bj)), _dot_nt(iq_hm, ik_new)], axis=1)
        lg = jnp.maximum(lg, 0.0)
        w = ikw_all[rows, IDX_DIM:IDX_DIM + IDX_HEADS]
        sc = jnp.zeros((tq, l), F32)
        for h in range(IDX_HEADS):
            sc = sc + w[:, h:h + 1] * lg[h * tq:(h + 1) * tq, :]
        scores.append(sc)
    score = jnp.concatenate(scores, axis=0) if nb > 1 else scores[0]

    row = lax.broadcasted_iota(jnp.int32, (nb * tq, l), 0) % tq
    col = lax.broadcasted_iota(jnp.int32, (nb * tq, l), 1)
    adm = col <= past + row
    adm = jnp.logical_and(adm, col < past + tq)
    sel = _topk_mask(score, adm, topk, tri)

    for bj in range(nb):
        rows = slice(bj * tq, (bj + 1) * tq)
        q = q_all[rows]
        sel_g = jnp.concatenate([sel[rows]] * group, axis=0)
        kt = pages(kp_refs, bj)
        vt = pages(vp_refs, bj)
        k_new = pad_rows(kn_ref[rows, :]).astype(BF16)
        v_new = pad_rows(vn_ref[rows, :]).astype(BF16)
        for g in range(KV_HEADS):
            fs = slice(g * HEAD_DIM, (g + 1) * HEAD_DIM)
            qg = jnp.concatenate([q[:, h * HEAD_DIM:(h + 1) * HEAD_DIM]
                                  for h in range(g * group, (g + 1) * group)], axis=0).astype(BF16)
            s = jnp.concatenate([_dot(qg, kt[fs]), _dot_nt(qg, k_new[:, fs])], axis=1)
            p, den = _softmax_rows(s, sel_g)
            p = p.astype(BF16)
            o = (_dot_nt(p[:, 0:past], vt[fs]) + _dot(p[:, past:l], v_new[:, fs])) / den
            for j in range(group):
                h = g * group + j
                o_ref[rows, h * HEAD_DIM:(h + 1) * HEAD_DIM] = o[j * tq:(j + 1) * tq, :]


def _dsa_sample(page_table, q16, iq16, ikw, k, v, cache_kt, cache_vt, cache_ikt, *, layer, row0, batch, tq, nb):
    n_pages = page_table.shape[1]
    page = cache_kt.shape[3]
    past = n_pages * page
    topk = min(TOPK_MAX, (past + tq) // 4)
    blk0 = row0 // (nb * tq)
    pt = page_table.reshape(-1)

    def tok(w):
        return pl.BlockSpec((nb * tq, w), lambda gi, pt_ref: (blk0 + gi, 0))

    def page_spec(w, bj, p):
        return pl.BlockSpec((None, None, w, page),
                            lambda gi, pt_ref: (layer, pt_ref[(gi * nb + bj) * n_pages + p], 0, 0))

    in_specs = [tok(ATT_WIDTH), tok(IDX_HEADS * IDX_DIM), tok(LANES), tok(KV_WIDTH), tok(KV_WIDTH)]
    for w in (KV_WIDTH, KV_WIDTH, IDX_DIM):
        in_specs += [page_spec(w, bj, p) for bj in range(nb) for p in range(n_pages)]
    vmem = 2 * nb * n_pages * page * (2 * KV_WIDTH + IDX_DIM) * 4 + 16 * nb * (past + LANES) * LANES * 4 + (8 << 20)
    npg = nb * n_pages
    return pl.pallas_call(
        functools.partial(_dsa_sample_body, n_pages=n_pages, page=page, tq=tq, nb=nb, topk=topk),
        grid_spec=pltpu.PrefetchScalarGridSpec(
            num_scalar_prefetch=1,
            grid=(batch // nb,),
            in_specs=in_specs,
            out_specs=pl.BlockSpec((nb * tq, ATT_WIDTH), lambda gi, pt_ref: (gi, 0)),
        ),
        out_shape=jax.ShapeDtypeStruct((batch * tq, ATT_WIDTH), F32),
        compiler_params=_cparams(("parallel",), vmem),
        name="dsa_sample",
    )(pt, q16, iq16, ikw, k, v, *([cache_kt] * npg), *([cache_vt] * npg), *([cache_ikt] * npg))


def _group_row(x, j, g):
    c, n = x.shape
    x3 = x.reshape(c // g, g, n)
    return jnp.broadcast_to(x3[:, j:j + 1, :], (c // g, g, n)).reshape(c, n)


def _hgrn_chunk(xq, xf, xi, lb, state_t, sub):
    c, dk = xq.shape
    f = lb + (1.0 - lb) * jax.nn.sigmoid(xf)
    logf = jnp.log(f)
    kk = 1.0 - f
    qf = _silu(xq) * (HG_KDIM ** -0.5)
    ta, tb = _tri_incl(c)
    tril = jnp.where(tb <= ta, 1.0, 0.0)
    b = jnp.dot(tril, logf, precision=lax.Precision.HIGHEST, preferred_element_type=F32)

    o = _dot_nt((qf * jnp.exp(b)).astype(BF16), state_t.astype(BF16))
    xi16 = xi.astype(BF16)

    att = None
    row = lax.broadcasted_iota(jnp.int32, (c, 1), 0)
    n = c // 2
    while n >= sub:
        upper = (row % (2 * n)) >= n
        bref = _group_row(b, n - 1, 2 * n)
        dq = b - bref
        qh = jnp.where(upper, qf * jnp.exp(jnp.minimum(dq, 0.0)), 0.0).astype(BF16)
        kh = jnp.where(upper, 0.0, kk * jnp.exp(jnp.minimum(-dq, 0.0))).astype(BF16)
        lev = _dot_nt(qh, kh)
        if 2 * n < c:
            lev = jnp.where(ta // (2 * n) == tb // (2 * n), lev, 0.0)
        att = lev if att is None else att + lev
        n //= 2
    if att is not None:
        o = o + _dot(att.astype(BF16), xi16)

    rsub = row % sub
    for j in range(sub):
        e = jnp.exp(jnp.minimum(b - _group_row(b, j, sub), 0.0))
        a = jnp.sum(qf * _group_row(kk, j, sub) * e, axis=-1, keepdims=True)
        o = o + jnp.where(rsub >= j, a, 0.0) * _group_row(xi, j, sub)

    blast = b[c - 1:c]
    kd = (kk * jnp.exp(blast - b)).astype(BF16)
    new_state_t = state_t * jnp.exp(blast) + _dot_tn(xi16, kd)
    return o, new_state_t


def _hgrn_gate(o, gnorm, xg):
    return _rms(o, gnorm) * _silu(xg)


def _hgrn_prompt_body(hq_ref, hf_ref, hi_ref, hg_ref, lb_ref, gn_ref, o_ref, s_ref, st_ref, *, chunk, sub):
    gi = pl.program_id(1)

    @pl.when(gi == 0)
    def _():
        st_ref[...] = jnp.zeros_like(st_ref)

    n_chunks = hq_ref.shape[0] // chunk
    gn = gn_ref[...]

    def step(ci, carry):
        rows = pl.ds(pl.multiple_of(ci * chunk, chunk), chunk)
        for h in range(HG_HEADS):
            cols = slice(h * HG_KDIM, (h + 1) * HG_KDIM)
            o, st = _hgrn_chunk(hq_ref[rows, cols], hf_ref[rows, cols], hi_ref[rows, cols],
                                lb_ref[:, cols], st_ref[h], sub)
            st_ref[h] = st
            o_ref[rows, cols] = _hgrn_gate(o, gn, hg_ref[rows, cols])
        return carry

    lax.fori_loop(0, n_chunks, step, 0)

    @pl.when(gi == pl.num_programs(1) - 1)
    def _():
        for h in range(HG_HEADS):
            s_ref[h] = st_ref[h].T


def _hgrn_prompt(hq, hf, hi, hg, lb, gnorm, *, batch, seq, rows_per_step):
    ng = seq // rows_per_step
    chunk = min(HG_CHUNK, seq)
    sub = min(HG_SUB, chunk)

    def tok():
        return pl.BlockSpec((rows_per_step, HG_WIDTH), lambda bi, gi: (bi * ng + gi, 0))

    vmem = 2 * 5 * rows_per_step * HG_WIDTH * 4 + 3 * HG_HEADS * HG_KDIM * HG_VDIM * 4 + (8 << 20)
    return pl.pallas_call(
        functools.partial(_hgrn_prompt_body, chunk=chunk, sub=sub),
        grid=(batch, ng),
        in_specs=[tok(), tok(), tok(), tok(),
                  pl.BlockSpec((1, HG_WIDTH), lambda bi, gi: (0, 0)),
                  pl.BlockSpec((1, HG_VDIM), lambda bi, gi: (0, 0))],
        out_specs=[tok(), pl.BlockSpec((None, HG_HEADS, HG_KDIM, HG_VDIM), lambda bi, gi: (bi, 0, 0, 0))],
        out_shape=[jax.ShapeDtypeStruct((batch * seq, HG_WIDTH), F32),
                   jax.ShapeDtypeStruct((batch, HG_HEADS, HG_KDIM, HG_VDIM), F32)],
        scratch_shapes=[pltpu.VMEM((HG_HEADS, HG_KDIM, HG_VDIM), F32)],
        compiler_params=_cparams(("parallel", "arbitrary"), vmem),
        name="hgrn_prompt",
    )(hq, hf, hi, hg, lb, gnorm)


def _hgrn_sample_body(hq_ref, hf_ref, hi_ref, hg_ref, lb_ref, gn_ref, s0_ref, o_ref, s_ref, *, tq, nb):
    gn = gn_ref[...]

    def step(bi, carry):
        rows = pl.ds(pl.multiple_of(bi * tq, tq), tq)
        for h in range(HG_HEADS):
            cols = slice(h * HG_KDIM, (h + 1) * HG_KDIM)
            o, st = _hgrn_chunk(hq_ref[rows, cols], hf_ref[rows, cols], hi_ref[rows, cols],
                                lb_ref[:, cols], s0_ref[bi, h].T, tq)
            s_ref[bi, h] = st.T
            o_ref[rows, cols] = _hgrn_gate(o, gn, hg_ref[rows, cols])
        return carry

    lax.fori_loop(0, nb, step, 0)


def _hgrn_sample(hq, hf, hi, hg, lb, gnorm, state, *, layer, row0, batch, tq, nb):
    assert tq <= HG_CHUNK
    blk0 = row0 // (nb * tq)

    def tok():
        return pl.BlockSpec((nb * tq, HG_WIDTH), lambda gi: (blk0 + gi, 0))

    st_blk = (nb, HG_HEADS, HG_KDIM, HG_VDIM)
    vmem = 2 * (5 * nb * tq * HG_WIDTH * 4 + 2 * nb * HG_HEADS * HG_KDIM * HG_VDIM * 4) + (8 << 20)
    return pl.pallas_call(
        functools.partial(_hgrn_sample_body, tq=tq, nb=nb),
        grid=(batch // nb,),
        in_specs=[tok(), tok(), tok(), tok(),
                  pl.BlockSpec((1, HG_WIDTH), lambda gi: (0, 0)),
                  pl.BlockSpec((1, HG_VDIM), lambda gi: (0, 0)),
                  pl.BlockSpec((None,) + st_blk, lambda gi: (layer, gi, 0, 0, 0))],
        out_specs=[pl.BlockSpec((nb * tq, HG_WIDTH), lambda gi: (gi, 0)),
                   pl.BlockSpec(st_blk, lambda gi: (gi, 0, 0, 0))],
        out_shape=[jax.ShapeDtypeStruct((batch * tq, HG_WIDTH), F32),
                   jax.ShapeDtypeStruct((batch, HG_HEADS, HG_KDIM, HG_VDIM), F32)],
        compiler_params=_cparams(("parallel",), vmem),
        name="hgrn_sample",
    )(hq, hf, hi, hg, lb, gnorm, state)


def _mix_body(attp_ref, atts_ref, ohgp_ref, ohgs_ref, h_ref, g_ref, w_ref, o_ref, *, n_prompt_tiles):
    is_prompt = pl.program_id(0) < n_prompt_tiles
    att = jnp.where(is_prompt, attp_ref[...], atts_ref[...]).astype(BF16)
    ohg = jnp.where(is_prompt, ohgp_ref[...], ohgs_ref[...]).astype(BF16)
    mix = _dot(att, w_ref[0:ATT_WIDTH, :]) + _dot(ohg, w_ref[ATT_WIDTH:ATT_WIDTH + HG_WIDTH, :])
    o_ref[...] = h_ref[...] + _rms(mix, g_ref[...])


def _mix(att_p, att_s, ohg_p, ohg_s, h, g, w_out, *, tm):
    n, d = h.shape
    ntp = att_p.shape[0] // tm
    nts = att_s.shape[0] // tm
    assert att_p.shape[0] % tm == 0 and att_s.shape[0] % tm == 0 and ntp + nts == n // tm
    p_map = lambda i: (jnp.minimum(i, ntp - 1), 0)
    s_map = lambda i: (jnp.maximum(i - ntp, 0), 0)
    vmem = 2 * (4 * tm * 512 * 4 + 2 * tm * d * 4 + (ATT_WIDTH + HG_WIDTH) * d * 2) + 3 * tm * d * 4 + (4 << 20)
    return pl.pallas_call(
        functools.partial(_mix_body, n_prompt_tiles=ntp),
        grid=(n // tm,),
        in_specs=[pl.BlockSpec((tm, ATT_WIDTH), p_map),
                  pl.BlockSpec((tm, ATT_WIDTH), s_map),
                  pl.BlockSpec((tm, HG_WIDTH), p_map),
                  pl.BlockSpec((tm, HG_WIDTH), s_map),
                  pl.BlockSpec((tm, d), lambda i: (i, 0)),
                  pl.BlockSpec((1, d), lambda i: (0, 0)),
                  pl.BlockSpec((ATT_WIDTH + HG_WIDTH, d), lambda i: (0, 0))],
        out_specs=pl.BlockSpec((tm, d), lambda i: (i, 0)),
        out_shape=jax.ShapeDtypeStruct((n, d), F32),
        compiler_params=_cparams(("parallel",), vmem),
        name="mix",
    )(att_p, att_s, ohg_p, ohg_s, h, g, w_out)


def _ple_body(h_ref, p_ref, gpre_ref, gpost_ref, wgate_ref, wproj_ref, o_ref):
    h = h_ref[...]
    gate = jax.nn.sigmoid(_dot(_rms(h, gpre_ref[...]).astype(BF16), wgate_ref[...]))
    pe = _dot(p_ref[...].astype(BF16), wproj_ref[...])
    o_ref[...] = h + _rms(pe * gate, gpost_ref[...])


def _ple(h, p, g_pre, g_post, w_gate, w_proj, *, tm):
    n, d = h.shape
    pd = p.shape[1]
    vmem = 2 * (2 * tm * d * 4 + tm * pd * 4 + d * d * 2 + pd * d * 2) + 3 * tm * d * 4 + (4 << 20)
    return pl.pallas_call(
        _ple_body,
        grid=(n // tm,),
        in_specs=[pl.BlockSpec((tm, d), lambda i: (i, 0)),
                  pl.BlockSpec((tm, pd), lambda i: (i, 0)),
                  pl.BlockSpec((1, d), lambda i: (0, 0)),
                  pl.BlockSpec((1, d), lambda i: (0, 0)),
                  pl.BlockSpec((d, d), lambda i: (0, 0)),
                  pl.BlockSpec((pd, d), lambda i: (0, 0))],
        out_specs=pl.BlockSpec((tm, d), lambda i: (i, 0)),
        out_shape=jax.ShapeDtypeStruct((n, d), F32),
        compiler_params=_cparams(("parallel",), vmem),
        name="ple",
    )(h, p, g_pre, g_post, w_gate, w_proj)


def _rot_cols(w, heads):
    d = w.shape[0]
    w4 = w.reshape(d, heads, 2, HEAD_DIM // 2)
    return jnp.concatenate([-w4[:, :, 1:2], w4[:, :, 0:1]], axis=2).reshape(d, heads * HEAD_DIM)


def _pack_w_in(w):
    d = w.shape[0]
    widths = (ATT_WIDTH, KV_WIDTH, KV_WIDTH, IDX_HEADS * IDX_DIM, IDX_DIM, IDX_HEADS,
              HG_HEADS * HG_KDIM, HG_HEADS * HG_KDIM, HG_WIDTH, HG_WIDTH)
    parts = []
    acc = 0
    for wd in widths:
        parts.append(w[:, acc:acc + wd])
        acc += wd
    q, k, v, iq, ik, iw, hq, hf, hi, hg = parts
    z = lambda n: jnp.zeros((d, n), w.dtype)
    ikw = jnp.concatenate([ik, iw, z(LANES - IDX_DIM - IDX_HEADS)], axis=1)
    ikr = jnp.concatenate([_rot_cols(ik, 1), z(LANES - IDX_DIM)], axis=1)
    cat = jnp.concatenate([q, k, iq, ikw, v, hq, hf, hi, hg,
                           _rot_cols(q, N_HEADS), _rot_cols(k, KV_HEADS), _rot_cols(iq, IDX_HEADS), ikr], axis=1)
    assert cat.shape[1] == _C_END
    w_t = jnp.concatenate([k, _rot_cols(k, KV_HEADS), v, ik, _rot_cols(ik, 1)], axis=1).T
    assert w_t.shape[0] == _R_END
    return cat.astype(BF16), w_t.astype(BF16)


def _rope_tables(seq, past_len, tq, tm):
    half = HEAD_DIM // 2
    inv_freq = ROPE_THETA ** (-jnp.arange(half, dtype=F32) / half)
    pos_p = jnp.arange(seq, dtype=jnp.int32)
    pos_s = past_len + (jnp.arange(tm, dtype=jnp.int32) % tq)
    pos = jnp.concatenate([pos_p, pos_s]).astype(F32)
    ang = pos[:, None] * inv_freq[None, :]
    reps = LANES // half
    cos = jnp.tile(jnp.cos(ang), (1, reps)).reshape(seq // tm + 1, tm, LANES)
    sin = jnp.tile(jnp.sin(ang), (1, reps)).reshape(seq // tm + 1, tm, LANES)
    nt = seq // tm
    cos_t = jnp.swapaxes(cos[:nt], 1, 2)
    sin_t = jnp.swapaxes(sin[:nt], 1, 2)
    return cos, sin, cos_t, sin_t


def _lower_bounds(logits):
    sm = jax.nn.softmax(logits.astype(F32), axis=0)
    return jnp.cumsum(sm, axis=0) - sm[0:1]


def _feature_major(cache, width):
    depth, pool, page = cache.shape[:3]
    c = cache.reshape(depth, pool, page, width)
    return jnp.swapaxes(c, 2, 3)


TM_DENSE = 1024
TM_PROJ = 512
TF_FFN = 256
Q_BLOCK = 128
N_KEY_LEN = 8
DSA_SEQS = 4
HG_ROWS = 512
HG_SEQS = 8


def kernel(x_prompt, x_sample, p_prompt, p_sample, cache_k, cache_v, cache_idx_k, state_hgrn, page_table,
           n_f1_pre, n_f1_post, w_f1_gate, w_f1_up, w_f1_down,
           n_mix_pre, n_mix_post, w_in, hg_lb_logits, hg_norm, w_out,
           n_f2_pre, n_f2_post, w_f2_gate, w_f2_up, w_f2_down,
           n_ple_pre, n_ple_post, w_ple_proj, w_ple_gate):
    bp, tp, d = x_prompt.shape
    bs, ts, _ = x_sample.shape
    depth = w_in.shape[0]
    n_p, n_s = bp * tp, bs * ts
    page = cache_k.shape[2]
    past_len = page_table.shape[1] * page

    h = jnp.concatenate([x_prompt.reshape(n_p, d), x_sample.reshape(n_s, d)], axis=0)
    ckt = _feature_major(cache_k, KV_WIDTH)
    cvt = _feature_major(cache_v, KV_WIDTH)
    cikt = _feature_major(cache_idx_k, IDX_DIM)
    cos_tab, sin_tab, cos_t, sin_t = _rope_tables(tp, past_len, ts, TM_PROJ)
    lbs = _lower_bounds(hg_lb_logits)
    row = lambda a: a.reshape(1, -1)
    bf = lambda a: a.astype(BF16)

    outs = {name: [] for name in ("kp", "vp", "ikp", "sp", "ks", "vs", "iks", "ss")}
    for i in range(depth):
        h1 = _ffn(h, row(n_f1_pre[i]), row(n_f1_post[i]), bf(w_f1_gate[i]), bf(w_f1_up[i]), bf(w_f1_down[i]),
                  tm=TM_DENSE, tf=TF_FFN)
        w_cat, w_t = _pack_w_in(w_in[i])
        g_mix = row(n_mix_pre[i])
        q16, k, v, iq16, ikw, hq, hf, hi, hg, ikw16 = _proj(
            h1, g_mix, w_cat, cos_tab, sin_tab,
            tm=TM_PROJ, tiles_per_seq=tp // TM_PROJ, n_prompt_tiles=n_p // TM_PROJ)
        kt, vt, ikt, kt16, vt16 = _proj_t(h1, g_mix, w_t, cos_t, sin_t, batch=bp, seq=tp, tm=TM_PROJ)

        att_p = _dsa_prompt(q16, iq16, ikw, ikw16, kt16, vt16, batch=bp, seq=tp, qb=min(Q_BLOCK, tp),
                            n_len=N_KEY_LEN)
        att_s = _dsa_sample(page_table, q16, iq16, ikw, k, v, ckt, cvt, cikt,
                            layer=i, row0=n_p, batch=bs, tq=ts, nb=DSA_SEQS)
        lb = row(lbs[i])
        gn = row(hg_norm[i])
        ohg_p, s_p = _hgrn_prompt(hq, hf, hi, hg, lb, gn, batch=bp, seq=tp, rows_per_step=HG_ROWS)
        ohg_s, s_s = _hgrn_sample(hq, hf, hi, hg, lb, gn, state_hgrn,
                                  layer=i, row0=n_p, batch=bs, tq=ts, nb=HG_SEQS)

        h2 = _mix(att_p, att_s, ohg_p, ohg_s, h1, row(n_mix_post[i]), bf(w_out[i]), tm=TM_DENSE)
        h3 = _ffn(h2, row(n_f2_pre[i]), row(n_f2_post[i]), bf(w_f2_gate[i]), bf(w_f2_up[i]), bf(w_f2_down[i]),
                  tm=TM_DENSE, tf=TF_FFN)
        p_emb = jnp.concatenate([p_prompt[i].reshape(n_p, -1), p_sample[i].reshape(n_s, -1)], axis=0)
        h = _ple(h3, p_emb, row(n_ple_pre[i]), row(n_ple_post[i]), bf(w_ple_gate[i]), bf(w_ple_proj[i]),
                 tm=TM_DENSE)

        outs["kp"].append(jnp.transpose(kt.reshape(bp, KV_HEADS, HEAD_DIM, tp), (0, 3, 1, 2)))
        outs["vp"].append(jnp.transpose(vt.reshape(bp, KV_HEADS, HEAD_DIM, tp), (0, 3, 1, 2)))
        outs["ikp"].append(jnp.swapaxes(ikt, 1, 2))
        outs["sp"].append(s_p)
        outs["ks"].append(k[n_p:].reshape(bs, ts, KV_HEADS, HEAD_DIM))
        outs["vs"].append(v[n_p:].reshape(bs, ts, KV_HEADS, HEAD_DIM))
        outs["iks"].append(ikw[n_p:, :IDX_DIM].reshape(bs, ts, IDX_DIM))
        outs["ss"].append(s_s.astype(state_hgrn.dtype))

    st = lambda name: jnp.stack(outs[name])
    return (h[:n_p].reshape(bp, tp, d), h[n_p:].reshape(bs, ts, d),
            st("kp"), st("vp"), st("ikp"), st("sp"), st("ks"), st("vs"), st("iks"), st("ss"))
```

```python
import functools

import jax
import jax.numpy as jnp
from jax import lax
from jax.experimental import pallas as pl
from jax.experimental.pallas import tpu as pltpu

HEAD_DIM = 64
N_HEADS = 8
KV_HEADS = 2
IDX_HEADS = 8
IDX_DIM = 64
TOPK_MAX = 256
HG_KDIM = 128
HG_VDIM = 128
HG_HEADS = 4
HG_CHUNK = 64
ROPE_THETA = 10000.0
EPS = 1e-6
ATT_WIDTH = N_HEADS * HEAD_DIM
KV_WIDTH = KV_HEADS * HEAD_DIM
HG_WIDTH = HG_HEADS * HG_VDIM
QK_SCALE = HEAD_DIM ** -0.5
IDX_SCALE = IDX_DIM ** -0.5

LANES = 128
SUBLANES = 8
VMEM_BUDGET_BYTES = 56 * 1024 * 1024

NEG_BIG = -1e30

F32 = jnp.float32
BF16 = jnp.bfloat16


def _cparams(sem, vmem_bytes):
    return pltpu.CompilerParams(dimension_semantics=sem,
                                vmem_limit_bytes=int(min(max(vmem_bytes, 16 << 20), VMEM_BUDGET_BYTES)))


def _rms(x, g):
    return x * lax.rsqrt(jnp.mean(x * x, axis=-1, keepdims=True) + EPS) * g


def _silu(x):
    return x * jax.nn.sigmoid(x)


def _dot(a, b):
    return jnp.dot(a, b, preferred_element_type=F32)


def _dot_nt(a, b):
    return lax.dot_general(a, b, (((1,), (1,)), ((), ())), preferred_element_type=F32)


def _dot_tn(a, b):
    return lax.dot_general(a, b, (((0,), (0,)), ((), ())), preferred_element_type=F32)


def _split_maps(n_first_tiles):
    return (lambda i, *_: (jnp.minimum(i, n_first_tiles - 1), 0),
            lambda i, *_: (jnp.maximum(i - n_first_tiles, 0), 0))


def _pick(first_ref, rest_ref, n_first_tiles):
    return jnp.where(pl.program_id(0) < n_first_tiles, first_ref[...], rest_ref[...])


def _ffn_body(*refs, n_first_tiles):
    n_x = len(refs) - 8
    gpre_ref, gpost_ref, wg_ref, wu_ref, wd_ref, o_ref, xn_ref, acc_ref = refs[n_x:]
    j = pl.program_id(1)

    def x():
        return refs[0][...] if n_x == 1 else _pick(refs[0], refs[1], n_first_tiles)

    @pl.when(j == 0)
    def _():
        xn_ref[...] = _rms(x(), gpre_ref[...]).astype(BF16)
        acc_ref[...] = jnp.zeros_like(acc_ref)

    xn = xn_ref[...]
    a = _dot(xn, wg_ref[...])
    b = _dot(xn, wu_ref[...])
    acc_ref[...] += _dot((_silu(a) * b).astype(BF16), wd_ref[...])

    @pl.when(j == pl.num_programs(1) - 1)
    def _():
        o_ref[...] = x() + 0.5 * _rms(acc_ref[...], gpost_ref[...])


def _ffn(xs, g_pre, g_post, wg, wu, wd, *, tm, tf):
    n = sum(x.shape[0] for x in xs)
    d = xs[0].shape[1]
    dff = wg.shape[1]
    n_first = xs[0].shape[0] // tm
    assert all(x.shape[0] % tm == 0 for x in xs) and dff % tf == 0
    if len(xs) == 1:
        x_specs = [pl.BlockSpec((tm, d), lambda i, j: (i, 0))]
    else:
        x_specs = [pl.BlockSpec((tm, d), m) for m in _split_maps(n_first)]
    vmem = 2 * ((1 + len(xs)) * tm * d * 4 + 3 * d * tf * 2) + tm * d * 6 + 4 * tm * tf * 4 + (4 << 20)
    return pl.pallas_call(
        functools.partial(_ffn_body, n_first_tiles=n_first),
        grid=(n // tm, dff // tf),
        in_specs=x_specs + [
            pl.BlockSpec((1, d), lambda i, j: (0, 0)),
            pl.BlockSpec((1, d), lambda i, j: (0, 0)),
            pl.BlockSpec((d, tf), lambda i, j: (0, j)),
            pl.BlockSpec((d, tf), lambda i, j: (0, j)),
            pl.BlockSpec((tf, d), lambda i, j: (j, 0)),
        ],
        out_specs=pl.BlockSpec((tm, d), lambda i, j: (i, 0)),
        out_shape=jax.ShapeDtypeStruct((n, d), F32),
        scratch_shapes=[pltpu.VMEM((tm, d), BF16), pltpu.VMEM((tm, d), F32)],
        compiler_params=_cparams(("parallel", "arbitrary"), vmem),
        name="ffn",
    )(*xs, g_pre, g_post, wg, wu, wd)


_C_Q, _C_K, _C_IQ, _C_IKW, _C_V, _C_HQ, _C_HF, _C_HI, _C_HG = 0, 512, 640, 1152, 1280, 1408, 1920, 2432, 2944
_C_QR, _C_KR, _C_IQR, _C_IKR, _C_END = 3456, 3968, 4096, 4608, 4736


def _proj_body(h_ref, g_ref, w_ref, cos_ref, sin_ref,
               q_ref, k_ref, v_ref, iq_ref, ikw_ref, hq_ref, hf_ref, hi_ref, hg_ref, ikw16_ref):
    u = _rms(h_ref[...], g_ref[...]).astype(BF16)

    def mm(lo, hi):
        return _dot(u, w_ref[:, lo:hi])

    cos = cos_ref[...]
    sin = sin_ref[...]
    cos4 = jnp.concatenate([cos] * 4, axis=1)
    sin4 = jnp.concatenate([sin] * 4, axis=1)
    q_ref[...] = ((mm(_C_Q, _C_K) * cos4 + mm(_C_QR, _C_KR) * sin4) * QK_SCALE).astype(BF16)
    k_ref[...] = mm(_C_K, _C_IQ) * cos + mm(_C_KR, _C_IQR) * sin
    iq_ref[...] = ((mm(_C_IQ, _C_IKW) * cos4 + mm(_C_IQR, _C_IKR) * sin4) * IDX_SCALE).astype(BF16)
    lane = lax.broadcasted_iota(jnp.int32, cos.shape, 1)
    cos_ikw = jnp.where(lane < IDX_DIM, cos, IDX_HEADS ** -0.5)
    ikw = mm(_C_IKW, _C_V) * cos_ikw + mm(_C_IKR, _C_END) * sin
    ikw_ref[...] = ikw
    ikw16_ref[...] = ikw.astype(BF16)
    v_ref[...] = mm(_C_V, _C_HQ)
    hq_ref[...] = mm(_C_HQ, _C_HF)
    hf_ref[...] = mm(_C_HF, _C_HI)
    hi_ref[...] = mm(_C_HI, _C_HG)
    hg_ref[...] = mm(_C_HG, _C_QR)


def _proj(h, g, w_cat, cos_tab, sin_tab, *, tm, tiles_per_seq, n_prompt_tiles):
    n, d = h.shape
    outs = ((512, BF16), (128, F32), (128, F32), (512, BF16), (128, F32),
            (512, F32), (512, F32), (512, F32), (512, F32), (128, BF16))

    def tab_map(i):
        return (jnp.where(i < n_prompt_tiles, i % tiles_per_seq, tiles_per_seq), 0, 0)

    out_bytes = sum(w * jnp.dtype(t).itemsize for w, t in outs)
    vmem = 2 * (tm * d * 4 + d * _C_END * 2 + tm * out_bytes + 2 * tm * LANES * 4) + 6 * tm * 512 * 4 + (4 << 20)
    return pl.pallas_call(
        _proj_body,
        grid=(n // tm,),
        in_specs=[
            pl.BlockSpec((tm, d), lambda i: (i, 0)),
            pl.BlockSpec((1, d), lambda i: (0, 0)),
            pl.BlockSpec((d, _C_END), lambda i: (0, 0)),
            pl.BlockSpec((None, tm, LANES), tab_map),
            pl.BlockSpec((None, tm, LANES), tab_map),
        ],
        out_specs=[pl.BlockSpec((tm, w), lambda i: (i, 0)) for w, _ in outs],
        out_shape=[jax.ShapeDtypeStruct((n, w), t) for w, t in outs],
        compiler_params=_cparams(("parallel",), vmem),
        name="proj",
    )(h, g, w_cat, cos_tab, sin_tab)


_R_K, _R_KR, _R_V, _R_IK, _R_IKR, _R_END = 0, 128, 256, 384, 448, 512


def _proj_t_body(h_ref, g_ref, w_ref, cos_ref, sin_ref, k_ref, v_ref, ik_ref, k16_ref, v16_ref):
    u = _rms(h_ref[...], g_ref[...]).astype(BF16)
    r = _dot_nt(w_ref[...], u)
    cos = cos_ref[...]
    sin = sin_ref[...]
    k = r[_R_K:_R_KR] * cos + r[_R_KR:_R_V] * sin
    v = r[_R_V:_R_IK]
    ik = r[_R_IK:_R_IKR] * cos[0:IDX_DIM] + r[_R_IKR:_R_END] * sin[0:IDX_DIM]
    k_ref[...] = k
    v_ref[...] = v
    ik_ref[...] = ik
    k16_ref[...] = k.astype(BF16)
    v16_ref[...] = v.astype(BF16)


def _proj_t(h, g, w_t, cos_t, sin_t, *, batch, seq, tm):
    d = h.shape[1]
    nt = seq // tm
    outs = ((KV_WIDTH, F32), (KV_WIDTH, F32), (IDX_DIM, F32), (KV_WIDTH, BF16), (KV_WIDTH, BF16))
    vmem = 2 * (tm * d * 4 + _R_END * d * 2 + 2 * LANES * tm * 4 + 3 * LANES * tm * 6) + 4 * _R_END * tm * 4 + (4 << 20)
    return pl.pallas_call(
        _proj_t_body,
        grid=(batch, nt),
        in_specs=[
            pl.BlockSpec((tm, d), lambda b, t: (b * nt + t, 0)),
            pl.BlockSpec((1, d), lambda b, t: (0, 0)),
            pl.BlockSpec((_R_END, d), lambda b, t: (0, 0)),
            pl.BlockSpec((None, LANES, tm), lambda b, t: (t, 0, 0)),
            pl.BlockSpec((None, LANES, tm), lambda b, t: (t, 0, 0)),
        ],
        out_specs=[pl.BlockSpec((None, w, tm), lambda b, t: (b, 0, t)) for w, _ in outs],
        out_shape=[jax.ShapeDtypeStruct((batch, w, seq), t) for w, t in outs],
        compiler_params=_cparams(("parallel", "parallel"), vmem),
        name="proj_t",
    )(h, g, w_t, cos_t, sin_t)


def _tri_incl(n):
    a = lax.broadcasted_iota(jnp.int32, (n, n), 0)
    b = lax.broadcasted_iota(jnp.int32, (n, n), 1)
    return a, b


I16_MIN = -32768
I16_ROWS = 16


def _tree_sum(xs):
    xs = list(xs)
    while len(xs) > 1:
        nxt = [xs[i] + xs[i + 1] for i in range(0, len(xs) - 1, 2)]
        if len(xs) % 2:
            nxt.append(xs[-1])
        xs = nxt
    return xs[0]


def _count_cols(mask, dtype):
    rows = I16_ROWS if dtype == jnp.int16 else SUBLANES
    one = jnp.where(mask, jnp.ones((), dtype), jnp.zeros((), dtype))
    part = _tree_sum([one[c * rows:(c + 1) * rows] for c in range(mask.shape[0] // rows)])
    return jnp.sum(part.astype(F32), axis=0, keepdims=True)


def _search16(x16, k):
    def body(i, t):
        cand = t + lax.shift_left(jnp.int32(1), 15 - i)
        cnt = _count_cols(x16 >= cand.astype(jnp.int16), jnp.int16)
        return jnp.where(cnt >= k, cand, t)

    return lax.fori_loop(0, 16, body, jnp.full(k.shape, I16_MIN, jnp.int32))


def _topk_mask_t(score, adm, k):
    l, n = score.shape
    score = jnp.where(score == 0.0, 0.0, score)
    score = jnp.where(adm, score, -jnp.inf)
    bits = lax.bitcast_convert_type(score, jnp.int32)
    key = bits ^ ((bits >> 31) & 0x7FFFFFFF)
    kf = jnp.full((1, n), float(k), F32)
    hi = (key >> 16).astype(jnp.int16)
    t_hi = _search16(hi, kf)
    t_hi16 = t_hi.astype(jnp.int16)
    above = _count_cols(hi > t_hi16, jnp.int16)
    lo = ((key & 0xFFFF) + I16_MIN).astype(jnp.int16)
    lo = jnp.where(hi == t_hi16, lo, jnp.int16(I16_MIN))
    t_lo = _search16(lo, kf - above)
    t = lax.shift_left(t_hi, 16) | (t_lo - I16_MIN)
    gt = key > t
    eq = key == t
    need = kf - _count_cols(gt, jnp.float32)
    a, b = _tri_incl(LANES)
    tril = jnp.where(b <= a, 1.0, 0.0).astype(BF16)
    carry = jnp.zeros((1, n), F32)
    sel = []
    for c in range(l // LANES):
        rows = slice(c * LANES, (c + 1) * LANES)
        pref = _dot(tril, jnp.where(eq[rows], 1.0, 0.0).astype(BF16)) + carry
        carry = pref[LANES - 1:LANES]
        take = jnp.logical_or(gt[rows], jnp.logical_and(eq[rows], pref <= need))
        sel.append(jnp.where(jnp.logical_and(take, adm[rows]), 1.0, 0.0))
    return jnp.concatenate(sel, axis=0) if len(sel) > 1 else sel[0]


def _dsa_prompt_tile(q_ref, iq_ref, iw_ref, kt_ref, vt_ref, ik_ref, o_ref, *, q0, l, topk):
    qb = q_ref.shape[0]
    iq_t = iq_ref[...].astype(F32).T.astype(BF16)
    w_t = iw_ref[...].T[IDX_DIM:IDX_DIM + IDX_HEADS]
    ik = ik_ref[0:l, 0:IDX_DIM]
    score = jnp.zeros((l, qb), F32)
    for h in range(IDX_HEADS):
        lg = _dot(ik, iq_t[h * IDX_DIM:(h + 1) * IDX_DIM])
        score = score + w_t[h:h + 1] * jnp.maximum(lg, 0.0)

    kpos = lax.broadcasted_iota(jnp.int32, (l, qb), 0)
    qpos = q0 + lax.broadcasted_iota(jnp.int32, (l, qb), 1)
    sel_t = _topk_mask_t(score, kpos <= qpos, topk)
    bias = ((sel_t - 1.0) * (-NEG_BIG)).T

    q = q_ref[...]
    group = N_HEADS // KV_HEADS
    for h in range(N_HEADS):
        g = h // group
        s = _dot(q[:, h * HEAD_DIM:(h + 1) * HEAD_DIM], kt_ref[g * HEAD_DIM:(g + 1) * HEAD_DIM, 0:l]) + bias
        p = jnp.exp(s - jnp.max(s, axis=-1, keepdims=True))
        den = jnp.sum(p, axis=-1, keepdims=True)
        o = _dot_nt(p.astype(BF16), vt_ref[g * HEAD_DIM:(g + 1) * HEAD_DIM, 0:l])
        o_ref[:, h * HEAD_DIM:(h + 1) * HEAD_DIM] = o / den


def _dsa_prompt_body(q_ref, iq_ref, iw_ref, kt_ref, vt_ref, ik_ref, o_ref, *, topk, n_len):
    qi = pl.program_id(1)
    nq = pl.num_programs(1)
    qb = q_ref.shape[0]
    seq = kt_ref.shape[1]
    per = nq // n_len
    for v in range(n_len):
        l = (v + 1) * per * qb

        @pl.when(qi // per == v)
        def _(l=l):
            _dsa_prompt_tile(q_ref, iq_ref, iw_ref, kt_ref, vt_ref, ik_ref, o_ref,
                             q0=qi * qb, l=min(l, seq), topk=topk)


def _dsa_prompt(q16, iq16, ikw, ikw16, kt16, vt16, *, batch, seq, qb, n_len):
    topk = min(TOPK_MAX, seq // 4)
    nq = seq // qb
    n_len = n_len if nq % n_len == 0 else 1
    vmem = 2 * (2 * qb * 512 * 2 + qb * LANES * 4 + qb * 512 * 4 + seq * (2 * KV_WIDTH + LANES) * 2) \
        + 16 * qb * seq * 4 + (4 << 20)
    return pl.pallas_call(
        functools.partial(_dsa_prompt_body, topk=topk, n_len=n_len),
        grid=(batch, nq),
        in_specs=[
            pl.BlockSpec((qb, ATT_WIDTH), lambda bi, qi: (bi * nq + qi, 0)),
            pl.BlockSpec((qb, IDX_HEADS * IDX_DIM), lambda bi, qi: (bi * nq + qi, 0)),
            pl.BlockSpec((qb, LANES), lambda bi, qi: (bi * nq + qi, 0)),
            pl.BlockSpec((None, KV_WIDTH, seq), lambda bi, qi: (bi, 0, 0)),
            pl.BlockSpec((None, KV_WIDTH, seq), lambda bi, qi: (bi, 0, 0)),
            pl.BlockSpec((seq, LANES), lambda bi, qi: (bi, 0)),
        ],
        out_specs=pl.BlockSpec((qb, ATT_WIDTH), lambda bi, qi: (bi * nq + qi, 0)),
        out_shape=jax.ShapeDtypeStruct((batch * seq, ATT_WIDTH), F32),
        compiler_params=_cparams(("parallel", "parallel"), vmem),
        name="dsa_prompt",
    )(q16, iq16, ikw, kt16, vt16, ikw16)


def _dsa_sample_body(pt_ref, q_ref, iq_ref, ikw_ref, kn_ref, vn_ref, *rest, n_pages, page, tq, nb, topk):
    del pt_ref
    npg = nb * n_pages
    kp_refs, vp_refs, ip_refs = rest[0:npg], rest[npg:2 * npg], rest[2 * npg:3 * npg]
    o_ref = rest[3 * npg]
    past = n_pages * page
    l = past + LANES
    group = N_HEADS // KV_HEADS

    def pad_rows(x):
        return jnp.concatenate([x, jnp.zeros((LANES - tq, x.shape[1]), x.dtype)], axis=0)

    def pages(refs, bj):
        return jnp.concatenate([refs[bj * n_pages + p][...] for p in range(n_pages)], axis=1).astype(BF16)

    iq_all = iq_ref[...].astype(F32)
    q_all = q_ref[...].astype(F32)
    ikw_all = ikw_ref[...]

    scores = []
    for bj in range(nb):
        rows = slice(bj * tq, (bj + 1) * tq)
        iq = iq_all[rows]
        iq_hm = jnp.concatenate([iq[:, h * IDX_DIM:(h + 1) * IDX_DIM] for h in range(IDX_HEADS)],
                                axis=0).astype(BF16)
        ik_new = pad_rows(ikw_all[rows, 0:IDX_DIM]).astype(BF16)
        lg = jnp.concatenate([_dot(iq_hm, pages(ip_refs, bj)), _dot_nt(iq_hm, ik_new)], axis=1)
        lg = jnp.maximum(lg, 0.0)
        w = ikw_all[rows, IDX_DIM:IDX_DIM + IDX_HEADS]
        sc = jnp.zeros((tq, l), F32)
        for h in range(IDX_HEADS):
            sc = sc + w[:, h:h + 1] * lg[h * tq:(h + 1) * tq, :]
        scores.append(sc)
    nq = nb * tq
    score_t = jnp.concatenate(scores + [jnp.zeros((LANES - nq, l), F32)], axis=0).T
    kidx = lax.broadcasted_iota(jnp.int32, (l, LANES), 0)
    qidx = lax.broadcasted_iota(jnp.int32, (l, LANES), 1)
    adm = kidx <= past + qidx % tq
    adm = jnp.logical_and(adm, kidx < past + tq)
    adm = jnp.logical_and(adm, qidx < nq)
    sel_t = _topk_mask_t(score_t, adm, topk)
    bias = ((sel_t - 1.0) * (-NEG_BIG)).T

    for bj in range(nb):
        rows = slice(bj * tq, (bj + 1) * tq)
        q = q_all[rows]
        bias_g = jnp.concatenate([bias[rows]] * group, axis=0)
        kt = pages(kp_refs, bj)
        vt = pages(vp_refs, bj)
        k_new = pad_rows(kn_ref[rows, :]).astype(BF16)
        v_new = pad_rows(vn_ref[rows, :]).astype(BF16)
        for g in range(KV_HEADS):
            fs = slice(g * HEAD_DIM, (g + 1) * HEAD_DIM)
            qg = jnp.concatenate([q[:, h * HEAD_DIM:(h + 1) * HEAD_DIM]
                                  for h in range(g * group, (g + 1) * group)], axis=0).astype(BF16)
            s = jnp.concatenate([_dot(qg, kt[fs]), _dot_nt(qg, k_new[:, fs])], axis=1) + bias_g
            p = jnp.exp(s - jnp.max(s, axis=-1, keepdims=True))
            den = jnp.sum(p, axis=-1, keepdims=True)
            p = p.astype(BF16)
            o = (_dot_nt(p[:, 0:past], vt[fs]) + _dot(p[:, past:l], v_new[:, fs])) / den
            for j in range(group):
                h = g * group + j
                o_ref[rows, h * HEAD_DIM:(h + 1) * HEAD_DIM] = o[j * tq:(j + 1) * tq, :]


def _dsa_sample(page_table, q16, iq16, ikw, k, v, cache_kt, cache_vt, cache_ikt, *, layer, row0, batch, tq, nb):
    n_pages = page_table.shape[1]
    page = cache_kt.shape[3]
    past = n_pages * page
    topk = min(TOPK_MAX, (past + tq) // 4)
    blk0 = row0 // (nb * tq)
    pt = page_table.reshape(-1)

    def tok(w):
        return pl.BlockSpec((nb * tq, w), lambda gi, pt_ref: (blk0 + gi, 0))

    def page_spec(w, bj, p):
        return pl.BlockSpec((None, None, w, page),
                            lambda gi, pt_ref: (layer, pt_ref[(gi * nb + bj) * n_pages + p], 0, 0))

    in_specs = [tok(ATT_WIDTH), tok(IDX_HEADS * IDX_DIM), tok(LANES), tok(KV_WIDTH), tok(KV_WIDTH)]
    for w in (KV_WIDTH, KV_WIDTH, IDX_DIM):
        in_specs += [page_spec(w, bj, p) for bj in range(nb) for p in range(n_pages)]
    vmem = 2 * nb * n_pages * page * (2 * KV_WIDTH + IDX_DIM) * 4 + 16 * nb * (past + LANES) * LANES * 4 + (8 << 20)
    npg = nb * n_pages
    return pl.pallas_call(
        functools.partial(_dsa_sample_body, n_pages=n_pages, page=page, tq=tq, nb=nb, topk=topk),
        grid_spec=pltpu.PrefetchScalarGridSpec(
            num_scalar_prefetch=1,
            grid=(batch // nb,),
            in_specs=in_specs,
            out_specs=pl.BlockSpec((nb * tq, ATT_WIDTH), lambda gi, pt_ref: (gi, 0)),
        ),
        out_shape=jax.ShapeDtypeStruct((batch * tq, ATT_WIDTH), F32),
        compiler_params=_cparams(("parallel",), vmem),
        name="dsa_sample",
    )(pt, q16, iq16, ikw, k, v, *([cache_kt] * npg), *([cache_vt] * npg), *([cache_ikt] * npg))


def _hgrn_levels(c):
    out = []
    n = c // 2
    while n >= 1:
        out.append(n)
        n //= 2
    return out


def _group_row(x, j, g):
    c, n = x.shape
    x3 = x.reshape(c // g, g, n)
    return jnp.broadcast_to(x3[:, j:j + 1, :], (c // g, g, n)).reshape(c, n)


def _boundary_rows(b, n, row):
    g = 2 * n
    if g >= SUBLANES:
        return _group_row(b, n - 1, g)
    out = _group_row(b, n - 1, SUBLANES)
    for j in range(1, SUBLANES // g):
        out = jnp.where((row % SUBLANES) // g == j, _group_row(b, j * g + n - 1, SUBLANES), out)
    return out


def _hgrn_chunk(xq, xf, xi, lb, state_t):
    c, dk = xq.shape
    f = lb + (1.0 - lb) * jax.nn.sigmoid(xf)
    logf = jnp.log(f)
    kk = 1.0 - f
    qf = _silu(xq) * (HG_KDIM ** -0.5)
    ta, tb = _tri_incl(c)
    row = lax.broadcasted_iota(jnp.int32, (c, 1), 0)
    b = logf
    step = 1
    while step < c:
        b = b + jnp.where(row >= step, pltpu.roll(b, step, 0), 0.0)
        step *= 2

    o = _dot_nt((qf * jnp.exp(b)).astype(BF16), state_t.astype(BF16))
    o = o + jnp.sum(qf * kk, axis=-1, keepdims=True) * xi
    xi16 = xi.astype(BF16)
    att = jnp.zeros((c, c), F32)
    for n in _hgrn_levels(c):
        upper = (row % (2 * n)) >= n
        e = jnp.exp(-jnp.abs(b - _boundary_rows(b, n, row)))
        m = (jnp.where(upper, qf, kk) * e).astype(BF16)
        pair = jnp.logical_and(ta // (2 * n) == tb // (2 * n),
                               jnp.logical_and((ta % (2 * n)) >= n, (tb % (2 * n)) < n))
        att = att + jnp.where(pair, _dot_nt(m, m), 0.0)
    o = o + _dot(att.astype(BF16), xi16)

    blast = b[c - 1:c]
    kd = (kk * jnp.exp(blast - b)).astype(BF16)
    new_state_t = state_t * jnp.exp(blast) + _dot_tn(xi16, kd)
    return o, new_state_t


def _hgrn_gate(o, gnorm, xg):
    return _rms(o, gnorm) * _silu(xg)


def _hgrn_prompt_body(hq_ref, hf_ref, hi_ref, hg_ref, lb_ref, gn_ref, o_ref, s_ref, st_ref, *, chunk):
    gi = pl.program_id(1)

    @pl.when(gi == 0)
    def _():
        st_ref[...] = jnp.zeros_like(st_ref)

    n_chunks = hq_ref.shape[0] // chunk
    gn = gn_ref[...]

    def step(ci, carry):
        rows = pl.ds(pl.multiple_of(ci * chunk, chunk), chunk)
        for h in range(HG_HEADS):
            cols = slice(h * HG_KDIM, (h + 1) * HG_KDIM)
            o, st = _hgrn_chunk(hq_ref[rows, cols], hf_ref[rows, cols], hi_ref[rows, cols],
                                lb_ref[:, cols], st_ref[h])
            st_ref[h] = st
            o_ref[rows, cols] = _hgrn_gate(o, gn, hg_ref[rows, cols])
        return carry

    lax.fori_loop(0, n_chunks, step, 0)

    @pl.when(gi == pl.num_programs(1) - 1)
    def _():
        for h in range(HG_HEADS):
            s_ref[h] = st_ref[h].T


def _hgrn_prompt(hq, hf, hi, hg, lb, gnorm, *, batch, seq, rows_per_step):
    ng = seq // rows_per_step
    chunk = min(HG_CHUNK, seq)

    def tok():
        return pl.BlockSpec((rows_per_step, HG_WIDTH), lambda bi, gi: (bi * ng + gi, 0))

    vmem = 2 * 5 * rows_per_step * HG_WIDTH * 4 + 3 * HG_HEADS * HG_KDIM * HG_VDIM * 4 + (8 << 20)
    return pl.pallas_call(
        functools.partial(_hgrn_prompt_body, chunk=chunk),
        grid=(batch, ng),
        in_specs=[tok(), tok(), tok(), tok(),
                  pl.BlockSpec((1, HG_WIDTH), lambda bi, gi: (0, 0)),
                  pl.BlockSpec((1, HG_VDIM), lambda bi, gi: (0, 0))],
        out_specs=[tok(), pl.BlockSpec((None, HG_HEADS, HG_KDIM, HG_VDIM), lambda bi, gi: (bi, 0, 0, 0))],
        out_shape=[jax.ShapeDtypeStruct((batch * seq, HG_WIDTH), F32),
                   jax.ShapeDtypeStruct((batch, HG_HEADS, HG_KDIM, HG_VDIM), F32)],
        scratch_shapes=[pltpu.VMEM((HG_HEADS, HG_KDIM, HG_VDIM), F32)],
        compiler_params=_cparams(("parallel", "arbitrary"), vmem),
        name="hgrn_prompt",
    )(hq, hf, hi, hg, lb, gnorm)


def _hgrn_sample_body(hq_ref, hf_ref, hi_ref, hg_ref, lb_ref, gn_ref, s0_ref, o_ref, s_ref, *, tq, nb):
    gn = gn_ref[...]

    def step(bi, carry):
        rows = pl.ds(pl.multiple_of(bi * tq, tq), tq)
        for h in range(HG_HEADS):
            cols = slice(h * HG_KDIM, (h + 1) * HG_KDIM)
            o, st = _hgrn_chunk(hq_ref[rows, cols], hf_ref[rows, cols], hi_ref[rows, cols],
                                lb_ref[:, cols], s0_ref[bi, h].T)
            s_ref[bi, h] = st.T
            o_ref[rows, cols] = _hgrn_gate(o, gn, hg_ref[rows, cols])
        return carry

    lax.fori_loop(0, nb, step, 0, unroll=2)


def _hgrn_sample(hq, hf, hi, hg, lb, gnorm, state, *, layer, row0, batch, tq, nb):
    assert tq <= HG_CHUNK
    blk0 = row0 // (nb * tq)

    def tok():
        return pl.BlockSpec((nb * tq, HG_WIDTH), lambda gi: (blk0 + gi, 0))

    st_blk = (nb, HG_HEADS, HG_KDIM, HG_VDIM)
    vmem = 2 * (5 * nb * tq * HG_WIDTH * 4 + 2 * nb * HG_HEADS * HG_KDIM * HG_VDIM * 4) + (8 << 20)
    return pl.pallas_call(
        functools.partial(_hgrn_sample_body, tq=tq, nb=nb),
        grid=(batch // nb,),
        in_specs=[tok(), tok(), tok(), tok(),
                  pl.BlockSpec((1, HG_WIDTH), lambda gi: (0, 0)),
                  pl.BlockSpec((1, HG_VDIM), lambda gi: (0, 0)),
                  pl.BlockSpec((None,) + st_blk, lambda gi: (layer, gi, 0, 0, 0))],
        out_specs=[pl.BlockSpec((nb * tq, HG_WIDTH), lambda gi: (gi, 0)),
                   pl.BlockSpec(st_blk, lambda gi: (gi, 0, 0, 0))],
        out_shape=[jax.ShapeDtypeStruct((batch * tq, HG_WIDTH), F32),
                   jax.ShapeDtypeStruct((batch, HG_HEADS, HG_KDIM, HG_VDIM), F32)],
        compiler_params=_cparams(("parallel",), vmem),
        name="hgrn_sample",
    )(hq, hf, hi, hg, lb, gnorm, state)


def _mix_body(attp_ref, atts_ref, ohgp_ref, ohgs_ref, h_ref, g_ref, w_ref, o_ref, *, n_prompt_tiles):
    att = _pick(attp_ref, atts_ref, n_prompt_tiles).astype(BF16)
    ohg = _pick(ohgp_ref, ohgs_ref, n_prompt_tiles).astype(BF16)
    mix = _dot(att, w_ref[0:ATT_WIDTH, :]) + _dot(ohg, w_ref[ATT_WIDTH:ATT_WIDTH + HG_WIDTH, :])
    o_ref[...] = h_ref[...] + _rms(mix, g_ref[...])


def _mix(att_p, att_s, ohg_p, ohg_s, h, g, w_out, *, tm):
    n, d = h.shape
    ntp = att_p.shape[0] // tm
    nts = att_s.shape[0] // tm
    assert att_p.shape[0] % tm == 0 and att_s.shape[0] % tm == 0 and ntp + nts == n // tm
    p_map, s_map = _split_maps(ntp)
    vmem = 2 * (4 * tm * 512 * 4 + 2 * tm * d * 4 + (ATT_WIDTH + HG_WIDTH) * d * 2) + 3 * tm * d * 4 + (4 << 20)
    return pl.pallas_call(
        functools.partial(_mix_body, n_prompt_tiles=ntp),
        grid=(n // tm,),
        in_specs=[pl.BlockSpec((tm, ATT_WIDTH), p_map),
                  pl.BlockSpec((tm, ATT_WIDTH), s_map),
                  pl.BlockSpec((tm, HG_WIDTH), p_map),
                  pl.BlockSpec((tm, HG_WIDTH), s_map),
                  pl.BlockSpec((tm, d), lambda i: (i, 0)),
                  pl.BlockSpec((1, d), lambda i: (0, 0)),
                  pl.BlockSpec((ATT_WIDTH + HG_WIDTH, d), lambda i: (0, 0))],
        out_specs=pl.BlockSpec((tm, d), lambda i: (i, 0)),
        out_shape=jax.ShapeDtypeStruct((n, d), F32),
        compiler_params=_cparams(("parallel",), vmem),
        name="mix",
    )(att_p, att_s, ohg_p, ohg_s, h, g, w_out)


def _ple_body(h_ref, pp_ref, ps_ref, gpre_ref, gpost_ref, wgate_ref, wproj_ref, *o_refs, n_first_tiles):
    h = h_ref[...]
    gate = jax.nn.sigmoid(_dot(_rms(h, gpre_ref[...]).astype(BF16), wgate_ref[...]))
    pe = _dot(_pick(pp_ref, ps_ref, n_first_tiles).astype(BF16), wproj_ref[...])
    y = h + _rms(pe * gate, gpost_ref[...])
    if len(o_refs) == 1:
        o_refs[0][...] = y
    else:
        is_first = pl.program_id(0) < n_first_tiles

        @pl.when(is_first)
        def _():
            o_refs[0][...] = y

        @pl.when(jnp.logical_not(is_first))
        def _():
            o_refs[1][...] = y


def _ple(h, p_first, p_rest, g_pre, g_post, w_gate, w_proj, *, tm, split_out):
    n, d = h.shape
    pd = p_first.shape[1]
    n_first = p_first.shape[0] // tm
    assert p_first.shape[0] % tm == 0 and p_rest.shape[0] % tm == 0
    first_map, rest_map = _split_maps(n_first)
    if split_out:
        out_specs = [pl.BlockSpec((tm, d), first_map), pl.BlockSpec((tm, d), rest_map)]
        out_shape = [jax.ShapeDtypeStruct((p_first.shape[0], d), F32), jax.ShapeDtypeStruct((p_rest.shape[0], d), F32)]
        sem = ("arbitrary",)
    else:
        out_specs = pl.BlockSpec((tm, d), lambda i: (i, 0))
        out_shape = jax.ShapeDtypeStruct((n, d), F32)
        sem = ("parallel",)
    vmem = 2 * (3 * tm * d * 4 + 2 * tm * pd * 4 + d * d * 2 + pd * d * 2) + 3 * tm * d * 4 + (4 << 20)
    return pl.pallas_call(
        functools.partial(_ple_body, n_first_tiles=n_first),
        grid=(n // tm,),
        in_specs=[pl.BlockSpec((tm, d), lambda i: (i, 0)),
                  pl.BlockSpec((tm, pd), first_map),
                  pl.BlockSpec((tm, pd), rest_map),
                  pl.BlockSpec((1, d), lambda i: (0, 0)),
                  pl.BlockSpec((1, d), lambda i: (0, 0)),
                  pl.BlockSpec((d, d), lambda i: (0, 0)),
                  pl.BlockSpec((pd, d), lambda i: (0, 0))],
        out_specs=out_specs,
        out_shape=out_shape,
        compiler_params=_cparams(sem, vmem),
        name="ple",
    )(h, p_first, p_rest, g_pre, g_post, w_gate, w_proj)


def _rot_cols(w, heads):
    d = w.shape[0]
    w4 = w.reshape(d, heads, 2, HEAD_DIM // 2)
    return jnp.concatenate([-w4[:, :, 1:2], w4[:, :, 0:1]], axis=2).reshape(d, heads * HEAD_DIM)


def _pack_w_in(w):
    d = w.shape[0]
    widths = (ATT_WIDTH, KV_WIDTH, KV_WIDTH, IDX_HEADS * IDX_DIM, IDX_DIM, IDX_HEADS,
              HG_HEADS * HG_KDIM, HG_HEADS * HG_KDIM, HG_WIDTH, HG_WIDTH)
    parts = []
    acc = 0
    for wd in widths:
        parts.append(w[:, acc:acc + wd])
        acc += wd
    q, k, v, iq, ik, iw, hq, hf, hi, hg = parts
    z = lambda n: jnp.zeros((d, n), w.dtype)
    ikw = jnp.concatenate([ik, iw, z(LANES - IDX_DIM - IDX_HEADS)], axis=1)
    ikr = jnp.concatenate([_rot_cols(ik, 1), z(LANES - IDX_DIM)], axis=1)
    cat = jnp.concatenate([q, k, iq, ikw, v, hq, hf, hi, hg,
                           _rot_cols(q, N_HEADS), _rot_cols(k, KV_HEADS), _rot_cols(iq, IDX_HEADS), ikr], axis=1)
    assert cat.shape[1] == _C_END
    w_t = jnp.concatenate([k, _rot_cols(k, KV_HEADS), v, ik, _rot_cols(ik, 1)], axis=1).T
    assert w_t.shape[0] == _R_END
    return cat.astype(BF16), w_t.astype(BF16)


def _rope_tables(seq, past_len, tq, tm):
    half = HEAD_DIM // 2
    inv_freq = ROPE_THETA ** (-jnp.arange(half, dtype=F32) / half)
    pos_p = jnp.arange(seq, dtype=jnp.int32)
    pos_s = past_len + (jnp.arange(tm, dtype=jnp.int32) % tq)
    pos = jnp.concatenate([pos_p, pos_s]).astype(F32)
    ang = pos[:, None] * inv_freq[None, :]
    reps = LANES // half
    cos = jnp.tile(jnp.cos(ang), (1, reps)).reshape(seq // tm + 1, tm, LANES)
    sin = jnp.tile(jnp.sin(ang), (1, reps)).reshape(seq // tm + 1, tm, LANES)
    nt = seq // tm
    cos_t = jnp.swapaxes(cos[:nt], 1, 2)
    sin_t = jnp.swapaxes(sin[:nt], 1, 2)
    return cos, sin, cos_t, sin_t


def _lower_bounds(logits):
    sm = jax.nn.softmax(logits.astype(F32), axis=0)
    return jnp.cumsum(sm, axis=0) - sm[0:1]


def _feature_major(cache, width):
    depth, pool, page = cache.shape[:3]
    c = cache.reshape(depth, pool, page, width)
    return jnp.swapaxes(c, 2, 3)


TM_DENSE = 1024
TM_PROJ = 512
TM_FFN = 1024
TF_FFN = 256
Q_BLOCK = 128
N_KEY_LEN = 8
DSA_SEQS = 4
HG_ROWS = 512
HG_SEQS = 8


def kernel(x_prompt, x_sample, p_prompt, p_sample, cache_k, cache_v, cache_idx_k, state_hgrn, page_table,
           n_f1_pre, n_f1_post, w_f1_gate, w_f1_up, w_f1_down,
           n_mix_pre, n_mix_post, w_in, hg_lb_logits, hg_norm, w_out,
           n_f2_pre, n_f2_post, w_f2_gate, w_f2_up, w_f2_down,
           n_ple_pre, n_ple_post, w_ple_proj, w_ple_gate):
    bp, tp, d = x_prompt.shape
    bs, ts, _ = x_sample.shape
    depth = w_in.shape[0]
    n_p, n_s = bp * tp, bs * ts
    page = cache_k.shape[2]
    past_len = page_table.shape[1] * page

    hs = (x_prompt.reshape(n_p, d), x_sample.reshape(n_s, d))
    ckt = _feature_major(cache_k, KV_WIDTH)
    cvt = _feature_major(cache_v, KV_WIDTH)
    cikt = _feature_major(cache_idx_k, IDX_DIM)
    cos_tab, sin_tab, cos_t, sin_t = _rope_tables(tp, past_len, ts, TM_PROJ)
    lbs = _lower_bounds(hg_lb_logits)
    row = lambda a: a.reshape(1, -1)
    bf = lambda a: a.astype(BF16)

    outs = {name: [] for name in ("kp", "vp", "ikp", "sp", "ks", "vs", "iks", "ss")}
    for i in range(depth):
        h1 = _ffn(hs, row(n_f1_pre[i]), row(n_f1_post[i]), bf(w_f1_gate[i]), bf(w_f1_up[i]), bf(w_f1_down[i]),
                  tm=TM_FFN, tf=TF_FFN)
        w_cat, w_t = _pack_w_in(w_in[i])
        g_mix = row(n_mix_pre[i])
        q16, k, v, iq16, ikw, hq, hf, hi, hg, ikw16 = _proj(
            h1, g_mix, w_cat, cos_tab, sin_tab,
            tm=TM_PROJ, tiles_per_seq=tp // TM_PROJ, n_prompt_tiles=n_p // TM_PROJ)
        kt, vt, ikt, kt16, vt16 = _proj_t(h1, g_mix, w_t, cos_t, sin_t, batch=bp, seq=tp, tm=TM_PROJ)

        att_p = _dsa_prompt(q16, iq16, ikw, ikw16, kt16, vt16, batch=bp, seq=tp, qb=min(Q_BLOCK, tp),
                            n_len=N_KEY_LEN)
        att_s = _dsa_sample(page_table, q16, iq16, ikw, k, v, ckt, cvt, cikt,
                            layer=i, row0=n_p, batch=bs, tq=ts, nb=DSA_SEQS)
        lb = row(lbs[i])
        gn = row(hg_norm[i])
        ohg_p, s_p = _hgrn_prompt(hq, hf, hi, hg, lb, gn, batch=bp, seq=tp, rows_per_step=HG_ROWS)
        ohg_s, s_s = _hgrn_sample(hq, hf, hi, hg, lb, gn, state_hgrn,
                                  layer=i, row0=n_p, batch=bs, tq=ts, nb=HG_SEQS)

        h2 = _mix(att_p, att_s, ohg_p, ohg_s, h1, row(n_mix_post[i]), bf(w_out[i]), tm=TM_DENSE)
        h3 = _ffn((h2,), row(n_f2_pre[i]), row(n_f2_post[i]), bf(w_f2_gate[i]), bf(w_f2_up[i]), bf(w_f2_down[i]),
                  tm=TM_FFN, tf=TF_FFN)
        last = i == depth - 1
        h = _ple(h3, p_prompt[i].reshape(n_p, -1), p_sample[i].reshape(n_s, -1),
                 row(n_ple_pre[i]), row(n_ple_post[i]), bf(w_ple_gate[i]), bf(w_ple_proj[i]),
                 tm=TM_DENSE, split_out=last)
        hs = h if last else (h,)

        outs["kp"].append(jnp.transpose(kt.reshape(bp, KV_HEADS, HEAD_DIM, tp), (0, 3, 1, 2)))
        outs["vp"].append(jnp.transpose(vt.reshape(bp, KV_HEADS, HEAD_DIM, tp), (0, 3, 1, 2)))
        outs["ikp"].append(jnp.swapaxes(ikt, 1, 2))
        outs["sp"].append(s_p)
        outs["ks"].append(k[n_p:].reshape(bs, ts, KV_HEADS, HEAD_DIM))
        outs["vs"].append(v[n_p:].reshape(bs, ts, KV_HEADS, HEAD_DIM))
        outs["iks"].append(ikw[n_p:, :IDX_DIM].reshape(bs, ts, IDX_DIM))
        outs["ss"].append(s_s.astype(state_hgrn.dtype))

    st = lambda name: jnp.stack(outs[name])
    return (hs[0].reshape(bp, tp, d), hs[1].reshape(bs, ts, d),
            st("kp"), st("vp"), st("ikp"), st("sp"), st("ks"), st("vs"), st("iks"), st("ss"))
```

```python
import functools

import jax
import jax.numpy as jnp
from jax import lax
from jax.experimental import pallas as pl
from jax.experimental.pallas import tpu as pltpu

HEAD_DIM = 64
N_HEADS = 8
KV_HEADS = 2
IDX_HEADS = 8
IDX_DIM = 64
TOPK_MAX = 256
HG_KDIM = 128
HG_VDIM = 128
HG_HEADS = 4
HG_CHUNK = 64
ROPE_THETA = 10000.0
EPS = 1e-6
ATT_WIDTH = N_HEADS * HEAD_DIM
KV_WIDTH = KV_HEADS * HEAD_DIM
HG_WIDTH = HG_HEADS * HG_VDIM
QK_SCALE = HEAD_DIM ** -0.5
IDX_SCALE = IDX_DIM ** -0.5

LANES = 128
SUBLANES = 8
VMEM_BUDGET_BYTES = 56 * 1024 * 1024

NEG_BIG = -1e30

F32 = jnp.float32
BF16 = jnp.bfloat16


def _cparams(sem, vmem_bytes):
    return pltpu.CompilerParams(dimension_semantics=sem,
                                vmem_limit_bytes=int(min(max(vmem_bytes, 16 << 20), VMEM_BUDGET_BYTES)))


def _rms(x, g):
    return x * lax.rsqrt(jnp.mean(x * x, axis=-1, keepdims=True) + EPS) * g


def _silu(x):
    return x * jax.nn.sigmoid(x)


def _dot(a, b):
    return jnp.dot(a, b, preferred_element_type=F32)


def _dot_nt(a, b):
    return lax.dot_general(a, b, (((1,), (1,)), ((), ())), preferred_element_type=F32)


def _dot_tn(a, b):
    return lax.dot_general(a, b, (((0,), (0,)), ((), ())), preferred_element_type=F32)


def _split_maps(n_first_tiles):
    return (lambda i, *_: (jnp.minimum(i, n_first_tiles - 1), 0),
            lambda i, *_: (jnp.maximum(i - n_first_tiles, 0), 0))


def _pick(first_ref, rest_ref, n_first_tiles):
    return jnp.where(pl.program_id(0) < n_first_tiles, first_ref[...], rest_ref[...])


def _mix_rows(att, ohg, h, g, w_ref):
    mix = _dot(att.astype(BF16), w_ref[0:ATT_WIDTH, :]) + \
        _dot(ohg.astype(BF16), w_ref[ATT_WIDTH:ATT_WIDTH + HG_WIDTH, :])
    return h + _rms(mix, g)


def _ple_rows(h, p, g_pre, g_post, w_gate, w_proj):
    gate = jax.nn.sigmoid(_dot(_rms(h, g_pre).astype(BF16), w_gate))
    return h + _rms(_dot(p.astype(BF16), w_proj) * gate, g_post)


_FFN_STAGE_INPUTS = {"x": 1, "x2": 2, "mix": 5, "ple": 7}
FFN_STAGE_ROWS = 256


def _ffn_body(*refs, stage, n_first_tiles):
    n_in = _FFN_STAGE_INPUTS[stage]
    ins = refs[:n_in]
    gpre_ref, gpost_ref, wg_ref, wu_ref, wd_ref, o_ref, x_ref, xn_ref, acc_ref = refs[n_in:]
    j = pl.program_id(1)
    is_first = pl.program_id(0) < n_first_tiles

    def stage_rows(r):
        if stage == "x":
            return ins[0][r, :]
        if stage == "x2":
            return jnp.where(is_first, ins[0][r, :], ins[1][r, :])
        if stage == "mix":
            att, ohg, h, g, w = ins
            return _mix_rows(att[r, :], ohg[r, :], h[r, :], g[...], w)
        h, pp, ps, g1, g2, wgate, wproj = ins
        return _ple_rows(h[r, :], jnp.where(is_first, pp[r, :], ps[r, :]), g1[...], g2[...], wgate[...], wproj[...])

    @pl.when(j == 0)
    def _():
        tm = x_ref.shape[0]
        rc = min(FFN_STAGE_ROWS, tm)
        for r0 in range(0, tm, rc):
            r = slice(r0, r0 + rc)
            x = stage_rows(r)
            x_ref[r, :] = x
            xn_ref[r, :] = _rms(x, gpre_ref[...]).astype(BF16)
        acc_ref[...] = jnp.zeros_like(acc_ref)

    xn = xn_ref[...]
    a = _dot(xn, wg_ref[...])
    b = _dot(xn, wu_ref[...])
    acc_ref[...] += _dot((_silu(a) * b).astype(BF16), wd_ref[...])

    @pl.when(j == pl.num_programs(1) - 1)
    def _():
        o_ref[...] = x_ref[...] + 0.5 * _rms(acc_ref[...], gpost_ref[...])


def _ffn(stage, ins, g_pre, g_post, wg, wu, wd, *, n, tm, tf, n_first):
    d = wg.shape[0]
    dff = wg.shape[1]
    assert n % tm == 0 and n_first % tm == 0 and dff % tf == 0 and len(ins) == _FFN_STAGE_INPUTS[stage]
    first_map, rest_map = _split_maps(n_first // tm)
    tok = lambda w: pl.BlockSpec((tm, w), lambda i, j: (i, 0))
    whole = lambda a: pl.BlockSpec(a.shape, lambda i, j: (0,) * a.ndim)
    if stage == "x":
        in_specs = [tok(d)]
    elif stage == "x2":
        in_specs = [pl.BlockSpec((tm, d), first_map), pl.BlockSpec((tm, d), rest_map)]
    elif stage == "mix":
        in_specs = [tok(ATT_WIDTH), tok(HG_WIDTH), tok(d), whole(ins[3]), whole(ins[4])]
    else:
        pd = ins[1].shape[1]
        in_specs = [tok(d), pl.BlockSpec((tm, pd), first_map), pl.BlockSpec((tm, pd), rest_map),
                    whole(ins[3]), whole(ins[4]), whole(ins[5]), whole(ins[6])]
    tile_bytes = sum(s.block_shape[0] * s.block_shape[1] * a.dtype.itemsize for s, a in zip(in_specs, ins))
    vmem = 2 * (tile_bytes + tm * d * 4 + 3 * d * tf * 2) + tm * d * 10 + 4 * tm * tf * 4 \
        + 6 * FFN_STAGE_ROWS * d * 4 + (4 << 20)
    return pl.pallas_call(
        functools.partial(_ffn_body, stage=stage, n_first_tiles=n_first // tm),
        grid=(n // tm, dff // tf),
        in_specs=in_specs + [
            pl.BlockSpec((1, d), lambda i, j: (0, 0)),
            pl.BlockSpec((1, d), lambda i, j: (0, 0)),
            pl.BlockSpec((d, tf), lambda i, j: (0, j)),
            pl.BlockSpec((d, tf), lambda i, j: (0, j)),
            pl.BlockSpec((tf, d), lambda i, j: (j, 0)),
        ],
        out_specs=pl.BlockSpec((tm, d), lambda i, j: (i, 0)),
        out_shape=jax.ShapeDtypeStruct((n, d), F32),
        scratch_shapes=[pltpu.VMEM((tm, d), F32), pltpu.VMEM((tm, d), BF16), pltpu.VMEM((tm, d), F32)],
        compiler_params=_cparams(("parallel", "arbitrary"), vmem),
        name="ffn_" + stage,
    )(*ins, g_pre, g_post, wg, wu, wd)


_C_Q, _C_K, _C_IQ, _C_IKW, _C_V, _C_HQ, _C_HF, _C_HI, _C_HG = 0, 512, 640, 1152, 1280, 1408, 1920, 2432, 2944
_C_QR, _C_KR, _C_IQR, _C_IKR, _C_END = 3456, 3968, 4096, 4608, 4736


def _proj_body(h_ref, g_ref, w_ref, cos_ref, sin_ref,
               q_ref, k_ref, v_ref, iq_ref, ikw_ref, hq_ref, hf_ref, hi_ref, hg_ref, ikw16_ref):
    u = _rms(h_ref[...], g_ref[...]).astype(BF16)

    def mm(lo, hi):
        return _dot(u, w_ref[:, lo:hi])

    cos = cos_ref[...]
    sin = sin_ref[...]
    cos4 = jnp.concatenate([cos] * 4, axis=1)
    sin4 = jnp.concatenate([sin] * 4, axis=1)
    q_ref[...] = ((mm(_C_Q, _C_K) * cos4 + mm(_C_QR, _C_KR) * sin4) * QK_SCALE).astype(BF16)
    k_ref[...] = mm(_C_K, _C_IQ) * cos + mm(_C_KR, _C_IQR) * sin
    iq_ref[...] = ((mm(_C_IQ, _C_IKW) * cos4 + mm(_C_IQR, _C_IKR) * sin4) * IDX_SCALE).astype(BF16)
    lane = lax.broadcasted_iota(jnp.int32, cos.shape, 1)
    cos_ikw = jnp.where(lane < IDX_DIM, cos, IDX_HEADS ** -0.5)
    ikw = mm(_C_IKW, _C_V) * cos_ikw + mm(_C_IKR, _C_END) * sin
    ikw_ref[...] = ikw
    ikw16_ref[...] = ikw.astype(BF16)
    v_ref[...] = mm(_C_V, _C_HQ)
    hq_ref[...] = mm(_C_HQ, _C_HF)
    hf_ref[...] = mm(_C_HF, _C_HI)
    hi_ref[...] = mm(_C_HI, _C_HG)
    hg_ref[...] = mm(_C_HG, _C_QR)


def _proj(h, g, w_cat, cos_tab, sin_tab, *, tm, tiles_per_seq, n_prompt_tiles):
    n, d = h.shape
    outs = ((512, BF16), (128, F32), (128, F32), (512, BF16), (128, F32),
            (512, F32), (512, F32), (512, F32), (512, F32), (128, BF16))

    def tab_map(i):
        return (jnp.where(i < n_prompt_tiles, i % tiles_per_seq, tiles_per_seq), 0, 0)

    out_bytes = sum(w * jnp.dtype(t).itemsize for w, t in outs)
    vmem = 2 * (tm * d * 4 + d * _C_END * 2 + tm * out_bytes + 2 * tm * LANES * 4) + 6 * tm * 512 * 4 + (4 << 20)
    return pl.pallas_call(
        _proj_body,
        grid=(n // tm,),
        in_specs=[
            pl.BlockSpec((tm, d), lambda i: (i, 0)),
            pl.BlockSpec((1, d), lambda i: (0, 0)),
            pl.BlockSpec((d, _C_END), lambda i: (0, 0)),
            pl.BlockSpec((None, tm, LANES), tab_map),
            pl.BlockSpec((None, tm, LANES), tab_map),
        ],
        out_specs=[pl.BlockSpec((tm, w), lambda i: (i, 0)) for w, _ in outs],
        out_shape=[jax.ShapeDtypeStruct((n, w), t) for w, t in outs],
        compiler_params=_cparams(("parallel",), vmem),
        name="proj",
    )(h, g, w_cat, cos_tab, sin_tab)


_R_K, _R_KR, _R_V, _R_IK, _R_IKR, _R_END = 0, 128, 256, 384, 448, 512


def _proj_t_body(h_ref, g_ref, w_ref, cos_ref, sin_ref, k_ref, v_ref, ik_ref, k16_ref, v16_ref):
    u = _rms(h_ref[...], g_ref[...]).astype(BF16)
    r = _dot_nt(w_ref[...], u)
    cos = cos_ref[...]
    sin = sin_ref[...]
    k = r[_R_K:_R_KR] * cos + r[_R_KR:_R_V] * sin
    v = r[_R_V:_R_IK]
    ik = r[_R_IK:_R_IKR] * cos[0:IDX_DIM] + r[_R_IKR:_R_END] * sin[0:IDX_DIM]
    k_ref[...] = k
    v_ref[...] = v
    ik_ref[...] = ik
    k16_ref[...] = k.astype(BF16)
    v16_ref[...] = v.astype(BF16)


def _proj_t(h, g, w_t, cos_t, sin_t, *, batch, seq, tm):
    d = h.shape[1]
    nt = seq // tm
    outs = ((KV_WIDTH, F32), (KV_WIDTH, F32), (IDX_DIM, F32), (KV_WIDTH, BF16), (KV_WIDTH, BF16))
    vmem = 2 * (tm * d * 4 + _R_END * d * 2 + 2 * LANES * tm * 4 + 3 * LANES * tm * 6) + 4 * _R_END * tm * 4 + (4 << 20)
    return pl.pallas_call(
        _proj_t_body,
        grid=(batch, nt),
        in_specs=[
            pl.BlockSpec((tm, d), lambda b, t: (b * nt + t, 0)),
            pl.BlockSpec((1, d), lambda b, t: (0, 0)),
            pl.BlockSpec((_R_END, d), lambda b, t: (0, 0)),
            pl.BlockSpec((None, LANES, tm), lambda b, t: (t, 0, 0)),
            pl.BlockSpec((None, LANES, tm), lambda b, t: (t, 0, 0)),
        ],
        out_specs=[pl.BlockSpec((None, w, tm), lambda b, t: (b, 0, t)) for w, _ in outs],
        out_shape=[jax.ShapeDtypeStruct((batch, w, seq), t) for w, t in outs],
        compiler_params=_cparams(("parallel", "parallel"), vmem),
        name="proj_t",
    )(h, g, w_t, cos_t, sin_t)


def _tri_incl(n):
    a = lax.broadcasted_iota(jnp.int32, (n, n), 0)
    b = lax.broadcasted_iota(jnp.int32, (n, n), 1)
    return a, b


I16_MIN = -32768
I16_ROWS = 16


def _tree_sum(xs):
    xs = list(xs)
    while len(xs) > 1:
        nxt = [xs[i] + xs[i + 1] for i in range(0, len(xs) - 1, 2)]
        if len(xs) % 2:
            nxt.append(xs[-1])
        xs = nxt
    return xs[0]


def _count_cols(mask, dtype):
    rows = I16_ROWS if dtype == jnp.int16 else SUBLANES
    one = jnp.where(mask, jnp.ones((), dtype), jnp.zeros((), dtype))
    part = _tree_sum([one[c * rows:(c + 1) * rows] for c in range(mask.shape[0] // rows)])
    return jnp.sum(part.astype(F32), axis=0, keepdims=True)


def _search16(x16, k):
    def body(i, t):
        cand = t + lax.shift_left(jnp.int32(1), 15 - i)
        cnt = _count_cols(x16 >= cand.astype(jnp.int16), jnp.int16)
        return jnp.where(cnt >= k, cand, t)

    return lax.fori_loop(0, 16, body, jnp.full(k.shape, I16_MIN, jnp.int32))


def _topk_mask_t(score, adm, k):
    l, n = score.shape
    score = jnp.where(score == 0.0, 0.0, score)
    score = jnp.where(adm, score, -jnp.inf)
    bits = lax.bitcast_convert_type(score, jnp.int32)
    key = bits ^ ((bits >> 31) & 0x7FFFFFFF)
    kf = jnp.full((1, n), float(k), F32)
    hi = (key >> 16).astype(jnp.int16)
    t_hi = _search16(hi, kf)
    t_hi16 = t_hi.astype(jnp.int16)
    above = _count_cols(hi > t_hi16, jnp.int16)
    lo = ((key & 0xFFFF) + I16_MIN).astype(jnp.int16)
    lo = jnp.where(hi == t_hi16, lo, jnp.int16(I16_MIN))
    t_lo = _search16(lo, kf - above)
    t = lax.shift_left(t_hi, 16) | (t_lo - I16_MIN)
    gt = key > t
    eq = key == t
    need = kf - _count_cols(gt, jnp.float32)
    a, b = _tri_incl(LANES)
    tril = jnp.where(b <= a, 1.0, 0.0).astype(BF16)
    carry = jnp.zeros((1, n), F32)
    sel = []
    for c in range(l // LANES):
        rows = slice(c * LANES, (c + 1) * LANES)
        pref = _dot(tril, jnp.where(eq[rows], 1.0, 0.0).astype(BF16)) + carry
        carry = pref[LANES - 1:LANES]
        take = jnp.logical_or(gt[rows], jnp.logical_and(eq[rows], pref <= need))
        sel.append(jnp.where(jnp.logical_and(take, adm[rows]), 1.0, 0.0))
    return jnp.concatenate(sel, axis=0) if len(sel) > 1 else sel[0]


def _dsa_prompt_tile(q_ref, iq_ref, iw_ref, kt_ref, vt_ref, ik_ref, o_ref, *, q0, l, topk):
    qb = q_ref.shape[0]
    iq_t = iq_ref[...].astype(F32).T.astype(BF16)
    w_t = iw_ref[...].T[IDX_DIM:IDX_DIM + IDX_HEADS]
    ik = ik_ref[0:l, 0:IDX_DIM]
    score = jnp.zeros((l, qb), F32)
    for h in range(IDX_HEADS):
        lg = _dot(ik, iq_t[h * IDX_DIM:(h + 1) * IDX_DIM])
        score = score + w_t[h:h + 1] * jnp.maximum(lg, 0.0)

    kpos = lax.broadcasted_iota(jnp.int32, (l, qb), 0)
    qpos = q0 + lax.broadcasted_iota(jnp.int32, (l, qb), 1)
    sel_t = _topk_mask_t(score, kpos <= qpos, topk)
    bias = ((sel_t - 1.0) * (-NEG_BIG)).T

    q = q_ref[...]
    group = N_HEADS // KV_HEADS
    for h in range(N_HEADS):
        g = h // group
        s = _dot(q[:, h * HEAD_DIM:(h + 1) * HEAD_DIM], kt_ref[g * HEAD_DIM:(g + 1) * HEAD_DIM, 0:l]) + bias
        p = jnp.exp(s - jnp.max(s, axis=-1, keepdims=True))
        den = jnp.sum(p, axis=-1, keepdims=True)
        o = _dot_nt(p.astype(BF16), vt_ref[g * HEAD_DIM:(g + 1) * HEAD_DIM, 0:l])
        o_ref[:, h * HEAD_DIM:(h + 1) * HEAD_DIM] = o / den


def _dsa_prompt_body(q_ref, iq_ref, iw_ref, kt_ref, vt_ref, ik_ref, o_ref, *, topk, n_len):
    qi = pl.program_id(1)
    nq = pl.num_programs(1)
    qb = q_ref.shape[0]
    seq = kt_ref.shape[1]
    per = nq // n_len
    for v in range(n_len):
        l = (v + 1) * per * qb

        @pl.when(qi // per == v)
        def _(l=l):
            _dsa_prompt_tile(q_ref, iq_ref, iw_ref, kt_ref, vt_ref, ik_ref, o_ref,
                             q0=qi * qb, l=min(l, seq), topk=topk)


def _dsa_prompt(q16, iq16, ikw, ikw16, kt16, vt16, *, batch, seq, qb, n_len):
    topk = min(TOPK_MAX, seq // 4)
    nq = seq // qb
    n_len = n_len if nq % n_len == 0 else 1
    vmem = 2 * (2 * qb * 512 * 2 + qb * LANES * 4 + qb * 512 * 4 + seq * (2 * KV_WIDTH + LANES) * 2) \
        + 16 * qb * seq * 4 + (4 << 20)
    return pl.pallas_call(
        functools.partial(_dsa_prompt_body, topk=topk, n_len=n_len),
        grid=(batch, nq),
        in_specs=[
            pl.BlockSpec((qb, ATT_WIDTH), lambda bi, qi: (bi * nq + qi, 0)),
            pl.BlockSpec((qb, IDX_HEADS * IDX_DIM), lambda bi, qi: (bi * nq + qi, 0)),
            pl.BlockSpec((qb, LANES), lambda bi, qi: (bi * nq + qi, 0)),
            pl.BlockSpec((None, KV_WIDTH, seq), lambda bi, qi: (bi, 0, 0)),
            pl.BlockSpec((None, KV_WIDTH, seq), lambda bi, qi: (bi, 0, 0)),
            pl.BlockSpec((seq, LANES), lambda bi, qi: (bi, 0)),
        ],
        out_specs=pl.BlockSpec((qb, ATT_WIDTH), lambda bi, qi: (bi * nq + qi, 0)),
        out_shape=jax.ShapeDtypeStruct((q16.shape[0], ATT_WIDTH), F32),
        compiler_params=_cparams(("parallel", "parallel"), vmem),
        name="dsa_prompt",
    )(q16, iq16, ikw, kt16, vt16, ikw16)


def _dsa_sample_body(pt_ref, q_ref, iq_ref, ikw_ref, kn_ref, vn_ref, *rest, n_pages, page, tq, nb, topk):
    del pt_ref
    npg = nb * n_pages
    kp_refs, vp_refs, ip_refs = rest[0:npg], rest[npg:2 * npg], rest[2 * npg:3 * npg]
    o_ref = rest[3 * npg + 1]
    past = n_pages * page
    l = past + LANES
    group = N_HEADS // KV_HEADS

    def pad_rows(x):
        return jnp.concatenate([x, jnp.zeros((LANES - tq, x.shape[1]), x.dtype)], axis=0)

    def pages(refs, bj):
        return jnp.concatenate([refs[bj * n_pages + p][...] for p in range(n_pages)], axis=1).astype(BF16)

    iq_all = iq_ref[...].astype(F32)
    q_all = q_ref[...].astype(F32)
    ikw_all = ikw_ref[...]

    scores = []
    for bj in range(nb):
        rows = slice(bj * tq, (bj + 1) * tq)
        iq = iq_all[rows]
        iq_hm = jnp.concatenate([iq[:, h * IDX_DIM:(h + 1) * IDX_DIM] for h in range(IDX_HEADS)],
                                axis=0).astype(BF16)
        ik_new = pad_rows(ikw_all[rows, 0:IDX_DIM]).astype(BF16)
        lg = jnp.concatenate([_dot(iq_hm, pages(ip_refs, bj)), _dot_nt(iq_hm, ik_new)], axis=1)
        lg = jnp.maximum(lg, 0.0)
        w = ikw_all[rows, IDX_DIM:IDX_DIM + IDX_HEADS]
        sc = jnp.zeros((tq, l), F32)
        for h in range(IDX_HEADS):
            sc = sc + w[:, h:h + 1] * lg[h * tq:(h + 1) * tq, :]
        scores.append(sc)
    nq = nb * tq
    score_t = jnp.concatenate(scores + [jnp.zeros((LANES - nq, l), F32)], axis=0).T
    kidx = lax.broadcasted_iota(jnp.int32, (l, LANES), 0)
    qidx = lax.broadcasted_iota(jnp.int32, (l, LANES), 1)
    adm = kidx <= past + qidx % tq
    adm = jnp.logical_and(adm, kidx < past + tq)
    adm = jnp.logical_and(adm, qidx < nq)
    sel_t = _topk_mask_t(score_t, adm, topk)
    bias = ((sel_t - 1.0) * (-NEG_BIG)).T

    for bj in range(nb):
        rows = slice(bj * tq, (bj + 1) * tq)
        q = q_all[rows]
        bias_g = jnp.concatenate([bias[rows]] * group, axis=0)
        kt = pages(kp_refs, bj)
        vt = pages(vp_refs, bj)
        k_new = pad_rows(kn_ref[rows, :]).astype(BF16)
        v_new = pad_rows(vn_ref[rows, :]).astype(BF16)
        for g in range(KV_HEADS):
            fs = slice(g * HEAD_DIM, (g + 1) * HEAD_DIM)
            qg = jnp.concatenate([q[:, h * HEAD_DIM:(h + 1) * HEAD_DIM]
                                  for h in range(g * group, (g + 1) * group)], axis=0).astype(BF16)
            s = jnp.concatenate([_dot(qg, kt[fs]), _dot_nt(qg, k_new[:, fs])], axis=1) + bias_g
            p = jnp.exp(s - jnp.max(s, axis=-1, keepdims=True))
            den = jnp.sum(p, axis=-1, keepdims=True)
            p = p.astype(BF16)
            o = (_dot_nt(p[:, 0:past], vt[fs]) + _dot(p[:, past:l], v_new[:, fs])) / den
            for j in range(group):
                h = g * group + j
                o_ref[rows, h * HEAD_DIM:(h + 1) * HEAD_DIM] = o[j * tq:(j + 1) * tq, :]


def _dsa_sample(page_table, q16, iq16, ikw, k, v, cache_kt, cache_vt, cache_ikt, att, *, layer, row0, batch, tq, nb):
    n_pages = page_table.shape[1]
    page = cache_kt.shape[3]
    past = n_pages * page
    topk = min(TOPK_MAX, (past + tq) // 4)
    blk0 = row0 // (nb * tq)
    pt = page_table.reshape(-1)

    def tok(w):
        return pl.BlockSpec((nb * tq, w), lambda gi, pt_ref: (blk0 + gi, 0))

    def page_spec(w, bj, p):
        return pl.BlockSpec((None, None, w, page),
                            lambda gi, pt_ref: (layer, pt_ref[(gi * nb + bj) * n_pages + p], 0, 0))

    in_specs = [tok(ATT_WIDTH), tok(IDX_HEADS * IDX_DIM), tok(LANES), tok(KV_WIDTH), tok(KV_WIDTH)]
    for w in (KV_WIDTH, KV_WIDTH, IDX_DIM):
        in_specs += [page_spec(w, bj, p) for bj in range(nb) for p in range(n_pages)]
    vmem = 2 * nb * n_pages * page * (2 * KV_WIDTH + IDX_DIM) * 4 + 16 * nb * (past + LANES) * LANES * 4 + (8 << 20)
    npg = nb * n_pages
    return pl.pallas_call(
        functools.partial(_dsa_sample_body, n_pages=n_pages, page=page, tq=tq, nb=nb, topk=topk),
        grid_spec=pltpu.PrefetchScalarGridSpec(
            num_scalar_prefetch=1,
            grid=(batch // nb,),
            in_specs=in_specs + [pl.BlockSpec(memory_space=pl.ANY)],
            out_specs=tok(ATT_WIDTH),
        ),
        out_shape=jax.ShapeDtypeStruct(att.shape, att.dtype),
        input_output_aliases={len(in_specs) + 1: 0},
        compiler_params=_cparams(("parallel",), vmem),
        name="dsa_sample",
    )(pt, q16, iq16, ikw, k, v, *([cache_kt] * npg), *([cache_vt] * npg), *([cache_ikt] * npg), att)


def _hgrn_levels(c):
    out = []
    n = c // 2
    while n >= 1:
        out.append(n)
        n //= 2
    return out


def _group_row(x, j, g):
    c, n = x.shape
    x3 = x.reshape(c // g, g, n)
    return jnp.broadcast_to(x3[:, j:j + 1, :], (c // g, g, n)).reshape(c, n)


def _boundary_rows(b, n, row):
    g = 2 * n
    if g >= SUBLANES:
        return _group_row(b, n - 1, g)
    out = _group_row(b, n - 1, SUBLANES)
    for j in range(1, SUBLANES // g):
        out = jnp.where((row % SUBLANES) // g == j, _group_row(b, j * g + n - 1, SUBLANES), out)
    return out


def _hgrn_chunk(xq, xf, xi, lb, state_t):
    c, dk = xq.shape
    f = lb + (1.0 - lb) * jax.nn.sigmoid(xf)
    logf = jnp.log(f)
    kk = 1.0 - f
    qf = _silu(xq) * (HG_KDIM ** -0.5)
    ta, tb = _tri_incl(c)
    row = lax.broadcasted_iota(jnp.int32, (c, 1), 0)
    b = logf
    step = 1
    while step < c:
        b = b + jnp.where(row >= step, pltpu.roll(b, step, 0), 0.0)
        step *= 2

    o = _dot_nt((qf * jnp.exp(b)).astype(BF16), state_t.astype(BF16))
    o = o + jnp.sum(qf * kk, axis=-1, keepdims=True) * xi
    xi16 = xi.astype(BF16)
    att = jnp.zeros((c, c), F32)
    for n in _hgrn_levels(c):
        upper = (row % (2 * n)) >= n
        e = jnp.exp(-jnp.abs(b - _boundary_rows(b, n, row)))
        m = (jnp.where(upper, qf, kk) * e).astype(BF16)
        pair = jnp.logical_and(ta // (2 * n) == tb // (2 * n),
                               jnp.logical_and((ta % (2 * n)) >= n, (tb % (2 * n)) < n))
        att = att + jnp.where(pair, _dot_nt(m, m), 0.0)
    o = o + _dot(att.astype(BF16), xi16)

    blast = b[c - 1:c]
    kd = (kk * jnp.exp(blast - b)).astype(BF16)
    new_state_t = state_t * jnp.exp(blast) + _dot_tn(xi16, kd)
    return o, new_state_t


def _hgrn_gate(o, gnorm, xg):
    return _rms(o, gnorm) * _silu(xg)


def _hgrn_prompt_body(hq_ref, hf_ref, hi_ref, hg_ref, lb_ref, gn_ref, o_ref, s_ref, st_ref, *, chunk):
    gi = pl.program_id(1)

    @pl.when(gi == 0)
    def _():
        st_ref[...] = jnp.zeros_like(st_ref)

    n_chunks = hq_ref.shape[0] // chunk
    gn = gn_ref[...]

    def step(ci, carry):
        rows = pl.ds(pl.multiple_of(ci * chunk, chunk), chunk)
        for h in range(HG_HEADS):
            cols = slice(h * HG_KDIM, (h + 1) * HG_KDIM)
            o, st = _hgrn_chunk(hq_ref[rows, cols], hf_ref[rows, cols], hi_ref[rows, cols],
                                lb_ref[:, cols], st_ref[h])
            st_ref[h] = st
            o_ref[rows, cols] = _hgrn_gate(o, gn, hg_ref[rows, cols])
        return carry

    lax.fori_loop(0, n_chunks, step, 0)

    @pl.when(gi == pl.num_programs(1) - 1)
    def _():
        for h in range(HG_HEADS):
            s_ref[h] = st_ref[h].T


def _hgrn_prompt(hq, hf, hi, hg, lb, gnorm, *, batch, seq, rows_per_step):
    ng = seq // rows_per_step
    chunk = min(HG_CHUNK, seq)

    def tok():
        return pl.BlockSpec((rows_per_step, HG_WIDTH), lambda bi, gi: (bi * ng + gi, 0))

    vmem = 2 * 5 * rows_per_step * HG_WIDTH * 4 + 3 * HG_HEADS * HG_KDIM * HG_VDIM * 4 + (8 << 20)
    return pl.pallas_call(
        functools.partial(_hgrn_prompt_body, chunk=chunk),
        grid=(batch, ng),
        in_specs=[tok(), tok(), tok(), tok(),
                  pl.BlockSpec((1, HG_WIDTH), lambda bi, gi: (0, 0)),
                  pl.BlockSpec((1, HG_VDIM), lambda bi, gi: (0, 0))],
        out_specs=[tok(), pl.BlockSpec((None, HG_HEADS, HG_KDIM, HG_VDIM), lambda bi, gi: (bi, 0, 0, 0))],
        out_shape=[jax.ShapeDtypeStruct((hq.shape[0], HG_WIDTH), F32),
                   jax.ShapeDtypeStruct((batch, HG_HEADS, HG_KDIM, HG_VDIM), F32)],
        scratch_shapes=[pltpu.VMEM((HG_HEADS, HG_KDIM, HG_VDIM), F32)],
        compiler_params=_cparams(("parallel", "arbitrary"), vmem),
        name="hgrn_prompt",
    )(hq, hf, hi, hg, lb, gnorm)


def _hgrn_sample_body(hq_ref, hf_ref, hi_ref, hg_ref, lb_ref, gn_ref, s0_ref, base_ref, o_ref, s_ref, *, tq, nb):
    del base_ref
    gn = gn_ref[...]

    def step(bi, carry):
        rows = pl.ds(pl.multiple_of(bi * tq, tq), tq)
        for h in range(HG_HEADS):
            cols = slice(h * HG_KDIM, (h + 1) * HG_KDIM)
            o, st = _hgrn_chunk(hq_ref[rows, cols], hf_ref[rows, cols], hi_ref[rows, cols],
                                lb_ref[:, cols], s0_ref[bi, h].T)
            s_ref[bi, h] = st.T
            o_ref[rows, cols] = _hgrn_gate(o, gn, hg_ref[rows, cols])
        return carry

    lax.fori_loop(0, nb, step, 0, unroll=2)


def _hgrn_sample(hq, hf, hi, hg, lb, gnorm, state, ohg, *, layer, row0, batch, tq, nb):
    assert tq <= HG_CHUNK
    blk0 = row0 // (nb * tq)

    def tok():
        return pl.BlockSpec((nb * tq, HG_WIDTH), lambda gi: (blk0 + gi, 0))

    st_blk = (nb, HG_HEADS, HG_KDIM, HG_VDIM)
    vmem = 2 * (5 * nb * tq * HG_WIDTH * 4 + 2 * nb * HG_HEADS * HG_KDIM * HG_VDIM * 4) + (8 << 20)
    return pl.pallas_call(
        functools.partial(_hgrn_sample_body, tq=tq, nb=nb),
        grid=(batch // nb,),
        in_specs=[tok(), tok(), tok(), tok(),
                  pl.BlockSpec((1, HG_WIDTH), lambda gi: (0, 0)),
                  pl.BlockSpec((1, HG_VDIM), lambda gi: (0, 0)),
                  pl.BlockSpec((None,) + st_blk, lambda gi: (layer, gi, 0, 0, 0)),
                  pl.BlockSpec(memory_space=pl.ANY)],
        out_specs=[tok(), pl.BlockSpec(st_blk, lambda gi: (gi, 0, 0, 0))],
        input_output_aliases={7: 0},
        out_shape=[jax.ShapeDtypeStruct(ohg.shape, ohg.dtype),
                   jax.ShapeDtypeStruct((batch, HG_HEADS, HG_KDIM, HG_VDIM), F32)],
        compiler_params=_cparams(("parallel",), vmem),
        name="hgrn_sample",
    )(hq, hf, hi, hg, lb, gnorm, state, ohg)


def _ple_body(h_ref, pp_ref, ps_ref, gpre_ref, gpost_ref, wgate_ref, wproj_ref, yp_ref, ys_ref, *, n_first_tiles):
    y = _ple_rows(h_ref[...], _pick(pp_ref, ps_ref, n_first_tiles), gpre_ref[...], gpost_ref[...],
                  wgate_ref[...], wproj_ref[...])
    is_first = pl.program_id(0) < n_first_tiles

    @pl.when(is_first)
    def _():
        yp_ref[...] = y

    @pl.when(jnp.logical_not(is_first))
    def _():
        ys_ref[...] = y


def _ple(h, p_first, p_rest, g_pre, g_post, w_gate, w_proj, *, tm):
    n, d = h.shape
    pd = p_first.shape[1]
    n_first = p_first.shape[0] // tm
    assert p_first.shape[0] % tm == 0 and p_rest.shape[0] % tm == 0
    first_map, rest_map = _split_maps(n_first)
    vmem = 2 * (3 * tm * d * 4 + 2 * tm * pd * 4 + d * d * 2 + pd * d * 2) + 3 * tm * d * 4 + (4 << 20)
    return pl.pallas_call(
        functools.partial(_ple_body, n_first_tiles=n_first),
        grid=(n // tm,),
        in_specs=[pl.BlockSpec((tm, d), lambda i: (i, 0)),
                  pl.BlockSpec((tm, pd), first_map),
                  pl.BlockSpec((tm, pd), rest_map),
                  pl.BlockSpec((1, d), lambda i: (0, 0)),
                  pl.BlockSpec((1, d), lambda i: (0, 0)),
                  pl.BlockSpec((d, d), lambda i: (0, 0)),
                  pl.BlockSpec((pd, d), lambda i: (0, 0))],
        out_specs=[pl.BlockSpec((tm, d), first_map), pl.BlockSpec((tm, d), rest_map)],
        out_shape=[jax.ShapeDtypeStruct((p_first.shape[0], d), F32), jax.ShapeDtypeStruct((p_rest.shape[0], d), F32)],
        compiler_params=_cparams(("arbitrary",), vmem),
        name="ple",
    )(h, p_first, p_rest, g_pre, g_post, w_gate, w_proj)


def _rot_cols(w, heads):
    d = w.shape[0]
    w4 = w.reshape(d, heads, 2, HEAD_DIM // 2)
    return jnp.concatenate([-w4[:, :, 1:2], w4[:, :, 0:1]], axis=2).reshape(d, heads * HEAD_DIM)


def _pack_w_in(w):
    d = w.shape[0]
    widths = (ATT_WIDTH, KV_WIDTH, KV_WIDTH, IDX_HEADS * IDX_DIM, IDX_DIM, IDX_HEADS,
              HG_HEADS * HG_KDIM, HG_HEADS * HG_KDIM, HG_WIDTH, HG_WIDTH)
    parts = []
    acc = 0
    for wd in widths:
        parts.append(w[:, acc:acc + wd])
        acc += wd
    q, k, v, iq, ik, iw, hq, hf, hi, hg = parts
    z = lambda n: jnp.zeros((d, n), w.dtype)
    ikw = jnp.concatenate([ik, iw, z(LANES - IDX_DIM - IDX_HEADS)], axis=1)
    ikr = jnp.concatenate([_rot_cols(ik, 1), z(LANES - IDX_DIM)], axis=1)
    cat = jnp.concatenate([q, k, iq, ikw, v, hq, hf, hi, hg,
                           _rot_cols(q, N_HEADS), _rot_cols(k, KV_HEADS), _rot_cols(iq, IDX_HEADS), ikr], axis=1)
    assert cat.shape[1] == _C_END
    w_t = jnp.concatenate([k, _rot_cols(k, KV_HEADS), v, ik, _rot_cols(ik, 1)], axis=1).T
    assert w_t.shape[0] == _R_END
    return cat.astype(BF16), w_t.astype(BF16)


def _rope_tables(seq, past_len, tq, tm):
    half = HEAD_DIM // 2
    inv_freq = ROPE_THETA ** (-jnp.arange(half, dtype=F32) / half)
    pos_p = jnp.arange(seq, dtype=jnp.int32)
    pos_s = past_len + (jnp.arange(tm, dtype=jnp.int32) % tq)
    pos = jnp.concatenate([pos_p, pos_s]).astype(F32)
    ang = pos[:, None] * inv_freq[None, :]
    reps = LANES // half
    cos = jnp.tile(jnp.cos(ang), (1, reps)).reshape(seq // tm + 1, tm, LANES)
    sin = jnp.tile(jnp.sin(ang), (1, reps)).reshape(seq // tm + 1, tm, LANES)
    nt = seq // tm
    cos_t = jnp.swapaxes(cos[:nt], 1, 2)
    sin_t = jnp.swapaxes(sin[:nt], 1, 2)
    return cos, sin, cos_t, sin_t


def _lower_bounds(logits):
    sm = jax.nn.softmax(logits.astype(F32), axis=0)
    return jnp.cumsum(sm, axis=0) - sm[0:1]


def _feature_major(cache, width):
    depth, pool, page = cache.shape[:3]
    c = cache.reshape(depth, pool, page, width)
    return jnp.swapaxes(c, 2, 3)


TM_DENSE = 1024
TM_PROJ = 512
TM_FFN = 1024
TF_FFN = 256
Q_BLOCK = 128
N_KEY_LEN = 8
DSA_SEQS = 4
HG_ROWS = 512
HG_SEQS = 8


def kernel(x_prompt, x_sample, p_prompt, p_sample, cache_k, cache_v, cache_idx_k, state_hgrn, page_table,
           n_f1_pre, n_f1_post, w_f1_gate, w_f1_up, w_f1_down,
           n_mix_pre, n_mix_post, w_in, hg_lb_logits, hg_norm, w_out,
           n_f2_pre, n_f2_post, w_f2_gate, w_f2_up, w_f2_down,
           n_ple_pre, n_ple_post, w_ple_proj, w_ple_gate):
    bp, tp, d = x_prompt.shape
    bs, ts, _ = x_sample.shape
    depth = w_in.shape[0]
    n_p, n_s = bp * tp, bs * ts
    page = cache_k.shape[2]
    past_len = page_table.shape[1] * page

    ckt = _feature_major(cache_k, KV_WIDTH)
    cvt = _feature_major(cache_v, KV_WIDTH)
    cikt = _feature_major(cache_idx_k, IDX_DIM)
    cos_tab, sin_tab, cos_t, sin_t = _rope_tables(tp, past_len, ts, TM_PROJ)
    lbs = _lower_bounds(hg_lb_logits)
    row = lambda a: a.reshape(1, -1)
    bf = lambda a: a.astype(BF16)

    outs = {name: [] for name in ("kp", "vp", "ikp", "sp", "ks", "vs", "iks", "ss")}
    ffn_kw = dict(n=n_p + n_s, tm=TM_FFN, tf=TF_FFN, n_first=n_p)
    ple_args = lambda i: (p_prompt[i].reshape(n_p, -1), p_sample[i].reshape(n_s, -1),
                          row(n_ple_pre[i]), row(n_ple_post[i]), bf(w_ple_gate[i]), bf(w_ple_proj[i]))
    h3 = None
    for i in range(depth):
        stage, ins = ("x2", (x_prompt.reshape(n_p, d), x_sample.reshape(n_s, d))) if i == 0 else \
            ("ple", (h3,) + ple_args(i - 1))
        h1 = _ffn(stage, ins, row(n_f1_pre[i]), row(n_f1_post[i]), bf(w_f1_gate[i]), bf(w_f1_up[i]),
                  bf(w_f1_down[i]), **ffn_kw)
        w_cat, w_t = _pack_w_in(w_in[i])
        g_mix = row(n_mix_pre[i])
        q16, k, v, iq16, ikw, hq, hf, hi, hg, ikw16 = _proj(
            h1, g_mix, w_cat, cos_tab, sin_tab,
            tm=TM_PROJ, tiles_per_seq=tp // TM_PROJ, n_prompt_tiles=n_p // TM_PROJ)
        kt, vt, ikt, kt16, vt16 = _proj_t(h1, g_mix, w_t, cos_t, sin_t, batch=bp, seq=tp, tm=TM_PROJ)

        att = _dsa_prompt(q16, iq16, ikw, ikw16, kt16, vt16, batch=bp, seq=tp, qb=min(Q_BLOCK, tp),
                          n_len=N_KEY_LEN)
        att = _dsa_sample(page_table, q16, iq16, ikw, k, v, ckt, cvt, cikt, att,
                          layer=i, row0=n_p, batch=bs, tq=ts, nb=DSA_SEQS)
        lb = row(lbs[i])
        gn = row(hg_norm[i])
        ohg, s_p = _hgrn_prompt(hq, hf, hi, hg, lb, gn, batch=bp, seq=tp, rows_per_step=HG_ROWS)
        ohg, s_s = _hgrn_sample(hq, hf, hi, hg, lb, gn, state_hgrn, ohg,
                                layer=i, row0=n_p, batch=bs, tq=ts, nb=HG_SEQS)

        h3 = _ffn("mix", (att, ohg, h1, row(n_mix_post[i]), bf(w_out[i])),
                  row(n_f2_pre[i]), row(n_f2_post[i]), bf(w_f2_gate[i]), bf(w_f2_up[i]), bf(w_f2_down[i]), **ffn_kw)

        outs["kp"].append(jnp.transpose(kt.reshape(bp, KV_HEADS, HEAD_DIM, tp), (0, 3, 1, 2)))
        outs["vp"].append(jnp.transpose(vt.reshape(bp, KV_HEADS, HEAD_DIM, tp), (0, 3, 1, 2)))
        outs["ikp"].append(jnp.swapaxes(ikt, 1, 2))
        outs["sp"].append(s_p)
        outs["ks"].append(k[n_p:].reshape(bs, ts, KV_HEADS, HEAD_DIM))
        outs["vs"].append(v[n_p:].reshape(bs, ts, KV_HEADS, HEAD_DIM))
        outs["iks"].append(ikw[n_p:, :IDX_DIM].reshape(bs, ts, IDX_DIM))
        outs["ss"].append(s_s.astype(state_hgrn.dtype))

    y_p, y_s = _ple(h3, *ple_args(depth - 1), tm=TM_DENSE)
    st = lambda name: jnp.stack(outs[name])
    return (y_p.reshape(bp, tp, d), y_s.reshape(bs, ts, d),
            st("kp"), st("vp"), st("ikp"), st("sp"), st("ks"), st("vs"), st("iks"), st("ss"))
```

```python
import functools

import jax
import jax.numpy as jnp
from jax import lax
from jax.experimental import pallas as pl
from jax.experimental.pallas import tpu as pltpu

HEAD_DIM = 64
N_HEADS = 8
KV_HEADS = 2
IDX_HEADS = 8
IDX_DIM = 64
TOPK_MAX = 256
HG_KDIM = 128
HG_VDIM = 128
HG_HEADS = 4
HG_CHUNK = 64
ROPE_THETA = 10000.0
EPS = 1e-6
ATT_WIDTH = N_HEADS * HEAD_DIM
KV_WIDTH = KV_HEADS * HEAD_DIM
HG_WIDTH = HG_HEADS * HG_VDIM
QK_SCALE = HEAD_DIM ** -0.5
IDX_SCALE = IDX_DIM ** -0.5

LANES = 128
SUBLANES = 8
VMEM_BUDGET_BYTES = 56 * 1024 * 1024

NEG_BIG = -1e30

F32 = jnp.float32
BF16 = jnp.bfloat16


def _cparams(sem, vmem_bytes):
    return pltpu.CompilerParams(dimension_semantics=sem,
                                vmem_limit_bytes=int(min(max(vmem_bytes, 16 << 20), VMEM_BUDGET_BYTES)))


def _rms(x, g):
    return x * lax.rsqrt(jnp.mean(x * x, axis=-1, keepdims=True) + EPS) * g


def _silu(x):
    return x * jax.nn.sigmoid(x)


def _dot(a, b):
    return jnp.dot(a, b, preferred_element_type=F32)


def _dot_nt(a, b):
    return lax.dot_general(a, b, (((1,), (1,)), ((), ())), preferred_element_type=F32)


def _dot_tn(a, b):
    return lax.dot_general(a, b, (((0,), (0,)), ((), ())), preferred_element_type=F32)


def _split_maps(n_first_tiles):
    return (lambda i, *_: (jnp.minimum(i, n_first_tiles - 1), 0),
            lambda i, *_: (jnp.maximum(i - n_first_tiles, 0), 0))


def _pick(first_ref, rest_ref, n_first_tiles):
    return jnp.where(pl.program_id(0) < n_first_tiles, first_ref[...], rest_ref[...])


def _mix_rows(att, ohg, h, g, w_ref):
    mix = _dot(att.astype(BF16), w_ref[0:ATT_WIDTH, :]) + \
        _dot(ohg.astype(BF16), w_ref[ATT_WIDTH:ATT_WIDTH + HG_WIDTH, :])
    return h + _rms(mix, g)


def _ple_rows(h, p, g_pre, g_post, w_gate, w_proj):
    gate = jax.nn.sigmoid(_dot(_rms(h, g_pre).astype(BF16), w_gate))
    return h + _rms(_dot(p.astype(BF16), w_proj) * gate, g_post)


_FFN_STAGE_INPUTS = {"x": 1, "x2": 2, "mix": 5, "ple": 7}
FFN_STAGE_ROWS = 256


def _ffn_body(*refs, stage, n_first_tiles):
    n_in = _FFN_STAGE_INPUTS[stage]
    ins = refs[:n_in]
    gpre_ref, gpost_ref, wg_ref, wu_ref, wd_ref, o_ref, x_ref, xn_ref, acc_ref = refs[n_in:]
    j = pl.program_id(1)
    is_first = pl.program_id(0) < n_first_tiles

    def stage_rows(r):
        if stage == "x":
            return ins[0][r, :]
        if stage == "x2":
            return jnp.where(is_first, ins[0][r, :], ins[1][r, :])
        if stage == "mix":
            att, ohg, h, g, w = ins
            return _mix_rows(att[r, :], ohg[r, :], h[r, :], g[...], w)
        h, pp, ps, g1, g2, wgate, wproj = ins
        return _ple_rows(h[r, :], jnp.where(is_first, pp[r, :], ps[r, :]), g1[...], g2[...], wgate[...], wproj[...])

    @pl.when(j == 0)
    def _():
        tm = x_ref.shape[0]
        rc = min(FFN_STAGE_ROWS, tm)
        for r0 in range(0, tm, rc):
            r = slice(r0, r0 + rc)
            x = stage_rows(r)
            x_ref[r, :] = x
            xn_ref[r, :] = _rms(x, gpre_ref[...]).astype(BF16)
        acc_ref[...] = jnp.zeros_like(acc_ref)

    xn = xn_ref[...]
    a = _dot(xn, wg_ref[...])
    b = _dot(xn, wu_ref[...])
    acc_ref[...] += _dot((_silu(a) * b).astype(BF16), wd_ref[...])

    @pl.when(j == pl.num_programs(1) - 1)
    def _():
        o_ref[...] = x_ref[...] + 0.5 * _rms(acc_ref[...], gpost_ref[...])


def _ffn(stage, ins, g_pre, g_post, wg, wu, wd, *, n, tm, tf, n_first):
    d = wg.shape[0]
    dff = wg.shape[1]
    assert n % tm == 0 and n_first % tm == 0 and dff % tf == 0 and len(ins) == _FFN_STAGE_INPUTS[stage]
    first_map, rest_map = _split_maps(n_first // tm)
    tok = lambda w: pl.BlockSpec((tm, w), lambda i, j: (i, 0))
    whole = lambda a: pl.BlockSpec(a.shape, lambda i, j: (0,) * a.ndim)
    if stage == "x":
        in_specs = [tok(d)]
    elif stage == "x2":
        in_specs = [pl.BlockSpec((tm, d), first_map), pl.BlockSpec((tm, d), rest_map)]
    elif stage == "mix":
        in_specs = [tok(ATT_WIDTH), tok(HG_WIDTH), tok(d), whole(ins[3]), whole(ins[4])]
    else:
        pd = ins[1].shape[1]
        in_specs = [tok(d), pl.BlockSpec((tm, pd), first_map), pl.BlockSpec((tm, pd), rest_map),
                    whole(ins[3]), whole(ins[4]), whole(ins[5]), whole(ins[6])]
    tile_bytes = sum(s.block_shape[0] * s.block_shape[1] * a.dtype.itemsize for s, a in zip(in_specs, ins))
    vmem = 2 * (tile_bytes + tm * d * 4 + 3 * d * tf * 2) + tm * d * 10 + 4 * tm * tf * 4 \
        + 6 * FFN_STAGE_ROWS * d * 4 + (4 << 20)
    return pl.pallas_call(
        functools.partial(_ffn_body, stage=stage, n_first_tiles=n_first // tm),
        grid=(n // tm, dff // tf),
        in_specs=in_specs + [
            pl.BlockSpec((1, d), lambda i, j: (0, 0)),
            pl.BlockSpec((1, d), lambda i, j: (0, 0)),
            pl.BlockSpec((d, tf), lambda i, j: (0, j)),
            pl.BlockSpec((d, tf), lambda i, j: (0, j)),
            pl.BlockSpec((tf, d), lambda i, j: (j, 0)),
        ],
        out_specs=pl.BlockSpec((tm, d), lambda i, j: (i, 0)),
        out_shape=jax.ShapeDtypeStruct((n, d), F32),
        scratch_shapes=[pltpu.VMEM((tm, d), F32), pltpu.VMEM((tm, d), BF16), pltpu.VMEM((tm, d), F32)],
        compiler_params=_cparams(("parallel", "arbitrary"), vmem),
        name="ffn_" + stage,
    )(*ins, g_pre, g_post, wg, wu, wd)


_C_Q, _C_K, _C_IQ, _C_IKW, _C_V, _C_HQ, _C_HF, _C_HI, _C_HG = 0, 512, 640, 1152, 1280, 1408, 1920, 2432, 2944
_C_QR, _C_KR, _C_IQR, _C_IKR, _C_END = 3456, 3968, 4096, 4608, 4736


def _proj_body(h_ref, g_ref, w_ref, cos_ref, sin_ref,
               q_ref, k_ref, v_ref, iq_ref, ikw_ref, hq_ref, hf_ref, hi_ref, hg_ref, ikw16_ref):
    u = _rms(h_ref[...], g_ref[...]).astype(BF16)

    def mm(lo, hi):
        return _dot(u, w_ref[:, lo:hi])

    cos = cos_ref[...]
    sin = sin_ref[...]
    cos4 = jnp.concatenate([cos] * 4, axis=1)
    sin4 = jnp.concatenate([sin] * 4, axis=1)
    q_ref[...] = ((mm(_C_Q, _C_K) * cos4 + mm(_C_QR, _C_KR) * sin4) * QK_SCALE).astype(BF16)
    k_ref[...] = mm(_C_K, _C_IQ) * cos + mm(_C_KR, _C_IQR) * sin
    iq_ref[...] = ((mm(_C_IQ, _C_IKW) * cos4 + mm(_C_IQR, _C_IKR) * sin4) * IDX_SCALE).astype(BF16)
    lane = lax.broadcasted_iota(jnp.int32, cos.shape, 1)
    cos_ikw = jnp.where(lane < IDX_DIM, cos, IDX_HEADS ** -0.5)
    ikw = mm(_C_IKW, _C_V) * cos_ikw + mm(_C_IKR, _C_END) * sin
    ikw_ref[...] = ikw
    ikw16_ref[...] = ikw.astype(BF16)
    v_ref[...] = mm(_C_V, _C_HQ)
    hq_ref[...] = mm(_C_HQ, _C_HF)
    hf_ref[...] = mm(_C_HF, _C_HI)
    hi_ref[...] = mm(_C_HI, _C_HG)
    hg_ref[...] = mm(_C_HG, _C_QR)


def _proj(h, g, w_cat, cos_tab, sin_tab, *, tm, tiles_per_seq, n_prompt_tiles):
    n, d = h.shape
    outs = ((512, BF16), (128, F32), (128, F32), (512, BF16), (128, F32),
            (512, F32), (512, F32), (512, F32), (512, F32), (128, BF16))

    def tab_map(i):
        return (jnp.where(i < n_prompt_tiles, i % tiles_per_seq, tiles_per_seq), 0, 0)

    out_bytes = sum(w * jnp.dtype(t).itemsize for w, t in outs)
    vmem = 2 * (tm * d * 4 + d * _C_END * 2 + tm * out_bytes + 2 * tm * LANES * 4) + 6 * tm * 512 * 4 + (4 << 20)
    return pl.pallas_call(
        _proj_body,
        grid=(n // tm,),
        in_specs=[
            pl.BlockSpec((tm, d), lambda i: (i, 0)),
            pl.BlockSpec((1, d), lambda i: (0, 0)),
            pl.BlockSpec((d, _C_END), lambda i: (0, 0)),
            pl.BlockSpec((None, tm, LANES), tab_map),
            pl.BlockSpec((None, tm, LANES), tab_map),
        ],
        out_specs=[pl.BlockSpec((tm, w), lambda i: (i, 0)) for w, _ in outs],
        out_shape=[jax.ShapeDtypeStruct((n, w), t) for w, t in outs],
        compiler_params=_cparams(("parallel",), vmem),
        name="proj",
    )(h, g, w_cat, cos_tab, sin_tab)


_R_K, _R_KR, _R_V, _R_IK, _R_IKR, _R_END = 0, 128, 256, 384, 448, 512


def _proj_t_body(h_ref, g_ref, w_ref, cos_ref, sin_ref, k_ref, v_ref, ik_ref, k16_ref, v16_ref):
    u = _rms(h_ref[...], g_ref[...]).astype(BF16)
    r = _dot_nt(w_ref[...], u)
    cos = cos_ref[...]
    sin = sin_ref[...]
    k = r[_R_K:_R_KR] * cos + r[_R_KR:_R_V] * sin
    v = r[_R_V:_R_IK]
    ik = r[_R_IK:_R_IKR] * cos[0:IDX_DIM] + r[_R_IKR:_R_END] * sin[0:IDX_DIM]
    k_ref[...] = k
    v_ref[...] = v
    ik_ref[...] = ik
    k16_ref[...] = k.astype(BF16)
    v16_ref[...] = v.astype(BF16)


def _proj_t(h, g, w_t, cos_t, sin_t, *, batch, seq, tm):
    d = h.shape[1]
    nt = seq // tm
    outs = ((KV_WIDTH, F32), (KV_WIDTH, F32), (IDX_DIM, F32), (KV_WIDTH, BF16), (KV_WIDTH, BF16))
    vmem = 2 * (tm * d * 4 + _R_END * d * 2 + 2 * LANES * tm * 4 + 3 * LANES * tm * 6) + 4 * _R_END * tm * 4 + (4 << 20)
    return pl.pallas_call(
        _proj_t_body,
        grid=(batch, nt),
        in_specs=[
            pl.BlockSpec((tm, d), lambda b, t: (b * nt + t, 0)),
            pl.BlockSpec((1, d), lambda b, t: (0, 0)),
            pl.BlockSpec((_R_END, d), lambda b, t: (0, 0)),
            pl.BlockSpec((None, LANES, tm), lambda b, t: (t, 0, 0)),
            pl.BlockSpec((None, LANES, tm), lambda b, t: (t, 0, 0)),
        ],
        out_specs=[pl.BlockSpec((None, w, tm), lambda b, t: (b, 0, t)) for w, _ in outs],
        out_shape=[jax.ShapeDtypeStruct((batch, w, seq), t) for w, t in outs],
        compiler_params=_cparams(("parallel", "parallel"), vmem),
        name="proj_t",
    )(h, g, w_t, cos_t, sin_t)


def _tri_incl(n):
    a = lax.broadcasted_iota(jnp.int32, (n, n), 0)
    b = lax.broadcasted_iota(jnp.int32, (n, n), 1)
    return a, b


I16_MIN = -32768
I16_ROWS = 16


def _tree_sum(xs):
    xs = list(xs)
    while len(xs) > 1:
        nxt = [xs[i] + xs[i + 1] for i in range(0, len(xs) - 1, 2)]
        if len(xs) % 2:
            nxt.append(xs[-1])
        xs = nxt
    return xs[0]


def _count_cols(mask, dtype):
    rows = I16_ROWS if dtype == jnp.int16 else SUBLANES
    one = jnp.where(mask, jnp.ones((), dtype), jnp.zeros((), dtype))
    part = _tree_sum([one[c * rows:(c + 1) * rows] for c in range(mask.shape[0] // rows)])
    return jnp.sum(part.astype(F32), axis=0, keepdims=True)


def _search16(x16, k):
    def body(i, t):
        cand = t + lax.shift_left(jnp.int32(1), 15 - i)
        cnt = _count_cols(x16 >= cand.astype(jnp.int16), jnp.int16)
        return jnp.where(cnt >= k, cand, t)

    return lax.fori_loop(0, 16, body, jnp.full(k.shape, I16_MIN, jnp.int32))


def _topk_mask_t(score, adm, k):
    l, n = score.shape
    score = jnp.where(score == 0.0, 0.0, score)
    score = jnp.where(adm, score, -jnp.inf)
    bits = lax.bitcast_convert_type(score, jnp.int32)
    key = bits ^ ((bits >> 31) & 0x7FFFFFFF)
    kf = jnp.full((1, n), float(k), F32)
    hi = (key >> 16).astype(jnp.int16)
    t_hi = _search16(hi, kf)
    t_hi16 = t_hi.astype(jnp.int16)
    above = _count_cols(hi > t_hi16, jnp.int16)
    lo = ((key & 0xFFFF) + I16_MIN).astype(jnp.int16)
    lo = jnp.where(hi == t_hi16, lo, jnp.int16(I16_MIN))
    t_lo = _search16(lo, kf - above)
    t = lax.shift_left(t_hi, 16) | (t_lo - I16_MIN)
    gt = key > t
    eq = key == t
    need = kf - _count_cols(gt, jnp.float32)
    a, b = _tri_incl(LANES)
    tril = jnp.where(b <= a, 1.0, 0.0).astype(BF16)
    carry = jnp.zeros((1, n), F32)
    sel = []
    for c in range(l // LANES):
        rows = slice(c * LANES, (c + 1) * LANES)
        pref = _dot(tril, jnp.where(eq[rows], 1.0, 0.0).astype(BF16)) + carry
        carry = pref[LANES - 1:LANES]
        take = jnp.logical_or(gt[rows], jnp.logical_and(eq[rows], pref <= need))
        sel.append(jnp.where(jnp.logical_and(take, adm[rows]), 1.0, 0.0))
    return jnp.concatenate(sel, axis=0) if len(sel) > 1 else sel[0]


def _dsa_prompt_tile(q_ref, iq_ref, iw_ref, kt_ref, vt_ref, ik_ref, o_ref, *, q0, l, topk):
    qb = q_ref.shape[0]
    iq_t = iq_ref[...].astype(F32).T.astype(BF16)
    w_t = iw_ref[...].T[IDX_DIM:IDX_DIM + IDX_HEADS]
    ik = ik_ref[0:l, 0:IDX_DIM]
    score = jnp.zeros((l, qb), F32)
    for h in range(IDX_HEADS):
        lg = _dot(ik, iq_t[h * IDX_DIM:(h + 1) * IDX_DIM])
        score = score + w_t[h:h + 1] * jnp.maximum(lg, 0.0)

    kpos = lax.broadcasted_iota(jnp.int32, (l, qb), 0)
    qpos = q0 + lax.broadcasted_iota(jnp.int32, (l, qb), 1)
    sel_t = _topk_mask_t(score, kpos <= qpos, topk)
    bias = ((sel_t - 1.0) * (-NEG_BIG)).T

    q = q_ref[...]
    group = N_HEADS // KV_HEADS
    for h in range(N_HEADS):
        g = h // group
        s = _dot(q[:, h * HEAD_DIM:(h + 1) * HEAD_DIM], kt_ref[g * HEAD_DIM:(g + 1) * HEAD_DIM, 0:l]) + bias
        p = jnp.exp(s - jnp.max(s, axis=-1, keepdims=True))
        den = jnp.sum(p, axis=-1, keepdims=True)
        o = _dot_nt(p.astype(BF16), vt_ref[g * HEAD_DIM:(g + 1) * HEAD_DIM, 0:l])
        o_ref[:, h * HEAD_DIM:(h + 1) * HEAD_DIM] = o / den


def _dsa_prompt_body(q_ref, iq_ref, iw_ref, kt_ref, vt_ref, ik_ref, o_ref, *, topk, n_len):
    qi = pl.program_id(1)
    nq = pl.num_programs(1)
    is_fill = pl.program_id(0) == pl.num_programs(0) - 1
    qb = q_ref.shape[0]
    seq = kt_ref.shape[1]
    per = nq // n_len
    for v in range(n_len):
        l = (v + 1) * per * qb

        @pl.when(jnp.logical_and(jnp.logical_not(is_fill), qi // per == v))
        def _(l=l):
            _dsa_prompt_tile(q_ref, iq_ref, iw_ref, kt_ref, vt_ref, ik_ref, o_ref,
                             q0=qi * qb, l=min(l, seq), topk=topk)

    @pl.when(is_fill)
    def _():
        o_ref[...] = jnp.zeros_like(o_ref)


def _dsa_prompt(q16, iq16, ikw, ikw16, kt16, vt16, *, batch, seq, qb, n_len):
    topk = min(TOPK_MAX, seq // 4)
    nq = seq // qb
    n_len = n_len if nq % n_len == 0 else 1
    last_blk = q16.shape[0] // qb - 1
    assert q16.shape[0] % qb == 0 and last_blk < (batch + 1) * nq
    tok_map = lambda bi, qi: (jnp.minimum(bi * nq + qi, last_blk), 0)
    seq_map = lambda bi, qi: (jnp.minimum(bi, batch - 1), 0, 0)
    vmem = 2 * (2 * qb * 512 * 2 + qb * LANES * 4 + qb * 512 * 4 + seq * (2 * KV_WIDTH + LANES) * 2) \
        + 16 * qb * seq * 4 + (4 << 20)
    return pl.pallas_call(
        functools.partial(_dsa_prompt_body, topk=topk, n_len=n_len),
        grid=(batch + 1, nq),
        in_specs=[
            pl.BlockSpec((qb, ATT_WIDTH), tok_map),
            pl.BlockSpec((qb, IDX_HEADS * IDX_DIM), tok_map),
            pl.BlockSpec((qb, LANES), tok_map),
            pl.BlockSpec((None, KV_WIDTH, seq), seq_map),
            pl.BlockSpec((None, KV_WIDTH, seq), seq_map),
            pl.BlockSpec((seq, LANES), lambda bi, qi: (jnp.minimum(bi, batch - 1), 0)),
        ],
        out_specs=pl.BlockSpec((qb, ATT_WIDTH), tok_map),
        out_shape=jax.ShapeDtypeStruct((q16.shape[0], ATT_WIDTH), F32),
        compiler_params=_cparams(("arbitrary", "arbitrary"), vmem),
        name="dsa_prompt",
    )(q16, iq16, ikw, kt16, vt16, ikw16)


def _dsa_sample_body(pt_ref, q_ref, iq_ref, ikw_ref, kn_ref, vn_ref, ck_hbm, cv_hbm, ci_hbm, att_hbm, o_ref,
                     kbuf, vbuf, ibuf, sems, *, layer, n_pages, page, tq, nb, topk):
    del att_hbm
    g = pl.program_id(0)
    slot = g % 2
    past = n_pages * page
    l = past + LANES
    group = N_HEADS // KV_HEADS

    def page_copies(step, s):
        out = []
        for bj in range(nb):
            for p in range(n_pages):
                idx = pt_ref[(step * nb + bj) * n_pages + p]
                cols = pl.ds(p * page, page)
                out.append(pltpu.make_async_copy(ck_hbm.at[layer, idx], kbuf.at[s, bj, :, cols], sems.at[s, 0]))
                out.append(pltpu.make_async_copy(cv_hbm.at[layer, idx], vbuf.at[s, bj, :, cols], sems.at[s, 1]))
                out.append(pltpu.make_async_copy(ci_hbm.at[layer, idx], ibuf.at[s, bj, :, cols], sems.at[s, 2]))
        return out

    @pl.when(g == 0)
    def _():
        for c in page_copies(0, 0):
            c.start()

    @pl.when(g + 1 < pl.num_programs(0))
    def _():
        for c in page_copies(g + 1, 1 - slot):
            c.start()

    for c in page_copies(g, slot):
        c.wait()

    def pad_rows(x):
        return jnp.concatenate([x, jnp.zeros((LANES - tq, x.shape[1]), x.dtype)], axis=0)

    iq_all = iq_ref[...].astype(F32)
    q_all = q_ref[...].astype(F32)
    ikw_all = ikw_ref[...]

    scores = []
    for bj in range(nb):
        rows = slice(bj * tq, (bj + 1) * tq)
        iq = iq_all[rows]
        iq_hm = jnp.concatenate([iq[:, h * IDX_DIM:(h + 1) * IDX_DIM] for h in range(IDX_HEADS)],
                                axis=0).astype(BF16)
        ik_new = pad_rows(ikw_all[rows, 0:IDX_DIM]).astype(BF16)
        lg = jnp.concatenate([_dot(iq_hm, ibuf[slot, bj].astype(BF16)), _dot_nt(iq_hm, ik_new)], axis=1)
        lg = jnp.maximum(lg, 0.0)
        w = ikw_all[rows, IDX_DIM:IDX_DIM + IDX_HEADS]
        sc = jnp.zeros((tq, l), F32)
        for h in range(IDX_HEADS):
            sc = sc + w[:, h:h + 1] * lg[h * tq:(h + 1) * tq, :]
        scores.append(sc)
    nq = nb * tq
    score_t = jnp.concatenate(scores + [jnp.zeros((LANES - nq, l), F32)], axis=0).T
    kidx = lax.broadcasted_iota(jnp.int32, (l, LANES), 0)
    qidx = lax.broadcasted_iota(jnp.int32, (l, LANES), 1)
    adm = kidx <= past + qidx % tq
    adm = jnp.logical_and(adm, kidx < past + tq)
    adm = jnp.logical_and(adm, qidx < nq)
    sel_t = _topk_mask_t(score_t, adm, topk)
    bias = ((sel_t - 1.0) * (-NEG_BIG)).T

    for bj in range(nb):
        rows = slice(bj * tq, (bj + 1) * tq)
        q = q_all[rows]
        bias_g = jnp.concatenate([bias[rows]] * group, axis=0)
        k_new = pad_rows(kn_ref[rows, :]).astype(BF16)
        v_new = pad_rows(vn_ref[rows, :]).astype(BF16)
        for hk in range(KV_HEADS):
            fs = slice(hk * HEAD_DIM, (hk + 1) * HEAD_DIM)
            qg = jnp.concatenate([q[:, h * HEAD_DIM:(h + 1) * HEAD_DIM]
                                  for h in range(hk * group, (hk + 1) * group)], axis=0).astype(BF16)
            kt = kbuf[slot, bj, fs, :].astype(BF16)
            vt = vbuf[slot, bj, fs, :].astype(BF16)
            s = jnp.concatenate([_dot(qg, kt), _dot_nt(qg, k_new[:, fs])], axis=1) + bias_g
            p = jnp.exp(s - jnp.max(s, axis=-1, keepdims=True))
            den = jnp.sum(p, axis=-1, keepdims=True)
            p = p.astype(BF16)
            o = (_dot_nt(p[:, 0:past], vt) + _dot(p[:, past:l], v_new[:, fs])) / den
            for j in range(group):
                h = hk * group + j
                o_ref[rows, h * HEAD_DIM:(h + 1) * HEAD_DIM] = o[j * tq:(j + 1) * tq, :]


def _dsa_sample(page_table, q16, iq16, ikw, k, v, cache_kt, cache_vt, cache_ikt, att, *, layer, row0, batch, tq, nb):
    n_pages = page_table.shape[1]
    page = cache_kt.shape[3]
    past = n_pages * page
    topk = min(TOPK_MAX, (past + tq) // 4)
    blk0 = row0 // (nb * tq)
    pt = page_table.reshape(-1)

    def tok(w):
        return pl.BlockSpec((nb * tq, w), lambda gi, pt_ref: (blk0 + gi, 0))

    hbm = pl.BlockSpec(memory_space=pl.ANY)
    in_specs = [tok(ATT_WIDTH), tok(IDX_HEADS * IDX_DIM), tok(LANES), tok(KV_WIDTH), tok(KV_WIDTH),
                hbm, hbm, hbm, hbm]
    buf_bytes = 2 * nb * past * (2 * KV_WIDTH + IDX_DIM) * 4
    vmem = buf_bytes + 16 * nb * (past + LANES) * LANES * 4 + (8 << 20)
    return pl.pallas_call(
        functools.partial(_dsa_sample_body, layer=layer, n_pages=n_pages, page=page, tq=tq, nb=nb, topk=topk),
        grid_spec=pltpu.PrefetchScalarGridSpec(
            num_scalar_prefetch=1,
            grid=(batch // nb,),
            in_specs=in_specs,
            out_specs=tok(ATT_WIDTH),
            scratch_shapes=[pltpu.VMEM((2, nb, KV_WIDTH, past), F32),
                            pltpu.VMEM((2, nb, KV_WIDTH, past), F32),
                            pltpu.VMEM((2, nb, IDX_DIM, past), F32),
                            pltpu.SemaphoreType.DMA((2, 3))],
        ),
        out_shape=jax.ShapeDtypeStruct(att.shape, att.dtype),
        input_output_aliases={len(in_specs): 0},
        compiler_params=_cparams(("arbitrary",), vmem),
        name="dsa_sample",
    )(pt, q16, iq16, ikw, k, v, cache_kt, cache_vt, cache_ikt, att)


def _hgrn_levels(c):
    out = []
    n = c // 2
    while n >= 1:
        out.append(n)
        n //= 2
    return out


def _group_row(x, j, g):
    c, n = x.shape
    x3 = x.reshape(c // g, g, n)
    return jnp.broadcast_to(x3[:, j:j + 1, :], (c // g, g, n)).reshape(c, n)


def _boundary_rows(b, n, row):
    g = 2 * n
    if g >= SUBLANES:
        return _group_row(b, n - 1, g)
    out = _group_row(b, n - 1, SUBLANES)
    for j in range(1, SUBLANES // g):
        out = jnp.where((row % SUBLANES) // g == j, _group_row(b, j * g + n - 1, SUBLANES), out)
    return out


def _hgrn_chunk(xq, xf, xi, lb, state_t):
    c, dk = xq.shape
    f = lb + (1.0 - lb) * jax.nn.sigmoid(xf)
    logf = jnp.log(f)
    kk = 1.0 - f
    qf = _silu(xq) * (HG_KDIM ** -0.5)
    ta, tb = _tri_incl(c)
    row = lax.broadcasted_iota(jnp.int32, (c, 1), 0)
    b = logf
    step = 1
    while step < c:
        b = b + jnp.where(row >= step, pltpu.roll(b, step, 0), 0.0)
        step *= 2

    o = _dot_nt((qf * jnp.exp(b)).astype(BF16), state_t.astype(BF16))
    o = o + jnp.sum(qf * kk, axis=-1, keepdims=True) * xi
    xi16 = xi.astype(BF16)
    att = jnp.zeros((c, c), F32)
    for n in _hgrn_levels(c):
        upper = (row % (2 * n)) >= n
        e = jnp.exp(-jnp.abs(b - _boundary_rows(b, n, row)))
        m = (jnp.where(upper, qf, kk) * e).astype(BF16)
        pair = jnp.logical_and(ta // (2 * n) == tb // (2 * n),
                               jnp.logical_and((ta % (2 * n)) >= n, (tb % (2 * n)) < n))
        att = att + jnp.where(pair, _dot_nt(m, m), 0.0)
    o = o + _dot(att.astype(BF16), xi16)

    blast = b[c - 1:c]
    kd = (kk * jnp.exp(blast - b)).astype(BF16)
    new_state_t = state_t * jnp.exp(blast) + _dot_tn(xi16, kd)
    return o, new_state_t


def _hgrn_gate(o, gnorm, xg):
    return _rms(o, gnorm) * _silu(xg)


def _hgrn_prompt_body(hq_ref, hf_ref, hi_ref, hg_ref, lb_ref, gn_ref, o_ref, s_ref, st_ref, *, chunk):
    gi = pl.program_id(1)
    is_fill = pl.program_id(0) == pl.num_programs(0) - 1

    @pl.when(is_fill)
    def _():
        o_ref[...] = jnp.zeros_like(o_ref)

    @pl.when(jnp.logical_not(is_fill))
    def _():
        @pl.when(gi == 0)
        def _():
            st_ref[...] = jnp.zeros_like(st_ref)

        n_chunks = hq_ref.shape[0] // chunk
        gn = gn_ref[...]

        def step(ci, carry):
            rows = pl.ds(pl.multiple_of(ci * chunk, chunk), chunk)
            for h in range(HG_HEADS):
                cols = slice(h * HG_KDIM, (h + 1) * HG_KDIM)
                o, st = _hgrn_chunk(hq_ref[rows, cols], hf_ref[rows, cols], hi_ref[rows, cols],
                                    lb_ref[:, cols], st_ref[h])
                st_ref[h] = st
                o_ref[rows, cols] = _hgrn_gate(o, gn, hg_ref[rows, cols])
            return carry

        lax.fori_loop(0, n_chunks, step, 0, unroll=2)

        @pl.when(gi == pl.num_programs(1) - 1)
        def _():
            for h in range(HG_HEADS):
                s_ref[h] = st_ref[h].T


def _hgrn_prompt(hq, hf, hi, hg, lb, gnorm, *, batch, seq, rows_per_step):
    ng = seq // rows_per_step
    chunk = min(HG_CHUNK, seq)
    last_blk = hq.shape[0] // rows_per_step - 1
    assert hq.shape[0] % rows_per_step == 0 and last_blk < (batch + 1) * ng

    def tok():
        return pl.BlockSpec((rows_per_step, HG_WIDTH), lambda bi, gi: (jnp.minimum(bi * ng + gi, last_blk), 0))

    vmem = 2 * 5 * rows_per_step * HG_WIDTH * 4 + 3 * HG_HEADS * HG_KDIM * HG_VDIM * 4 + (8 << 20)
    return pl.pallas_call(
        functools.partial(_hgrn_prompt_body, chunk=chunk),
        grid=(batch + 1, ng),
        in_specs=[tok(), tok(), tok(), tok(),
                  pl.BlockSpec((1, HG_WIDTH), lambda bi, gi: (0, 0)),
                  pl.BlockSpec((1, HG_VDIM), lambda bi, gi: (0, 0))],
        out_specs=[tok(), pl.BlockSpec((None, HG_HEADS, HG_KDIM, HG_VDIM),
                                       lambda bi, gi: (jnp.minimum(bi, batch - 1), 0, 0, 0))],
        out_shape=[jax.ShapeDtypeStruct((hq.shape[0], HG_WIDTH), F32),
                   jax.ShapeDtypeStruct((batch, HG_HEADS, HG_KDIM, HG_VDIM), F32)],
        scratch_shapes=[pltpu.VMEM((HG_HEADS, HG_KDIM, HG_VDIM), F32)],
        compiler_params=_cparams(("arbitrary", "arbitrary"), vmem),
        name="hgrn_prompt",
    )(hq, hf, hi, hg, lb, gnorm)


def _hgrn_sample_body(hq_ref, hf_ref, hi_ref, hg_ref, lb_ref, gn_ref, s0_ref, base_ref, o_ref, s_ref, *, tq, nb):
    del base_ref
    gn = gn_ref[...]

    def step(bi, carry):
        rows = pl.ds(pl.multiple_of(bi * tq, tq), tq)
        for h in range(HG_HEADS):
            cols = slice(h * HG_KDIM, (h + 1) * HG_KDIM)
            o, st = _hgrn_chunk(hq_ref[rows, cols], hf_ref[rows, cols], hi_ref[rows, cols],
                                lb_ref[:, cols], s0_ref[bi, h].T)
            s_ref[bi, h] = st.T
            o_ref[rows, cols] = _hgrn_gate(o, gn, hg_ref[rows, cols])
        return carry

    lax.fori_loop(0, nb, step, 0, unroll=2)


def _hgrn_sample(hq, hf, hi, hg, lb, gnorm, state, ohg, *, layer, row0, batch, tq, nb):
    assert tq <= HG_CHUNK
    blk0 = row0 // (nb * tq)

    def tok():
        return pl.BlockSpec((nb * tq, HG_WIDTH), lambda gi: (blk0 + gi, 0))

    st_blk = (nb, HG_HEADS, HG_KDIM, HG_VDIM)
    vmem = 2 * (5 * nb * tq * HG_WIDTH * 4 + 2 * nb * HG_HEADS * HG_KDIM * HG_VDIM * 4) + (8 << 20)
    return pl.pallas_call(
        functools.partial(_hgrn_sample_body, tq=tq, nb=nb),
        grid=(batch // nb,),
        in_specs=[tok(), tok(), tok(), tok(),
                  pl.BlockSpec((1, HG_WIDTH), lambda gi: (0, 0)),
                  pl.BlockSpec((1, HG_VDIM), lambda gi: (0, 0)),
                  pl.BlockSpec((None,) + st_blk, lambda gi: (layer, gi, 0, 0, 0)),
                  pl.BlockSpec(memory_space=pl.ANY)],
        out_specs=[tok(), pl.BlockSpec(st_blk, lambda gi: (gi, 0, 0, 0))],
        input_output_aliases={7: 0},
        out_shape=[jax.ShapeDtypeStruct(ohg.shape, ohg.dtype),
                   jax.ShapeDtypeStruct((batch, HG_HEADS, HG_KDIM, HG_VDIM), F32)],
        compiler_params=_cparams(("parallel",), vmem),
        name="hgrn_sample",
    )(hq, hf, hi, hg, lb, gnorm, state, ohg)


def _ple_body(h_ref, pp_ref, ps_ref, gpre_ref, gpost_ref, wgate_ref, wproj_ref, yp_ref, ys_ref, *, n_first_tiles):
    y = _ple_rows(h_ref[...], _pick(pp_ref, ps_ref, n_first_tiles), gpre_ref[...], gpost_ref[...],
                  wgate_ref[...], wproj_ref[...])
    is_first = pl.program_id(0) < n_first_tiles

    @pl.when(is_first)
    def _():
        yp_ref[...] = y

    @pl.when(jnp.logical_not(is_first))
    def _():
        ys_ref[...] = y


def _ple(h, p_first, p_rest, g_pre, g_post, w_gate, w_proj, *, tm):
    n, d = h.shape
    pd = p_first.shape[1]
    n_first = p_first.shape[0] // tm
    assert p_first.shape[0] % tm == 0 and p_rest.shape[0] % tm == 0
    first_map, rest_map = _split_maps(n_first)
    vmem = 2 * (3 * tm * d * 4 + 2 * tm * pd * 4 + d * d * 2 + pd * d * 2) + 3 * tm * d * 4 + (4 << 20)
    return pl.pallas_call(
        functools.partial(_ple_body, n_first_tiles=n_first),
        grid=(n // tm,),
        in_specs=[pl.BlockSpec((tm, d), lambda i: (i, 0)),
                  pl.BlockSpec((tm, pd), first_map),
                  pl.BlockSpec((tm, pd), rest_map),
                  pl.BlockSpec((1, d), lambda i: (0, 0)),
                  pl.BlockSpec((1, d), lambda i: (0, 0)),
                  pl.BlockSpec((d, d), lambda i: (0, 0)),
                  pl.BlockSpec((pd, d), lambda i: (0, 0))],
        out_specs=[pl.BlockSpec((tm, d), first_map), pl.BlockSpec((tm, d), rest_map)],
        out_shape=[jax.ShapeDtypeStruct((p_first.shape[0], d), F32), jax.ShapeDtypeStruct((p_rest.shape[0], d), F32)],
        compiler_params=_cparams(("arbitrary",), vmem),
        name="ple",
    )(h, p_first, p_rest, g_pre, g_post, w_gate, w_proj)


def _rot_cols(w, heads):
    d = w.shape[0]
    w4 = w.reshape(d, heads, 2, HEAD_DIM // 2)
    return jnp.concatenate([-w4[:, :, 1:2], w4[:, :, 0:1]], axis=2).reshape(d, heads * HEAD_DIM)


def _pack_w_in(w):
    d = w.shape[0]
    widths = (ATT_WIDTH, KV_WIDTH, KV_WIDTH, IDX_HEADS * IDX_DIM, IDX_DIM, IDX_HEADS,
              HG_HEADS * HG_KDIM, HG_HEADS * HG_KDIM, HG_WIDTH, HG_WIDTH)
    parts = []
    acc = 0
    for wd in widths:
        parts.append(w[:, acc:acc + wd])
        acc += wd
    q, k, v, iq, ik, iw, hq, hf, hi, hg = parts
    z = lambda n: jnp.zeros((d, n), w.dtype)
    ikw = jnp.concatenate([ik, iw, z(LANES - IDX_DIM - IDX_HEADS)], axis=1)
    ikr = jnp.concatenate([_rot_cols(ik, 1), z(LANES - IDX_DIM)], axis=1)
    cat = jnp.concatenate([q, k, iq, ikw, v, hq, hf, hi, hg,
                           _rot_cols(q, N_HEADS), _rot_cols(k, KV_HEADS), _rot_cols(iq, IDX_HEADS), ikr], axis=1)
    assert cat.shape[1] == _C_END
    w_t = jnp.concatenate([k, _rot_cols(k, KV_HEADS), v, ik, _rot_cols(ik, 1)], axis=1).T
    assert w_t.shape[0] == _R_END
    return cat.astype(BF16), w_t.astype(BF16)


def _rope_tables(seq, past_len, tq, tm):
    half = HEAD_DIM // 2
    inv_freq = ROPE_THETA ** (-jnp.arange(half, dtype=F32) / half)
    pos_p = jnp.arange(seq, dtype=jnp.int32)
    pos_s = past_len + (jnp.arange(tm, dtype=jnp.int32) % tq)
    pos = jnp.concatenate([pos_p, pos_s]).astype(F32)
    ang = pos[:, None] * inv_freq[None, :]
    reps = LANES // half
    cos = jnp.tile(jnp.cos(ang), (1, reps)).reshape(seq // tm + 1, tm, LANES)
    sin = jnp.tile(jnp.sin(ang), (1, reps)).reshape(seq // tm + 1, tm, LANES)
    nt = seq // tm
    cos_t = jnp.swapaxes(cos[:nt], 1, 2)
    sin_t = jnp.swapaxes(sin[:nt], 1, 2)
    return cos, sin, cos_t, sin_t


def _lower_bounds(logits):
    sm = jax.nn.softmax(logits.astype(F32), axis=0)
    return jnp.cumsum(sm, axis=0) - sm[0:1]


def _feature_major(cache, width):
    depth, pool, page = cache.shape[:3]
    c = cache.reshape(depth, pool, page, width)
    return jnp.swapaxes(c, 2, 3)


TM_DENSE = 1024
TM_PROJ = 512
TM_FFN = 1024
TF_FFN = 256
Q_BLOCK = 128
N_KEY_LEN = 8
DSA_SEQS = 4
HG_ROWS = 512
HG_SEQS = 8


def kernel(x_prompt, x_sample, p_prompt, p_sample, cache_k, cache_v, cache_idx_k, state_hgrn, page_table,
           n_f1_pre, n_f1_post, w_f1_gate, w_f1_up, w_f1_down,
           n_mix_pre, n_mix_post, w_in, hg_lb_logits, hg_norm, w_out,
           n_f2_pre, n_f2_post, w_f2_gate, w_f2_up, w_f2_down,
           n_ple_pre, n_ple_post, w_ple_proj, w_ple_gate):
    bp, tp, d = x_prompt.shape
    bs, ts, _ = x_sample.shape
    depth = w_in.shape[0]
    n_p, n_s = bp * tp, bs * ts
    page = cache_k.shape[2]
    past_len = page_table.shape[1] * page

    ckt = _feature_major(cache_k, KV_WIDTH)
    cvt = _feature_major(cache_v, KV_WIDTH)
    cikt = _feature_major(cache_idx_k, IDX_DIM)
    cos_tab, sin_tab, cos_t, sin_t = _rope_tables(tp, past_len, ts, TM_PROJ)
    lbs = _lower_bounds(hg_lb_logits)
    row = lambda a: a.reshape(1, -1)
    bf = lambda a: a.astype(BF16)

    outs = {name: [] for name in ("kp", "vp", "ikp", "sp", "ks", "vs", "iks", "ss")}
    ffn_kw = dict(n=n_p + n_s, tm=TM_FFN, tf=TF_FFN, n_first=n_p)
    ple_args = lambda i: (p_prompt[i].reshape(n_p, -1), p_sample[i].reshape(n_s, -1),
                          row(n_ple_pre[i]), row(n_ple_post[i]), bf(w_ple_gate[i]), bf(w_ple_proj[i]))
    h3 = None
    for i in range(depth):
        stage, ins = ("x2", (x_prompt.reshape(n_p, d), x_sample.reshape(n_s, d))) if i == 0 else \
            ("ple", (h3,) + ple_args(i - 1))
        h1 = _ffn(stage, ins, row(n_f1_pre[i]), row(n_f1_post[i]), bf(w_f1_gate[i]), bf(w_f1_up[i]),
                  bf(w_f1_down[i]), **ffn_kw)
        w_cat, w_t = _pack_w_in(w_in[i])
        g_mix = row(n_mix_pre[i])
        q16, k, v, iq16, ikw, hq, hf, hi, hg, ikw16 = _proj(
            h1, g_mix, w_cat, cos_tab, sin_tab,
            tm=TM_PROJ, tiles_per_seq=tp // TM_PROJ, n_prompt_tiles=n_p // TM_PROJ)
        kt, vt, ikt, kt16, vt16 = _proj_t(h1, g_mix, w_t, cos_t, sin_t, batch=bp, seq=tp, tm=TM_PROJ)

        att = _dsa_prompt(q16, iq16, ikw, ikw16, kt16, vt16, batch=bp, seq=tp, qb=min(Q_BLOCK, tp),
                          n_len=N_KEY_LEN)
        att = _dsa_sample(page_table, q16, iq16, ikw, k, v, ckt, cvt, cikt, att,
                          layer=i, row0=n_p, batch=bs, tq=ts, nb=DSA_SEQS)
        lb = row(lbs[i])
        gn = row(hg_norm[i])
        ohg, s_p = _hgrn_prompt(hq, hf, hi, hg, lb, gn, batch=bp, seq=tp, rows_per_step=HG_ROWS)
        ohg, s_s = _hgrn_sample(hq, hf, hi, hg, lb, gn, state_hgrn, ohg,
                                layer=i, row0=n_p, batch=bs, tq=ts, nb=HG_SEQS)

        h3 = _ffn("mix", (att, ohg, h1, row(n_mix_post[i]), bf(w_out[i])),
                  row(n_f2_pre[i]), row(n_f2_post[i]), bf(w_f2_gate[i]), bf(w_f2_up[i]), bf(w_f2_down[i]), **ffn_kw)

        outs["kp"].append(jnp.transpose(kt.reshape(bp, KV_HEADS, HEAD_DIM, tp), (0, 3, 1, 2)))
        outs["vp"].append(jnp.transpose(vt.reshape(bp, KV_HEADS, HEAD_DIM, tp), (0, 3, 1, 2)))
        outs["ikp"].append(jnp.swapaxes(ikt, 1, 2))
        outs["sp"].append(s_p)
        outs["ks"].append(k[n_p:].reshape(bs, ts, KV_HEADS, HEAD_DIM))
        outs["vs"].append(v[n_p:].reshape(bs, ts, KV_HEADS, HEAD_DIM))
        outs["iks"].append(ikw[n_p:, :IDX_DIM].reshape(bs, ts, IDX_DIM))
        outs["ss"].append(s_s.astype(state_hgrn.dtype))

    y_p, y_s = _ple(h3, *ple_args(depth - 1), tm=TM_DENSE)
    st = lambda name: jnp.stack(outs[name])
    return (y_p.reshape(bp, tp, d), y_s.reshape(bs, ts, d),
            st("kp"), st("vp"), st("ikp"), st("sp"), st("ks"), st("vs"), st("iks"), st("ss"))
```

```python
import functools

import jax
import jax.numpy as jnp
from jax import lax
from jax.experimental import pallas as pl
from jax.experimental.pallas import tpu as pltpu

HEAD_DIM = 64
N_HEADS = 8
KV_HEADS = 2
IDX_HEADS = 8
IDX_DIM = 64
TOPK_MAX = 256
HG_KDIM = 128
HG_VDIM = 128
HG_HEADS = 4
HG_CHUNK = 64
ROPE_THETA = 10000.0
EPS = 1e-6
ATT_WIDTH = N_HEADS * HEAD_DIM
KV_WIDTH = KV_HEADS * HEAD_DIM
HG_WIDTH = HG_HEADS * HG_VDIM
QK_SCALE = HEAD_DIM ** -0.5
IDX_SCALE = IDX_DIM ** -0.5

LANES = 128
SUBLANES = 8
VMEM_BUDGET_BYTES = 56 * 1024 * 1024

NEG_BIG = -1e30

F32 = jnp.float32
BF16 = jnp.bfloat16


def _cparams(sem, vmem_bytes):
    return pltpu.CompilerParams(dimension_semantics=sem,
                                vmem_limit_bytes=int(min(max(vmem_bytes, 16 << 20), VMEM_BUDGET_BYTES)))


def _rms(x, g):
    return x * lax.rsqrt(jnp.mean(x * x, axis=-1, keepdims=True) + EPS) * g


def _silu(x):
    return x * jax.nn.sigmoid(x)


def _dot(a, b):
    return jnp.dot(a, b, preferred_element_type=F32)


def _dot_nt(a, b):
    return lax.dot_general(a, b, (((1,), (1,)), ((), ())), preferred_element_type=F32)


def _dot_tn(a, b):
    return lax.dot_general(a, b, (((0,), (0,)), ((), ())), preferred_element_type=F32)


def _split_maps(n_first_tiles):
    return (lambda i, *_: (jnp.minimum(i, n_first_tiles - 1), 0),
            lambda i, *_: (jnp.maximum(i - n_first_tiles, 0), 0))


def _pick(first_ref, rest_ref, n_first_tiles):
    return jnp.where(pl.program_id(0) < n_first_tiles, first_ref[...], rest_ref[...])


def _mix_rows(att, ohg, h, g, w_ref):
    mix = _dot(att.astype(BF16), w_ref[0:ATT_WIDTH, :]) + \
        _dot(ohg.astype(BF16), w_ref[ATT_WIDTH:ATT_WIDTH + HG_WIDTH, :])
    return h + _rms(mix, g)


def _ple_rows(h, p, g_pre, g_post, w_gate, w_proj):
    gate = jax.nn.sigmoid(_dot(_rms(h, g_pre).astype(BF16), w_gate))
    return h + _rms(_dot(p.astype(BF16), w_proj) * gate, g_post)


_FFN_STAGE_INPUTS = {"x": 1, "x2": 2, "mix": 5, "ple": 7}
FFN_STAGE_ROWS = 256


def _ffn_body(*refs, stage, n_first_tiles):
    n_in = _FFN_STAGE_INPUTS[stage]
    ins = refs[:n_in]
    gpre_ref, gpost_ref, wg_ref, wu_ref, wd_ref, o_ref, x_ref, xn_ref, acc_ref = refs[n_in:]
    j = pl.program_id(1)
    is_first = pl.program_id(0) < n_first_tiles

    def stage_rows(r):
        if stage == "x":
            return ins[0][r, :]
        if stage == "x2":
            return jnp.where(is_first, ins[0][r, :], ins[1][r, :])
        if stage == "mix":
            att, ohg, h, g, w = ins
            return _mix_rows(att[r, :], ohg[r, :], h[r, :], g[...], w)
        h, pp, ps, g1, g2, wgate, wproj = ins
        return _ple_rows(h[r, :], jnp.where(is_first, pp[r, :], ps[r, :]), g1[...], g2[...], wgate[...], wproj[...])

    @pl.when(j == 0)
    def _():
        tm = x_ref.shape[0]
        rc = min(FFN_STAGE_ROWS, tm)
        for r0 in range(0, tm, rc):
            r = slice(r0, r0 + rc)
            x = stage_rows(r)
            x_ref[r, :] = x
            xn_ref[r, :] = _rms(x, gpre_ref[...]).astype(BF16)
        acc_ref[...] = jnp.zeros_like(acc_ref)

    xn = xn_ref[...]
    a = _dot(xn, wg_ref[...])
    b = _dot(xn, wu_ref[...])
    acc_ref[...] += _dot((_silu(a) * b).astype(BF16), wd_ref[...])

    @pl.when(j == pl.num_programs(1) - 1)
    def _():
        o_ref[...] = x_ref[...] + 0.5 * _rms(acc_ref[...], gpost_ref[...])


def _ffn(stage, ins, g_pre, g_post, wg, wu, wd, *, n, tm, tf, n_first):
    d = wg.shape[0]
    dff = wg.shape[1]
    assert n % tm == 0 and n_first % tm == 0 and dff % tf == 0 and len(ins) == _FFN_STAGE_INPUTS[stage]
    first_map, rest_map = _split_maps(n_first // tm)
    tok = lambda w: pl.BlockSpec((tm, w), lambda i, j: (i, 0))
    whole = lambda a: pl.BlockSpec(a.shape, lambda i, j: (0,) * a.ndim)
    if stage == "x":
        in_specs = [tok(d)]
    elif stage == "x2":
        in_specs = [pl.BlockSpec((tm, d), first_map), pl.BlockSpec((tm, d), rest_map)]
    elif stage == "mix":
        in_specs = [tok(ATT_WIDTH), tok(HG_WIDTH), tok(d), whole(ins[3]), whole(ins[4])]
    else:
        pd = ins[1].shape[1]
        in_specs = [tok(d), pl.BlockSpec((tm, pd), first_map), pl.BlockSpec((tm, pd), rest_map),
                    whole(ins[3]), whole(ins[4]), whole(ins[5]), whole(ins[6])]
    tile_bytes = sum(s.block_shape[0] * s.block_shape[1] * a.dtype.itemsize for s, a in zip(in_specs, ins))
    vmem = 2 * (tile_bytes + tm * d * 4 + 3 * d * tf * 2) + tm * d * 10 + 4 * tm * tf * 4 \
        + 6 * FFN_STAGE_ROWS * d * 4 + (4 << 20)
    return pl.pallas_call(
        functools.partial(_ffn_body, stage=stage, n_first_tiles=n_first // tm),
        grid=(n // tm, dff // tf),
        in_specs=in_specs + [
            pl.BlockSpec((1, d), lambda i, j: (0, 0)),
            pl.BlockSpec((1, d), lambda i, j: (0, 0)),
            pl.BlockSpec((d, tf), lambda i, j: (0, j)),
            pl.BlockSpec((d, tf), lambda i, j: (0, j)),
            pl.BlockSpec((tf, d), lambda i, j: (j, 0)),
        ],
        out_specs=pl.BlockSpec((tm, d), lambda i, j: (i, 0)),
        out_shape=jax.ShapeDtypeStruct((n, d), F32),
        scratch_shapes=[pltpu.VMEM((tm, d), F32), pltpu.VMEM((tm, d), BF16), pltpu.VMEM((tm, d), F32)],
        compiler_params=_cparams(("parallel", "arbitrary"), vmem),
        name="ffn_" + stage,
    )(*ins, g_pre, g_post, wg, wu, wd)


_C_Q, _C_K, _C_IQ, _C_IKW, _C_V, _C_HQ, _C_HF, _C_HI, _C_HG = 0, 512, 640, 1152, 1280, 1408, 1920, 2432, 2944
_C_QR, _C_KR, _C_IQR, _C_IKR, _C_END = 3456, 3968, 4096, 4608, 4736


def _proj_body(h_ref, g_ref, w_ref, cos_ref, sin_ref,
               q_ref, k_ref, v_ref, iq_ref, ikw_ref, hq_ref, hf_ref, hi_ref, hg_ref, ikw16_ref):
    u = _rms(h_ref[...], g_ref[...]).astype(BF16)

    def mm(lo, hi):
        return _dot(u, w_ref[:, lo:hi])

    cos = cos_ref[...]
    sin = sin_ref[...]
    cos4 = jnp.concatenate([cos] * 4, axis=1)
    sin4 = jnp.concatenate([sin] * 4, axis=1)
    q_ref[...] = ((mm(_C_Q, _C_K) * cos4 + mm(_C_QR, _C_KR) * sin4) * QK_SCALE).astype(BF16)
    k_ref[...] = mm(_C_K, _C_IQ) * cos + mm(_C_KR, _C_IQR) * sin
    iq_ref[...] = ((mm(_C_IQ, _C_IKW) * cos4 + mm(_C_IQR, _C_IKR) * sin4) * IDX_SCALE).astype(BF16)
    lane = lax.broadcasted_iota(jnp.int32, cos.shape, 1)
    cos_ikw = jnp.where(lane < IDX_DIM, cos, IDX_HEADS ** -0.5)
    ikw = mm(_C_IKW, _C_V) * cos_ikw + mm(_C_IKR, _C_END) * sin
    ikw_ref[...] = ikw
    ikw16_ref[...] = ikw.astype(BF16)
    v_ref[...] = mm(_C_V, _C_HQ)
    hq_ref[...] = mm(_C_HQ, _C_HF)
    hf_ref[...] = mm(_C_HF, _C_HI)
    hi_ref[...] = mm(_C_HI, _C_HG)
    hg_ref[...] = mm(_C_HG, _C_QR)


def _proj(h, g, w_cat, cos_tab, sin_tab, *, tm, tiles_per_seq, n_prompt_tiles):
    n, d = h.shape
    outs = ((512, BF16), (128, F32), (128, F32), (512, BF16), (128, F32),
            (512, F32), (512, F32), (512, F32), (512, F32), (128, BF16))

    def tab_map(i):
        return (jnp.where(i < n_prompt_tiles, i % tiles_per_seq, tiles_per_seq), 0, 0)

    out_bytes = sum(w * jnp.dtype(t).itemsize for w, t in outs)
    vmem = 2 * (tm * d * 4 + d * _C_END * 2 + tm * out_bytes + 2 * tm * LANES * 4) + 6 * tm * 512 * 4 + (4 << 20)
    return pl.pallas_call(
        _proj_body,
        grid=(n // tm,),
        in_specs=[
            pl.BlockSpec((tm, d), lambda i: (i, 0)),
            pl.BlockSpec((1, d), lambda i: (0, 0)),
            pl.BlockSpec((d, _C_END), lambda i: (0, 0)),
            pl.BlockSpec((None, tm, LANES), tab_map),
            pl.BlockSpec((None, tm, LANES), tab_map),
        ],
        out_specs=[pl.BlockSpec((tm, w), lambda i: (i, 0)) for w, _ in outs],
        out_shape=[jax.ShapeDtypeStruct((n, w), t) for w, t in outs],
        compiler_params=_cparams(("parallel",), vmem),
        name="proj",
    )(h, g, w_cat, cos_tab, sin_tab)


_R_K, _R_KR, _R_V, _R_IK, _R_IKR, _R_END = 0, 128, 256, 384, 448, 512


def _proj_t_body(h_ref, g_ref, w_ref, cos_ref, sin_ref, k_ref, v_ref, ik_ref, k16_ref, v16_ref):
    u = _rms(h_ref[...], g_ref[...]).astype(BF16)
    r = _dot_nt(w_ref[...], u)
    cos = cos_ref[...]
    sin = sin_ref[...]
    k = r[_R_K:_R_KR] * cos + r[_R_KR:_R_V] * sin
    v = r[_R_V:_R_IK]
    ik = r[_R_IK:_R_IKR] * cos[0:IDX_DIM] + r[_R_IKR:_R_END] * sin[0:IDX_DIM]
    k_ref[...] = k
    v_ref[...] = v
    ik_ref[...] = ik
    k16_ref[...] = k.astype(BF16)
    v16_ref[...] = v.astype(BF16)


def _proj_t(h, g, w_t, cos_t, sin_t, *, batch, seq, tm):
    d = h.shape[1]
    nt = seq // tm
    outs = ((KV_WIDTH, F32), (KV_WIDTH, F32), (IDX_DIM, F32), (KV_WIDTH, BF16), (KV_WIDTH, BF16))
    vmem = 2 * (tm * d * 4 + _R_END * d * 2 + 2 * LANES * tm * 4 + 3 * LANES * tm * 6) + 4 * _R_END * tm * 4 + (4 << 20)
    return pl.pallas_call(
        _proj_t_body,
        grid=(batch, nt),
        in_specs=[
            pl.BlockSpec((tm, d), lambda b, t: (b * nt + t, 0)),
            pl.BlockSpec((1, d), lambda b, t: (0, 0)),
            pl.BlockSpec((_R_END, d), lambda b, t: (0, 0)),
            pl.BlockSpec((None, LANES, tm), lambda b, t: (t, 0, 0)),
            pl.BlockSpec((None, LANES, tm), lambda b, t: (t, 0, 0)),
        ],
        out_specs=[pl.BlockSpec((None, w, tm), lambda b, t: (b, 0, t)) for w, _ in outs],
        out_shape=[jax.ShapeDtypeStruct((batch, w, seq), t) for w, t in outs],
        compiler_params=_cparams(("parallel", "parallel"), vmem),
        name="proj_t",
    )(h, g, w_t, cos_t, sin_t)


def _tri_incl(n):
    a = lax.broadcasted_iota(jnp.int32, (n, n), 0)
    b = lax.broadcasted_iota(jnp.int32, (n, n), 1)
    return a, b


I16_MIN = -32768
I16_ROWS = 16


def _tree_sum(xs):
    xs = list(xs)
    while len(xs) > 1:
        nxt = [xs[i] + xs[i + 1] for i in range(0, len(xs) - 1, 2)]
        if len(xs) % 2:
            nxt.append(xs[-1])
        xs = nxt
    return xs[0]


def _count_cols(mask, dtype):
    rows = I16_ROWS if dtype == jnp.int16 else SUBLANES
    one = jnp.where(mask, jnp.ones((), dtype), jnp.zeros((), dtype))
    part = _tree_sum([one[c * rows:(c + 1) * rows] for c in range(mask.shape[0] // rows)])
    return jnp.sum(part.astype(F32), axis=0, keepdims=True)


def _search16(x16, k):
    def body(i, t):
        cand = t + lax.shift_left(jnp.int32(1), 15 - i)
        cnt = _count_cols(x16 >= cand.astype(jnp.int16), jnp.int16)
        return jnp.where(cnt >= k, cand, t)

    return lax.fori_loop(0, 16, body, jnp.full(k.shape, I16_MIN, jnp.int32))


def _key_float(key):
    return lax.bitcast_convert_type(key ^ ((key >> 31) & 0x7FFFFFFF), F32)


_ZERO_IMAGES = 1 << 23


def _succ(key):
    nxt = key + 1
    in_zero = jnp.logical_and(nxt >= -_ZERO_IMAGES, nxt < _ZERO_IMAGES)
    return jnp.where(in_zero, jnp.where(key >= -_ZERO_IMAGES, _ZERO_IMAGES, 0), nxt)


BRACKET_STEPS = 40


def _settle_threshold(score, t0, kf):
    def count(key):
        return _count_cols(score >= _key_float(key), jnp.float32)

    def any_set(flag):
        return jnp.max(jnp.where(flag, 1.0, 0.0)) > 0.0

    int_min = jnp.iinfo(jnp.int32).min

    def widen(direction):
        def body(state):
            lo, hi, step, _, it = state
            if direction < 0:
                bad = count(lo) < kf
                down = lo - step
                down = jnp.where(down > lo, int_min, down)
                lo, hi = jnp.where(bad, down, lo), jnp.where(bad, lo, hi)
            else:
                bad = count(hi) >= kf
                lo, hi = jnp.where(bad, hi, lo), jnp.where(bad, hi + step, hi)
            return lo, hi, step * 2, any_set(bad), it + 1
        return body

    def cond(state):
        return jnp.logical_and(state[3], state[4] < BRACKET_STEPS)

    one = jnp.ones_like(t0)
    start = (jnp.bool_(True), jnp.int32(0))
    lo, hi, _, _, _ = lax.while_loop(cond, widen(-1), (t0, _succ(t0), one) + start)
    lo, hi, _, _, _ = lax.while_loop(cond, widen(+1), (lo, hi, one) + start)

    def halve(state):
        lo, hi, _, it = state
        mid = lo + lax.shift_right_logical(hi - lo, 1)
        ok = count(mid) >= kf
        wide = hi > _succ(lo)
        lo = jnp.where(jnp.logical_and(wide, ok), mid, lo)
        hi = jnp.where(jnp.logical_and(wide, jnp.logical_not(ok)), mid, hi)
        return lo, hi, any_set(hi > _succ(lo)), it + 1

    lo, _, _, _ = lax.while_loop(lambda s: jnp.logical_and(s[2], s[3] < BRACKET_STEPS), halve,
                                 (lo, hi, any_set(hi > _succ(lo)), jnp.int32(0)))
    return lo


def _topk_mask_t(score, adm, k):
    l, n = score.shape
    if k >= l:
        return jnp.where(adm, 1.0, 0.0)
    score = jnp.where(score == 0.0, 0.0, score)
    score = jnp.where(adm, score, -jnp.inf)
    bits = lax.bitcast_convert_type(score, jnp.int32)
    key = bits ^ ((bits >> 31) & 0x7FFFFFFF)
    kf = jnp.full((1, n), float(k), F32)
    hi = (key >> 16).astype(jnp.int16)
    t_hi = _search16(hi, kf)
    t_hi16 = t_hi.astype(jnp.int16)
    above = _count_cols(hi > t_hi16, jnp.int16)
    lo = ((key & 0xFFFF) + I16_MIN).astype(jnp.int16)
    lo = jnp.where(hi == t_hi16, lo, jnp.int16(I16_MIN))
    t_lo = _search16(lo, kf - above)
    t = _settle_threshold(score, lax.shift_left(t_hi, 16) | (t_lo - I16_MIN), kf)
    thr = _key_float(t)
    thr = jnp.where(thr != thr, -jnp.inf, thr)
    gt = score >= _key_float(_succ(t))
    inbin = jnp.logical_and(score >= thr, jnp.logical_not(gt))
    need = kf - _count_cols(gt, jnp.float32)

    def top_group(rem):
        pieces = [jnp.where(rem[c * SUBLANES:(c + 1) * SUBLANES] > 0.0, score[c * SUBLANES:(c + 1) * SUBLANES],
                            -jnp.inf) for c in range(l // SUBLANES)]
        while len(pieces) > 1:
            pieces = [jnp.maximum(pieces[i], pieces[i + 1]) for i in range(0, len(pieces) - 1, 2)] + \
                ([pieces[-1]] if len(pieces) % 2 else [])
        grp = jnp.logical_and(rem > 0.0, score == jnp.max(pieces[0], axis=0, keepdims=True))
        return jnp.where(grp, 1.0, 0.0), _count_cols(grp, jnp.float32)

    def fits(cnt, need_):
        return jnp.logical_and(cnt > 0.0, cnt < need_)

    def peel(state):
        rem, whole, need_, grp, cnt, it = state
        f = fits(cnt, need_)
        take = jnp.where(f, grp, 0.0)
        rem, whole, need_ = rem - take, whole + take, jnp.where(f, need_ - cnt, need_)
        grp, cnt = top_group(rem)
        return rem, whole, need_, grp, cnt, it + 1

    rem0 = jnp.where(inbin, 1.0, 0.0)
    grp0, cnt0 = top_group(rem0)
    _, whole, need, eq, _, _ = lax.while_loop(
        lambda s: jnp.logical_and(jnp.max(jnp.where(fits(s[4], s[2]), 1.0, 0.0)) > 0.0, s[5] < BRACKET_STEPS),
        peel, (rem0, jnp.zeros_like(rem0), need, grp0, cnt0, jnp.int32(0)))
    gt = jnp.logical_or(gt, whole > 0.0)
    eq = eq > 0.0
    a, b = _tri_incl(LANES)
    tril = jnp.where(b <= a, 1.0, 0.0).astype(BF16)
    carry = jnp.zeros((1, n), F32)
    sel = []
    for c in range(l // LANES):
        rows = slice(c * LANES, (c + 1) * LANES)
        pref = _dot(tril, jnp.where(eq[rows], 1.0, 0.0).astype(BF16)) + carry
        carry = pref[LANES - 1:LANES]
        take = jnp.logical_or(gt[rows], jnp.logical_and(eq[rows], pref <= need))
        sel.append(jnp.where(jnp.logical_and(take, adm[rows]), 1.0, 0.0))
    return jnp.concatenate(sel, axis=0) if len(sel) > 1 else sel[0]


def _dsa_prompt_tile(q_ref, iq_ref, iw_ref, kt_ref, vt_ref, ik_ref, o_ref, *, q0, l, topk):
    qb = q_ref.shape[0]
    iq_t = iq_ref[...].astype(F32).T.astype(BF16)
    w_t = iw_ref[...].T[IDX_DIM:IDX_DIM + IDX_HEADS]
    ik = ik_ref[0:l, 0:IDX_DIM]
    score = jnp.zeros((l, qb), F32)
    for h in range(IDX_HEADS):
        lg = _dot(ik, iq_t[h * IDX_DIM:(h + 1) * IDX_DIM])
        score = score + w_t[h:h + 1] * jnp.maximum(lg, 0.0)

    kpos = lax.broadcasted_iota(jnp.int32, (l, qb), 0)
    qpos = q0 + lax.broadcasted_iota(jnp.int32, (l, qb), 1)
    sel_t = _topk_mask_t(score, kpos <= qpos, topk)
    bias = ((sel_t - 1.0) * (-NEG_BIG)).T

    q = q_ref[...]
    group = N_HEADS // KV_HEADS
    for h in range(N_HEADS):
        g = h // group
        s = _dot(q[:, h * HEAD_DIM:(h + 1) * HEAD_DIM], kt_ref[g * HEAD_DIM:(g + 1) * HEAD_DIM, 0:l]) + bias
        p = jnp.exp(s - jnp.max(s, axis=-1, keepdims=True))
        den = jnp.sum(p, axis=-1, keepdims=True)
        o = _dot_nt(p.astype(BF16), vt_ref[g * HEAD_DIM:(g + 1) * HEAD_DIM, 0:l])
        o_ref[:, h * HEAD_DIM:(h + 1) * HEAD_DIM] = o / den


def _dsa_prompt_body(q_ref, iq_ref, iw_ref, kt_ref, vt_ref, ik_ref, o_ref, *, topk, n_len):
    qi = pl.program_id(1)
    nq = pl.num_programs(1)
    is_fill = pl.program_id(0) == pl.num_programs(0) - 1
    qb = q_ref.shape[0]
    seq = kt_ref.shape[1]
    per = nq // n_len
    for v in range(n_len):
        l = (v + 1) * per * qb

        @pl.when(jnp.logical_and(jnp.logical_not(is_fill), qi // per == v))
        def _(l=l):
            _dsa_prompt_tile(q_ref, iq_ref, iw_ref, kt_ref, vt_ref, ik_ref, o_ref,
                             q0=qi * qb, l=min(l, seq), topk=topk)

    @pl.when(is_fill)
    def _():
        o_ref[...] = jnp.zeros_like(o_ref)


def _dsa_prompt(q16, iq16, ikw, ikw16, kt16, vt16, *, batch, seq, qb, n_len):
    topk = min(TOPK_MAX, seq // 4)
    nq = seq // qb
    n_len = n_len if nq % n_len == 0 else 1
    last_blk = q16.shape[0] // qb - 1
    assert q16.shape[0] % qb == 0 and last_blk < (batch + 1) * nq
    tok_map = lambda bi, qi: (jnp.minimum(bi * nq + qi, last_blk), 0)
    seq_map = lambda bi, qi: (jnp.minimum(bi, batch - 1), 0, 0)
    vmem = 2 * (2 * qb * 512 * 2 + qb * LANES * 4 + qb * 512 * 4 + seq * (2 * KV_WIDTH + LANES) * 2) \
        + 16 * qb * seq * 4 + (4 << 20)
    return pl.pallas_call(
        functools.partial(_dsa_prompt_body, topk=topk, n_len=n_len),
        grid=(batch + 1, nq),
        in_specs=[
            pl.BlockSpec((qb, ATT_WIDTH), tok_map),
            pl.BlockSpec((qb, IDX_HEADS * IDX_DIM), tok_map),
            pl.BlockSpec((qb, LANES), tok_map),
            pl.BlockSpec((None, KV_WIDTH, seq), seq_map),
            pl.BlockSpec((None, KV_WIDTH, seq), seq_map),
            pl.BlockSpec((seq, LANES), lambda bi, qi: (jnp.minimum(bi, batch - 1), 0)),
        ],
        out_specs=pl.BlockSpec((qb, ATT_WIDTH), tok_map),
        out_shape=jax.ShapeDtypeStruct((q16.shape[0], ATT_WIDTH), F32),
        compiler_params=_cparams(("arbitrary", "arbitrary"), vmem),
        name="dsa_prompt",
    )(q16, iq16, ikw, kt16, vt16, ikw16)


def _dsa_sample_body(pt_ref, q_ref, iq_ref, ikw_ref, kn_ref, vn_ref, ck_hbm, cv_hbm, ci_hbm, att_hbm, o_ref,
                     kbuf, vbuf, ibuf, sems, *, layer, n_pages, page, tq, nb, topk):
    del att_hbm
    g = pl.program_id(0)
    slot = g % 2
    past = n_pages * page
    l = past + LANES
    group = N_HEADS // KV_HEADS

    def page_copies(step, s):
        out = []
        for bj in range(nb):
            for p in range(n_pages):
                idx = pt_ref[(step * nb + bj) * n_pages + p]
                cols = pl.ds(p * page, page)
                out.append(pltpu.make_async_copy(ck_hbm.at[layer, idx], kbuf.at[s, bj, :, cols], sems.at[s, 0]))
                out.append(pltpu.make_async_copy(cv_hbm.at[layer, idx], vbuf.at[s, bj, :, cols], sems.at[s, 1]))
                out.append(pltpu.make_async_copy(ci_hbm.at[layer, idx], ibuf.at[s, bj, :, cols], sems.at[s, 2]))
        return out

    @pl.when(g == 0)
    def _():
        for c in page_copies(0, 0):
            c.start()

    @pl.when(g + 1 < pl.num_programs(0))
    def _():
        for c in page_copies(g + 1, 1 - slot):
            c.start()

    for c in page_copies(g, slot):
        c.wait()

    def pad_rows(x):
        return jnp.concatenate([x, jnp.zeros((LANES - tq, x.shape[1]), x.dtype)], axis=0)

    iq_all = iq_ref[...].astype(F32)
    q_all = q_ref[...].astype(F32)
    ikw_all = ikw_ref[...]

    scores = []
    for bj in range(nb):
        rows = slice(bj * tq, (bj + 1) * tq)
        iq = iq_all[rows]
        iq_hm = jnp.concatenate([iq[:, h * IDX_DIM:(h + 1) * IDX_DIM] for h in range(IDX_HEADS)],
                                axis=0).astype(BF16)
        ik_new = pad_rows(ikw_all[rows, 0:IDX_DIM]).astype(BF16)
        lg = jnp.concatenate([_dot(iq_hm, ibuf[slot, bj].astype(BF16)), _dot_nt(iq_hm, ik_new)], axis=1)
        lg = jnp.maximum(lg, 0.0)
        w = ikw_all[rows, IDX_DIM:IDX_DIM + IDX_HEADS]
        sc = jnp.zeros((tq, l), F32)
        for h in range(IDX_HEADS):
            sc = sc + w[:, h:h + 1] * lg[h * tq:(h + 1) * tq, :]
        scores.append(sc)
    nq = nb * tq
    score_t = jnp.concatenate(scores + [jnp.zeros((LANES - nq, l), F32)], axis=0).T
    kidx = lax.broadcasted_iota(jnp.int32, (l, LANES), 0)
    qidx = lax.broadcasted_iota(jnp.int32, (l, LANES), 1)
    adm = kidx <= past + qidx % tq
    adm = jnp.logical_and(adm, kidx < past + tq)
    adm = jnp.logical_and(adm, qidx < nq)
    sel_t = _topk_mask_t(score_t, adm, topk)
    bias = ((sel_t - 1.0) * (-NEG_BIG)).T

    for bj in range(nb):
        rows = slice(bj * tq, (bj + 1) * tq)
        q = q_all[rows]
        bias_g = jnp.concatenate([bias[rows]] * group, axis=0)
        k_new = pad_rows(kn_ref[rows, :]).astype(BF16)
        v_new = pad_rows(vn_ref[rows, :]).astype(BF16)
        for hk in range(KV_HEADS):
            fs = slice(hk * HEAD_DIM, (hk + 1) * HEAD_DIM)
            qg = jnp.concatenate([q[:, h * HEAD_DIM:(h + 1) * HEAD_DIM]
                                  for h in range(hk * group, (hk + 1) * group)], axis=0).astype(BF16)
            kt = kbuf[slot, bj, fs, :].astype(BF16)
            vt = vbuf[slot, bj, fs, :].astype(BF16)
            s = jnp.concatenate([_dot(qg, kt), _dot_nt(qg, k_new[:, fs])], axis=1) + bias_g
            p = jnp.exp(s - jnp.max(s, axis=-1, keepdims=True))
            den = jnp.sum(p, axis=-1, keepdims=True)
            p = p.astype(BF16)
            o = (_dot_nt(p[:, 0:past], vt) + _dot(p[:, past:l], v_new[:, fs])) / den
            for j in range(group):
                h = hk * group + j
                o_ref[rows, h * HEAD_DIM:(h + 1) * HEAD_DIM] = o[j * tq:(j + 1) * tq, :]


def _dsa_sample(page_table, q16, iq16, ikw, k, v, cache_kt, cache_vt, cache_ikt, att, *, layer, row0, batch, tq, nb):
    n_pages = page_table.shape[1]
    page = cache_kt.shape[3]
    past = n_pages * page
    topk = min(TOPK_MAX, (past + tq) // 4)
    blk0 = row0 // (nb * tq)
    pt = page_table.reshape(-1)

    def tok(w):
        return pl.BlockSpec((nb * tq, w), lambda gi, pt_ref: (blk0 + gi, 0))

    hbm = pl.BlockSpec(memory_space=pl.ANY)
    in_specs = [tok(ATT_WIDTH), tok(IDX_HEADS * IDX_DIM), tok(LANES), tok(KV_WIDTH), tok(KV_WIDTH),
                hbm, hbm, hbm, hbm]
    buf_bytes = 2 * nb * past * (2 * KV_WIDTH + IDX_DIM) * 4
    vmem = buf_bytes + 16 * nb * (past + LANES) * LANES * 4 + (8 << 20)
    return pl.pallas_call(
        functools.partial(_dsa_sample_body, layer=layer, n_pages=n_pages, page=page, tq=tq, nb=nb, topk=topk),
        grid_spec=pltpu.PrefetchScalarGridSpec(
            num_scalar_prefetch=1,
            grid=(batch // nb,),
            in_specs=in_specs,
            out_specs=tok(ATT_WIDTH),
            scratch_shapes=[pltpu.VMEM((2, nb, KV_WIDTH, past), F32),
                            pltpu.VMEM((2, nb, KV_WIDTH, past), F32),
                            pltpu.VMEM((2, nb, IDX_DIM, past), F32),
                            pltpu.SemaphoreType.DMA((2, 3))],
        ),
        out_shape=jax.ShapeDtypeStruct(att.shape, att.dtype),
        input_output_aliases={len(in_specs): 0},
        compiler_params=_cparams(("arbitrary",), vmem),
        name="dsa_sample",
    )(pt, q16, iq16, ikw, k, v, cache_kt, cache_vt, cache_ikt, att)


def _hgrn_levels(c):
    out = []
    n = c // 2
    while n >= 1:
        out.append(n)
        n //= 2
    return out


def _group_row(x, j, g):
    c, n = x.shape
    x3 = x.reshape(c // g, g, n)
    return jnp.broadcast_to(x3[:, j:j + 1, :], (c // g, g, n)).reshape(c, n)


def _boundary_rows(b, n, row):
    g = 2 * n
    if g >= SUBLANES:
        return _group_row(b, n - 1, g)
    out = _group_row(b, n - 1, SUBLANES)
    for j in range(1, SUBLANES // g):
        out = jnp.where((row % SUBLANES) // g == j, _group_row(b, j * g + n - 1, SUBLANES), out)
    return out


def _hgrn_chunk(xq, xf, xi, lb, state_t):
    c, dk = xq.shape
    f = lb + (1.0 - lb) * jax.nn.sigmoid(xf)
    logf = jnp.log(f)
    kk = 1.0 - f
    qf = _silu(xq) * (HG_KDIM ** -0.5)
    ta, tb = _tri_incl(c)
    row = lax.broadcasted_iota(jnp.int32, (c, 1), 0)
    b = logf
    step = 1
    while step < c:
        b = b + jnp.where(row >= step, pltpu.roll(b, step, 0), 0.0)
        step *= 2

    o = _dot_nt((qf * jnp.exp(b)).astype(BF16), state_t.astype(BF16))
    o = o + jnp.sum(qf * kk, axis=-1, keepdims=True) * xi
    xi16 = xi.astype(BF16)
    att = jnp.zeros((c, c), F32)
    for n in _hgrn_levels(c):
        upper = (row % (2 * n)) >= n
        e = jnp.exp(-jnp.abs(b - _boundary_rows(b, n, row)))
        m = (jnp.where(upper, qf, kk) * e).astype(BF16)
        pair = jnp.logical_and(ta // (2 * n) == tb // (2 * n),
                               jnp.logical_and((ta % (2 * n)) >= n, (tb % (2 * n)) < n))
        att = att + jnp.where(pair, _dot_nt(m, m), 0.0)
    o = o + _dot(att.astype(BF16), xi16)

    blast = b[c - 1:c]
    kd = (kk * jnp.exp(blast - b)).astype(BF16)
    new_state_t = state_t * jnp.exp(blast) + _dot_tn(xi16, kd)
    return o, new_state_t


def _hgrn_gate(o, gnorm, xg):
    return _rms(o, gnorm) * _silu(xg)


def _hgrn_prompt_body(hq_ref, hf_ref, hi_ref, hg_ref, lb_ref, gn_ref, o_ref, s_ref, st_ref, *, chunk):
    gi = pl.program_id(1)
    is_fill = pl.program_id(0) == pl.num_programs(0) - 1

    @pl.when(is_fill)
    def _():
        o_ref[...] = jnp.zeros_like(o_ref)

    @pl.when(jnp.logical_not(is_fill))
    def _():
        @pl.when(gi == 0)
        def _():
            st_ref[...] = jnp.zeros_like(st_ref)

        n_chunks = hq_ref.shape[0] // chunk
        gn = gn_ref[...]

        def step(ci, carry):
            rows = pl.ds(pl.multiple_of(ci * chunk, chunk), chunk)
            for h in range(HG_HEADS):
                cols = slice(h * HG_KDIM, (h + 1) * HG_KDIM)
                o, st = _hgrn_chunk(hq_ref[rows, cols], hf_ref[rows, cols], hi_ref[rows, cols],
                                    lb_ref[:, cols], st_ref[h])
                st_ref[h] = st
                o_ref[rows, cols] = _hgrn_gate(o, gn, hg_ref[rows, cols])
            return carry

        lax.fori_loop(0, n_chunks, step, 0, unroll=2)

        @pl.when(gi == pl.num_programs(1) - 1)
        def _():
            for h in range(HG_HEADS):
                s_ref[h] = st_ref[h].T


def _hgrn_prompt(hq, hf, hi, hg, lb, gnorm, *, batch, seq, rows_per_step):
    ng = seq // rows_per_step
    chunk = min(HG_CHUNK, seq)
    last_blk = hq.shape[0] // rows_per_step - 1
    assert hq.shape[0] % rows_per_step == 0 and last_blk < (batch + 1) * ng

    def tok():
        return pl.BlockSpec((rows_per_step, HG_WIDTH), lambda bi, gi: (jnp.minimum(bi * ng + gi, last_blk), 0))

    vmem = 2 * 5 * rows_per_step * HG_WIDTH * 4 + 3 * HG_HEADS * HG_KDIM * HG_VDIM * 4 + (8 << 20)
    return pl.pallas_call(
        functools.partial(_hgrn_prompt_body, chunk=chunk),
        grid=(batch + 1, ng),
        in_specs=[tok(), tok(), tok(), tok(),
                  pl.BlockSpec((1, HG_WIDTH), lambda bi, gi: (0, 0)),
                  pl.BlockSpec((1, HG_VDIM), lambda bi, gi: (0, 0))],
        out_specs=[tok(), pl.BlockSpec((None, HG_HEADS, HG_KDIM, HG_VDIM),
                                       lambda bi, gi: (jnp.minimum(bi, batch - 1), 0, 0, 0))],
        out_shape=[jax.ShapeDtypeStruct((hq.shape[0], HG_WIDTH), F32),
                   jax.ShapeDtypeStruct((batch, HG_HEADS, HG_KDIM, HG_VDIM), F32)],
        scratch_shapes=[pltpu.VMEM((HG_HEADS, HG_KDIM, HG_VDIM), F32)],
        compiler_params=_cparams(("arbitrary", "arbitrary"), vmem),
        name="hgrn_prompt",
    )(hq, hf, hi, hg, lb, gnorm)


def _hgrn_sample_body(hq_ref, hf_ref, hi_ref, hg_ref, lb_ref, gn_ref, s0_ref, base_ref, o_ref, s_ref, *, tq, nb):
    del base_ref
    gn = gn_ref[...]

    def step(bi, carry):
        rows = pl.ds(pl.multiple_of(bi * tq, tq), tq)
        for h in range(HG_HEADS):
            cols = slice(h * HG_KDIM, (h + 1) * HG_KDIM)
            o, st = _hgrn_chunk(hq_ref[rows, cols], hf_ref[rows, cols], hi_ref[rows, cols],
                                lb_ref[:, cols], s0_ref[bi, h].T)
            s_ref[bi, h] = st.T
            o_ref[rows, cols] = _hgrn_gate(o, gn, hg_ref[rows, cols])
        return carry

    lax.fori_loop(0, nb, step, 0, unroll=2)


def _hgrn_sample(hq, hf, hi, hg, lb, gnorm, state, ohg, *, layer, row0, batch, tq, nb):
    assert tq <= HG_CHUNK
    blk0 = row0 // (nb * tq)

    def tok():
        return pl.BlockSpec((nb * tq, HG_WIDTH), lambda gi: (blk0 + gi, 0))

    st_blk = (nb, HG_HEADS, HG_KDIM, HG_VDIM)
    vmem = 2 * (5 * nb * tq * HG_WIDTH * 4 + 2 * nb * HG_HEADS * HG_KDIM * HG_VDIM * 4) + (8 << 20)
    return pl.pallas_call(
        functools.partial(_hgrn_sample_body, tq=tq, nb=nb),
        grid=(batch // nb,),
        in_specs=[tok(), tok(), tok(), tok(),
                  pl.BlockSpec((1, HG_WIDTH), lambda gi: (0, 0)),
                  pl.BlockSpec((1, HG_VDIM), lambda gi: (0, 0)),
                  pl.BlockSpec((None,) + st_blk, lambda gi: (layer, gi, 0, 0, 0)),
                  pl.BlockSpec(memory_space=pl.ANY)],
        out_specs=[tok(), pl.BlockSpec(st_blk, lambda gi: (gi, 0, 0, 0))],
        input_output_aliases={7: 0},
        out_shape=[jax.ShapeDtypeStruct(ohg.shape, ohg.dtype),
                   jax.ShapeDtypeStruct((batch, HG_HEADS, HG_KDIM, HG_VDIM), F32)],
        compiler_params=_cparams(("parallel",), vmem),
        name="hgrn_sample",
    )(hq, hf, hi, hg, lb, gnorm, state, ohg)


def _ple_body(h_ref, pp_ref, ps_ref, gpre_ref, gpost_ref, wgate_ref, wproj_ref, yp_ref, ys_ref, *, n_first_tiles):
    y = _ple_rows(h_ref[...], _pick(pp_ref, ps_ref, n_first_tiles), gpre_ref[...], gpost_ref[...],
                  wgate_ref[...], wproj_ref[...])
    is_first = pl.program_id(0) < n_first_tiles

    @pl.when(is_first)
    def _():
        yp_ref[...] = y

    @pl.when(jnp.logical_not(is_first))
    def _():
        ys_ref[...] = y


def _ple(h, p_first, p_rest, g_pre, g_post, w_gate, w_proj, *, tm):
    n, d = h.shape
    pd = p_first.shape[1]
    n_first = p_first.shape[0] // tm
    assert p_first.shape[0] % tm == 0 and p_rest.shape[0] % tm == 0
    first_map, rest_map = _split_maps(n_first)
    vmem = 2 * (3 * tm * d * 4 + 2 * tm * pd * 4 + d * d * 2 + pd * d * 2) + 3 * tm * d * 4 + (4 << 20)
    return pl.pallas_call(
        functools.partial(_ple_body, n_first_tiles=n_first),
        grid=(n // tm,),
        in_specs=[pl.BlockSpec((tm, d), lambda i: (i, 0)),
                  pl.BlockSpec((tm, pd), first_map),
                  pl.BlockSpec((tm, pd), rest_map),
                  pl.BlockSpec((1, d), lambda i: (0, 0)),
                  pl.BlockSpec((1, d), lambda i: (0, 0)),
                  pl.BlockSpec((d, d), lambda i: (0, 0)),
                  pl.BlockSpec((pd, d), lambda i: (0, 0))],
        out_specs=[pl.BlockSpec((tm, d), first_map), pl.BlockSpec((tm, d), rest_map)],
        out_shape=[jax.ShapeDtypeStruct((p_first.shape[0], d), F32), jax.ShapeDtypeStruct((p_rest.shape[0], d), F32)],
        compiler_params=_cparams(("arbitrary",), vmem),
        name="ple",
    )(h, p_first, p_rest, g_pre, g_post, w_gate, w_proj)


def _rot_cols(w, heads):
    d = w.shape[0]
    w4 = w.reshape(d, heads, 2, HEAD_DIM // 2)
    return jnp.concatenate([-w4[:, :, 1:2], w4[:, :, 0:1]], axis=2).reshape(d, heads * HEAD_DIM)


def _pack_w_in(w):
    d = w.shape[0]
    widths = (ATT_WIDTH, KV_WIDTH, KV_WIDTH, IDX_HEADS * IDX_DIM, IDX_DIM, IDX_HEADS,
              HG_HEADS * HG_KDIM, HG_HEADS * HG_KDIM, HG_WIDTH, HG_WIDTH)
    parts = []
    acc = 0
    for wd in widths:
        parts.append(w[:, acc:acc + wd])
        acc += wd
    q, k, v, iq, ik, iw, hq, hf, hi, hg = parts
    z = lambda n: jnp.zeros((d, n), w.dtype)
    ikw = jnp.concatenate([ik, iw, z(LANES - IDX_DIM - IDX_HEADS)], axis=1)
    ikr = jnp.concatenate([_rot_cols(ik, 1), z(LANES - IDX_DIM)], axis=1)
    cat = jnp.concatenate([q, k, iq, ikw, v, hq, hf, hi, hg,
                           _rot_cols(q, N_HEADS), _rot_cols(k, KV_HEADS), _rot_cols(iq, IDX_HEADS), ikr], axis=1)
    assert cat.shape[1] == _C_END
    w_t = jnp.concatenate([k, _rot_cols(k, KV_HEADS), v, ik, _rot_cols(ik, 1)], axis=1).T
    assert w_t.shape[0] == _R_END
    return cat.astype(BF16), w_t.astype(BF16)


def _rope_tables(seq, past_len, tq, tm):
    half = HEAD_DIM // 2
    inv_freq = ROPE_THETA ** (-jnp.arange(half, dtype=F32) / half)
    pos_p = jnp.arange(seq, dtype=jnp.int32)
    pos_s = past_len + (jnp.arange(tm, dtype=jnp.int32) % tq)
    pos = jnp.concatenate([pos_p, pos_s]).astype(F32)
    ang = pos[:, None] * inv_freq[None, :]
    reps = LANES // half
    cos = jnp.tile(jnp.cos(ang), (1, reps)).reshape(seq // tm + 1, tm, LANES)
    sin = jnp.tile(jnp.sin(ang), (1, reps)).reshape(seq // tm + 1, tm, LANES)
    nt = seq // tm
    cos_t = jnp.swapaxes(cos[:nt], 1, 2)
    sin_t = jnp.swapaxes(sin[:nt], 1, 2)
    return cos, sin, cos_t, sin_t


def _lower_bounds(logits):
    sm = jax.nn.softmax(logits.astype(F32), axis=0)
    return jnp.cumsum(sm, axis=0) - sm[0:1]


def _feature_major(cache, width):
    depth, pool, page = cache.shape[:3]
    c = cache.reshape(depth, pool, page, width)
    return jnp.swapaxes(c, 2, 3)


TM_DENSE = 1024
TM_PROJ = 512
TM_FFN = 1024
TF_FFN = 256
Q_BLOCK = 128
N_KEY_LEN = 8
DSA_SEQS = 4
HG_ROWS = 512
HG_SEQS = 8


def kernel(x_prompt, x_sample, p_prompt, p_sample, cache_k, cache_v, cache_idx_k, state_hgrn, page_table,
           n_f1_pre, n_f1_post, w_f1_gate, w_f1_up, w_f1_down,
           n_mix_pre, n_mix_post, w_in, hg_lb_logits, hg_norm, w_out,
           n_f2_pre, n_f2_post, w_f2_gate, w_f2_up, w_f2_down,
           n_ple_pre, n_ple_post, w_ple_proj, w_ple_gate):
    bp, tp, d = x_prompt.shape
    bs, ts, _ = x_sample.shape
    depth = w_in.shape[0]
    n_p, n_s = bp * tp, bs * ts
    page = cache_k.shape[2]
    past_len = page_table.shape[1] * page

    ckt = _feature_major(cache_k, KV_WIDTH)
    cvt = _feature_major(cache_v, KV_WIDTH)
    cikt = _feature_major(cache_idx_k, IDX_DIM)
    cos_tab, sin_tab, cos_t, sin_t = _rope_tables(tp, past_len, ts, TM_PROJ)
    lbs = _lower_bounds(hg_lb_logits)
    row = lambda a: a.reshape(1, -1)
    bf = lambda a: a.astype(BF16)

    outs = {name: [] for name in ("kp", "vp", "ikp", "sp", "ks", "vs", "iks", "ss")}
    ffn_kw = dict(n=n_p + n_s, tm=TM_FFN, tf=TF_FFN, n_first=n_p)
    ple_args = lambda i: (p_prompt[i].reshape(n_p, -1), p_sample[i].reshape(n_s, -1),
                          row(n_ple_pre[i]), row(n_ple_post[i]), bf(w_ple_gate[i]), bf(w_ple_proj[i]))
    h3 = None
    for i in range(depth):
        stage, ins = ("x2", (x_prompt.reshape(n_p, d), x_sample.reshape(n_s, d))) if i == 0 else \
            ("ple", (h3,) + ple_args(i - 1))
        h1 = _ffn(stage, ins, row(n_f1_pre[i]), row(n_f1_post[i]), bf(w_f1_gate[i]), bf(w_f1_up[i]),
                  bf(w_f1_down[i]), **ffn_kw)
        w_cat, w_t = _pack_w_in(w_in[i])
        g_mix = row(n_mix_pre[i])
        q16, k, v, iq16, ikw, hq, hf, hi, hg, ikw16 = _proj(
            h1, g_mix, w_cat, cos_tab, sin_tab,
            tm=TM_PROJ, tiles_per_seq=tp // TM_PROJ, n_prompt_tiles=n_p // TM_PROJ)
        kt, vt, ikt, kt16, vt16 = _proj_t(h1, g_mix, w_t, cos_t, sin_t, batch=bp, seq=tp, tm=TM_PROJ)

        att = _dsa_prompt(q16, iq16, ikw, ikw16, kt16, vt16, batch=bp, seq=tp, qb=min(Q_BLOCK, tp),
                          n_len=N_KEY_LEN)
        att = _dsa_sample(page_table, q16, iq16, ikw, k, v, ckt, cvt, cikt, att,
                          layer=i, row0=n_p, batch=bs, tq=ts, nb=DSA_SEQS)
        lb = row(lbs[i])
        gn = row(hg_norm[i])
        ohg, s_p = _hgrn_prompt(hq, hf, hi, hg, lb, gn, batch=bp, seq=tp, rows_per_step=HG_ROWS)
        ohg, s_s = _hgrn_sample(hq, hf, hi, hg, lb, gn, state_hgrn, ohg,
                                layer=i, row0=n_p, batch=bs, tq=ts, nb=HG_SEQS)

        h3 = _ffn("mix", (att, ohg, h1, row(n_mix_post[i]), bf(w_out[i])),
                  row(n_f2_pre[i]), row(n_f2_post[i]), bf(w_f2_gate[i]), bf(w_f2_up[i]), bf(w_f2_down[i]), **ffn_kw)

        outs["kp"].append(jnp.transpose(kt.reshape(bp, KV_HEADS, HEAD_DIM, tp), (0, 3, 1, 2)))
        outs["vp"].append(jnp.transpose(vt.reshape(bp, KV_HEADS, HEAD_DIM, tp), (0, 3, 1, 2)))
        outs["ikp"].append(jnp.swapaxes(ikt, 1, 2))
        outs["sp"].append(s_p)
        outs["ks"].append(k[n_p:].reshape(bs, ts, KV_HEADS, HEAD_DIM))
        outs["vs"].append(v[n_p:].reshape(bs, ts, KV_HEADS, HEAD_DIM))
        outs["iks"].append(ikw[n_p:, :IDX_DIM].reshape(bs, ts, IDX_DIM))
        outs["ss"].append(s_s.astype(state_hgrn.dtype))

    y_p, y_s = _ple(h3, *ple_args(depth - 1), tm=TM_DENSE)
    st = lambda name: jnp.stack(outs[name])
    return (y_p.reshape(bp, tp, d), y_s.reshape(bs, ts, d),
            st("kp"), st("vp"), st("ikp"), st("sp"), st("ks"), st("vs"), st("iks"), st("ss"))
```

```python
import functools

import jax
import jax.numpy as jnp
from jax import lax
from jax.experimental import pallas as pl
from jax.experimental.pallas import tpu as pltpu

HEAD_DIM = 64
N_HEADS = 8
KV_HEADS = 2
IDX_HEADS = 8
IDX_DIM = 64
TOPK_MAX = 256
HG_KDIM = 128
HG_VDIM = 128
HG_HEADS = 4
HG_CHUNK = 64
ROPE_THETA = 10000.0
EPS = 1e-6
ATT_WIDTH = N_HEADS * HEAD_DIM
KV_WIDTH = KV_HEADS * HEAD_DIM
HG_WIDTH = HG_HEADS * HG_VDIM
QK_SCALE = HEAD_DIM ** -0.5
IDX_SCALE = IDX_DIM ** -0.5

LANES = 128
SUBLANES = 8
VMEM_BUDGET_BYTES = 56 * 1024 * 1024

NEG_BIG = -1e30

F32 = jnp.float32
BF16 = jnp.bfloat16


def _cparams(sem, vmem_bytes):
    return pltpu.CompilerParams(dimension_semantics=sem,
                                vmem_limit_bytes=int(min(max(vmem_bytes, 16 << 20), VMEM_BUDGET_BYTES)))


def _rms(x, g):
    return x * lax.rsqrt(jnp.mean(x * x, axis=-1, keepdims=True) + EPS) * g


def _silu(x):
    return x * jax.nn.sigmoid(x)


def _dot(a, b):
    return jnp.dot(a, b, preferred_element_type=F32)


def _dot_nt(a, b):
    return lax.dot_general(a, b, (((1,), (1,)), ((), ())), preferred_element_type=F32)


def _dot_tn(a, b):
    return lax.dot_general(a, b, (((0,), (0,)), ((), ())), preferred_element_type=F32)


def _split_maps(n_first_tiles):
    return (lambda i, *_: (jnp.minimum(i, n_first_tiles - 1), 0),
            lambda i, *_: (jnp.maximum(i - n_first_tiles, 0), 0))


def _pick(first_ref, rest_ref, n_first_tiles):
    return jnp.where(pl.program_id(0) < n_first_tiles, first_ref[...], rest_ref[...])


def _mix_rows(att, ohg, h, g, w_ref):
    mix = _dot(att.astype(BF16), w_ref[0:ATT_WIDTH, :]) + \
        _dot(ohg.astype(BF16), w_ref[ATT_WIDTH:ATT_WIDTH + HG_WIDTH, :])
    return h + _rms(mix, g)


def _ple_rows(h, p, g_pre, g_post, w_gate, w_proj):
    gate = jax.nn.sigmoid(_dot(_rms(h, g_pre).astype(BF16), w_gate))
    return h + _rms(_dot(p.astype(BF16), w_proj) * gate, g_post)


_FFN_STAGE_INPUTS = {"x": 1, "x2": 2, "mix": 5, "ple": 7}
FFN_STAGE_ROWS = 256


def _ffn_body(*refs, stage, n_first_tiles):
    n_in = _FFN_STAGE_INPUTS[stage]
    ins = refs[:n_in]
    gpre_ref, gpost_ref, wg_ref, wu_ref, wd_ref, o_ref, x_ref, xn_ref, acc_ref = refs[n_in:]
    j = pl.program_id(1)
    is_first = pl.program_id(0) < n_first_tiles

    def stage_rows(r):
        if stage == "x":
            return ins[0][r, :]
        if stage == "x2":
            return jnp.where(is_first, ins[0][r, :], ins[1][r, :])
        if stage == "mix":
            att, ohg, h, g, w = ins
            return _mix_rows(att[r, :], ohg[r, :], h[r, :], g[...], w)
        h, pp, ps, g1, g2, wgate, wproj = ins
        return _ple_rows(h[r, :], jnp.where(is_first, pp[r, :], ps[r, :]), g1[...], g2[...], wgate[...], wproj[...])

    @pl.when(j == 0)
    def _():
        tm = x_ref.shape[0]
        rc = min(FFN_STAGE_ROWS, tm)
        for r0 in range(0, tm, rc):
            r = slice(r0, r0 + rc)
            x = stage_rows(r)
            x_ref[r, :] = x
            xn_ref[r, :] = _rms(x, gpre_ref[...]).astype(BF16)
        acc_ref[...] = jnp.zeros_like(acc_ref)

    xn = xn_ref[...]
    a = _dot(xn, wg_ref[...])
    b = _dot(xn, wu_ref[...])
    acc_ref[...] += _dot((_silu(a) * b).astype(BF16), wd_ref[...])

    @pl.when(j == pl.num_programs(1) - 1)
    def _():
        o_ref[...] = x_ref[...] + 0.5 * _rms(acc_ref[...], gpost_ref[...])


def _ffn(stage, ins, g_pre, g_post, wg, wu, wd, *, n, tm, tf, n_first):
    d = wg.shape[0]
    dff = wg.shape[1]
    assert n % tm == 0 and n_first % tm == 0 and dff % tf == 0 and len(ins) == _FFN_STAGE_INPUTS[stage]
    first_map, rest_map = _split_maps(n_first // tm)
    tok = lambda w: pl.BlockSpec((tm, w), lambda i, j: (i, 0))
    whole = lambda a: pl.BlockSpec(a.shape, lambda i, j: (0,) * a.ndim)
    if stage == "x":
        in_specs = [tok(d)]
    elif stage == "x2":
        in_specs = [pl.BlockSpec((tm, d), first_map), pl.BlockSpec((tm, d), rest_map)]
    elif stage == "mix":
        in_specs = [tok(ATT_WIDTH), tok(HG_WIDTH), tok(d), whole(ins[3]), whole(ins[4])]
    else:
        pd = ins[1].shape[1]
        in_specs = [tok(d), pl.BlockSpec((tm, pd), first_map), pl.BlockSpec((tm, pd), rest_map),
                    whole(ins[3]), whole(ins[4]), whole(ins[5]), whole(ins[6])]
    tile_bytes = sum(s.block_shape[0] * s.block_shape[1] * a.dtype.itemsize for s, a in zip(in_specs, ins))
    vmem = 2 * (tile_bytes + tm * d * 4 + 3 * d * tf * 2) + tm * d * 10 + 4 * tm * tf * 4 \
        + 6 * FFN_STAGE_ROWS * d * 4 + (4 << 20)
    return pl.pallas_call(
        functools.partial(_ffn_body, stage=stage, n_first_tiles=n_first // tm),
        grid=(n // tm, dff // tf),
        in_specs=in_specs + [
            pl.BlockSpec((1, d), lambda i, j: (0, 0)),
            pl.BlockSpec((1, d), lambda i, j: (0, 0)),
            pl.BlockSpec((d, tf), lambda i, j: (0, j)),
            pl.BlockSpec((d, tf), lambda i, j: (0, j)),
            pl.BlockSpec((tf, d), lambda i, j: (j, 0)),
        ],
        out_specs=pl.BlockSpec((tm, d), lambda i, j: (i, 0)),
        out_shape=jax.ShapeDtypeStruct((n, d), F32),
        scratch_shapes=[pltpu.VMEM((tm, d), F32), pltpu.VMEM((tm, d), BF16), pltpu.VMEM((tm, d), F32)],
        compiler_params=_cparams(("parallel", "arbitrary"), vmem),
        name="ffn_" + stage,
    )(*ins, g_pre, g_post, wg, wu, wd)


_C_Q, _C_K, _C_IQ, _C_IKW, _C_V, _C_HQ, _C_HF, _C_HI, _C_HG = 0, 512, 640, 1152, 1280, 1408, 1920, 2432, 2944
_C_QR, _C_KR, _C_IQR, _C_IKR, _C_END = 3456, 3968, 4096, 4608, 4736


def _proj_body(h_ref, g_ref, w_ref, cos_ref, sin_ref,
               q_ref, k_ref, v_ref, iq_ref, ikw_ref, hq_ref, hf_ref, hi_ref, hg_ref, ikw16_ref):
    u = _rms(h_ref[...], g_ref[...]).astype(BF16)

    def mm(lo, hi):
        return _dot(u, w_ref[:, lo:hi])

    cos = cos_ref[...]
    sin = sin_ref[...]
    cos4 = jnp.concatenate([cos] * 4, axis=1)
    sin4 = jnp.concatenate([sin] * 4, axis=1)
    q_ref[...] = ((mm(_C_Q, _C_K) * cos4 + mm(_C_QR, _C_KR) * sin4) * QK_SCALE).astype(BF16)
    k_ref[...] = mm(_C_K, _C_IQ) * cos + mm(_C_KR, _C_IQR) * sin
    iq_ref[...] = ((mm(_C_IQ, _C_IKW) * cos4 + mm(_C_IQR, _C_IKR) * sin4) * IDX_SCALE).astype(BF16)
    lane = lax.broadcasted_iota(jnp.int32, cos.shape, 1)
    cos_ikw = jnp.where(lane < IDX_DIM, cos, IDX_HEADS ** -0.5)
    ikw = mm(_C_IKW, _C_V) * cos_ikw + mm(_C_IKR, _C_END) * sin
    ikw_ref[...] = ikw
    ikw16_ref[...] = ikw.astype(BF16)
    v_ref[...] = mm(_C_V, _C_HQ)
    hq_ref[...] = mm(_C_HQ, _C_HF)
    hf_ref[...] = mm(_C_HF, _C_HI)
    hi_ref[...] = mm(_C_HI, _C_HG)
    hg_ref[...] = mm(_C_HG, _C_QR)


def _proj(h, g, w_cat, cos_tab, sin_tab, *, tm, tiles_per_seq, n_prompt_tiles):
    n, d = h.shape
    outs = ((512, BF16), (128, F32), (128, F32), (512, BF16), (128, F32),
            (512, F32), (512, F32), (512, F32), (512, F32), (128, BF16))

    def tab_map(i):
        return (jnp.where(i < n_prompt_tiles, i % tiles_per_seq, tiles_per_seq), 0, 0)

    out_bytes = sum(w * jnp.dtype(t).itemsize for w, t in outs)
    vmem = 2 * (tm * d * 4 + d * _C_END * 2 + tm * out_bytes + 2 * tm * LANES * 4) + 6 * tm * 512 * 4 + (4 << 20)
    return pl.pallas_call(
        _proj_body,
        grid=(n // tm,),
        in_specs=[
            pl.BlockSpec((tm, d), lambda i: (i, 0)),
            pl.BlockSpec((1, d), lambda i: (0, 0)),
            pl.BlockSpec((d, _C_END), lambda i: (0, 0)),
            pl.BlockSpec((None, tm, LANES), tab_map),
            pl.BlockSpec((None, tm, LANES), tab_map),
        ],
        out_specs=[pl.BlockSpec((tm, w), lambda i: (i, 0)) for w, _ in outs],
        out_shape=[jax.ShapeDtypeStruct((n, w), t) for w, t in outs],
        compiler_params=_cparams(("parallel",), vmem),
        name="proj",
    )(h, g, w_cat, cos_tab, sin_tab)


_R_K, _R_KR, _R_V, _R_IK, _R_IKR, _R_END = 0, 128, 256, 384, 448, 512


def _proj_t_body(h_ref, g_ref, w_ref, cos_ref, sin_ref, k_ref, v_ref, ik_ref, k16_ref, v16_ref):
    u = _rms(h_ref[...], g_ref[...]).astype(BF16)
    r = _dot_nt(w_ref[...], u)
    cos = cos_ref[...]
    sin = sin_ref[...]
    k = r[_R_K:_R_KR] * cos + r[_R_KR:_R_V] * sin
    v = r[_R_V:_R_IK]
    ik = r[_R_IK:_R_IKR] * cos[0:IDX_DIM] + r[_R_IKR:_R_END] * sin[0:IDX_DIM]
    k_ref[...] = k
    v_ref[...] = v
    ik_ref[...] = ik
    k16_ref[...] = k.astype(BF16)
    v16_ref[...] = v.astype(BF16)


def _proj_t(h, g, w_t, cos_t, sin_t, *, batch, seq, tm):
    d = h.shape[1]
    nt = seq // tm
    outs = ((KV_WIDTH, F32), (KV_WIDTH, F32), (IDX_DIM, F32), (KV_WIDTH, BF16), (KV_WIDTH, BF16))
    vmem = 2 * (tm * d * 4 + _R_END * d * 2 + 2 * LANES * tm * 4 + 3 * LANES * tm * 6) + 4 * _R_END * tm * 4 + (4 << 20)
    return pl.pallas_call(
        _proj_t_body,
        grid=(batch, nt),
        in_specs=[
            pl.BlockSpec((tm, d), lambda b, t: (b * nt + t, 0)),
            pl.BlockSpec((1, d), lambda b, t: (0, 0)),
            pl.BlockSpec((_R_END, d), lambda b, t: (0, 0)),
            pl.BlockSpec((None, LANES, tm), lambda b, t: (t, 0, 0)),
            pl.BlockSpec((None, LANES, tm), lambda b, t: (t, 0, 0)),
        ],
        out_specs=[pl.BlockSpec((None, w, tm), lambda b, t: (b, 0, t)) for w, _ in outs],
        out_shape=[jax.ShapeDtypeStruct((batch, w, seq), t) for w, t in outs],
        compiler_params=_cparams(("parallel", "parallel"), vmem),
        name="proj_t",
    )(h, g, w_t, cos_t, sin_t)


def _tri_incl(n):
    a = lax.broadcasted_iota(jnp.int32, (n, n), 0)
    b = lax.broadcasted_iota(jnp.int32, (n, n), 1)
    return a, b


I16_MIN = -32768
I16_ROWS = 16


def _tree_sum(xs):
    xs = list(xs)
    while len(xs) > 1:
        nxt = [xs[i] + xs[i + 1] for i in range(0, len(xs) - 1, 2)]
        if len(xs) % 2:
            nxt.append(xs[-1])
        xs = nxt
    return xs[0]


def _count_cols(mask, dtype):
    rows = I16_ROWS if dtype == jnp.int16 else SUBLANES
    one = jnp.where(mask, jnp.ones((), dtype), jnp.zeros((), dtype))
    part = _tree_sum([one[c * rows:(c + 1) * rows] for c in range(mask.shape[0] // rows)])
    return jnp.sum(part.astype(F32), axis=0, keepdims=True)


def _search16(x16, k):
    def body(i, t):
        cand = t + lax.shift_left(jnp.int32(1), 15 - i)
        cnt = _count_cols(x16 >= cand.astype(jnp.int16), jnp.int16)
        return jnp.where(cnt >= k, cand, t)

    return lax.fori_loop(0, 16, body, jnp.full(k.shape, I16_MIN, jnp.int32))


def _key_float(key):
    return lax.bitcast_convert_type(key ^ ((key >> 31) & 0x7FFFFFFF), F32)


_ZERO_IMAGES = 1 << 23


def _succ(key):
    nxt = key + 1
    in_zero = jnp.logical_and(nxt >= -_ZERO_IMAGES, nxt < _ZERO_IMAGES)
    return jnp.where(in_zero, jnp.where(key >= -_ZERO_IMAGES, _ZERO_IMAGES, 0), nxt)


BRACKET_STEPS = 48
KEY_NEG_INF = -2139095041
KEY_POS_INF = 2139095040


def _settle_threshold(score, t0, kf):
    def count(key):
        return _count_cols(score >= _key_float(key), jnp.float32)

    def any_set(flag):
        return jnp.max(jnp.where(flag, 1.0, 0.0)) > 0.0

    def widen(direction):
        def body(state):
            lo, hi, step, _, it = state
            if direction < 0:
                bad = count(lo) < kf
                down = lo - step
                down = jnp.where(jnp.logical_or(down > lo, down < KEY_NEG_INF), KEY_NEG_INF, down)
                lo, hi = jnp.where(bad, down, lo), jnp.where(bad, lo, hi)
            else:
                bad = count(hi) >= kf
                up = hi + step
                up = jnp.where(jnp.logical_or(up < hi, up > KEY_POS_INF), KEY_POS_INF + 1, up)
                lo, hi = jnp.where(bad, hi, lo), jnp.where(bad, up, hi)
            return lo, hi, jnp.minimum(step * 2, 1 << 30), any_set(bad), it + 1
        return body

    def cond(state):
        return jnp.logical_and(state[3], state[4] < BRACKET_STEPS)

    one = jnp.ones_like(t0)
    start = (jnp.bool_(True), jnp.int32(0))
    lo, hi, _, _, _ = lax.while_loop(cond, widen(-1), (t0, _succ(t0), one) + start)
    lo, hi, _, _, _ = lax.while_loop(cond, widen(+1), (lo, hi, one) + start)

    def halve(state):
        lo, hi, _, it = state
        mid = lo + lax.shift_right_logical(hi - lo, 1)
        ok = count(mid) >= kf
        wide = hi > _succ(lo)
        lo = jnp.where(jnp.logical_and(wide, ok), mid, lo)
        hi = jnp.where(jnp.logical_and(wide, jnp.logical_not(ok)), mid, hi)
        return lo, hi, any_set(hi > _succ(lo)), it + 1

    lo, _, _, _ = lax.while_loop(lambda s: jnp.logical_and(s[2], s[3] < BRACKET_STEPS), halve,
                                 (lo, hi, any_set(hi > _succ(lo)), jnp.int32(0)))
    return lo


def _topk_mask_t(score, adm, k):
    l, n = score.shape
    if k >= l:
        return jnp.where(adm, 1.0, 0.0)
    score = jnp.where(score == 0.0, 0.0, score)
    score = jnp.where(adm, score, -jnp.inf)
    bits = lax.bitcast_convert_type(score, jnp.int32)
    key = bits ^ ((bits >> 31) & 0x7FFFFFFF)
    kf = jnp.full((1, n), float(k), F32)
    hi = (key >> 16).astype(jnp.int16)
    t_hi = _search16(hi, kf)
    t_hi16 = t_hi.astype(jnp.int16)
    above = _count_cols(hi > t_hi16, jnp.int16)
    lo = ((key & 0xFFFF) + I16_MIN).astype(jnp.int16)
    lo = jnp.where(hi == t_hi16, lo, jnp.int16(I16_MIN))
    t_lo = _search16(lo, kf - above)
    t = _settle_threshold(score, lax.shift_left(t_hi, 16) | (t_lo - I16_MIN), kf)
    thr = _key_float(t)
    thr = jnp.where(thr != thr, -jnp.inf, thr)
    gt = score >= _key_float(_succ(t))
    inbin = jnp.logical_and(score >= thr, jnp.logical_not(gt))
    need = kf - _count_cols(gt, jnp.float32)

    def top_group(rem):
        pieces = [jnp.where(rem[c * SUBLANES:(c + 1) * SUBLANES] > 0.0, score[c * SUBLANES:(c + 1) * SUBLANES],
                            -jnp.inf) for c in range(l // SUBLANES)]
        while len(pieces) > 1:
            pieces = [jnp.maximum(pieces[i], pieces[i + 1]) for i in range(0, len(pieces) - 1, 2)] + \
                ([pieces[-1]] if len(pieces) % 2 else [])
        grp = jnp.logical_and(rem > 0.0, score == jnp.max(pieces[0], axis=0, keepdims=True))
        return jnp.where(grp, 1.0, 0.0), _count_cols(grp, jnp.float32)

    def fits(cnt, need_):
        return jnp.logical_and(cnt > 0.0, cnt < need_)

    def peel(state):
        rem, whole, need_, grp, cnt, it = state
        f = fits(cnt, need_)
        take = jnp.where(f, grp, 0.0)
        rem, whole, need_ = rem - take, whole + take, jnp.where(f, need_ - cnt, need_)
        grp, cnt = top_group(rem)
        return rem, whole, need_, grp, cnt, it + 1

    rem0 = jnp.where(inbin, 1.0, 0.0)
    grp0, cnt0 = top_group(rem0)
    _, whole, need, eq, _, _ = lax.while_loop(
        lambda s: jnp.logical_and(jnp.max(jnp.where(fits(s[4], s[2]), 1.0, 0.0)) > 0.0, s[5] < BRACKET_STEPS),
        peel, (rem0, jnp.zeros_like(rem0), need, grp0, cnt0, jnp.int32(0)))
    gt = jnp.logical_or(gt, whole > 0.0)
    eq = eq > 0.0
    a, b = _tri_incl(LANES)
    tril = jnp.where(b <= a, 1.0, 0.0).astype(BF16)
    carry = jnp.zeros((1, n), F32)
    sel = []
    for c in range(l // LANES):
        rows = slice(c * LANES, (c + 1) * LANES)
        pref = _dot(tril, jnp.where(eq[rows], 1.0, 0.0).astype(BF16)) + carry
        carry = pref[LANES - 1:LANES]
        take = jnp.logical_or(gt[rows], jnp.logical_and(eq[rows], pref <= need))
        sel.append(jnp.where(jnp.logical_and(take, adm[rows]), 1.0, 0.0))
    return jnp.concatenate(sel, axis=0) if len(sel) > 1 else sel[0]


def _dsa_prompt_tile(q_ref, iq_ref, iw_ref, kt_ref, vt_ref, ik_ref, o_ref, *, q0, l, topk):
    qb = q_ref.shape[0]
    iq_t = iq_ref[...].astype(F32).T.astype(BF16)
    w_t = iw_ref[...].T[IDX_DIM:IDX_DIM + IDX_HEADS]
    ik = ik_ref[0:l, 0:IDX_DIM]
    score = jnp.zeros((l, qb), F32)
    for h in range(IDX_HEADS):
        lg = _dot(ik, iq_t[h * IDX_DIM:(h + 1) * IDX_DIM])
        score = score + w_t[h:h + 1] * jnp.maximum(lg, 0.0)

    kpos = lax.broadcasted_iota(jnp.int32, (l, qb), 0)
    qpos = q0 + lax.broadcasted_iota(jnp.int32, (l, qb), 1)
    sel_t = _topk_mask_t(score, kpos <= qpos, topk)
    bias = ((sel_t - 1.0) * (-NEG_BIG)).T

    q = q_ref[...]
    group = N_HEADS // KV_HEADS
    for h in range(N_HEADS):
        g = h // group
        s = _dot(q[:, h * HEAD_DIM:(h + 1) * HEAD_DIM], kt_ref[g * HEAD_DIM:(g + 1) * HEAD_DIM, 0:l]) + bias
        p = jnp.exp(s - jnp.max(s, axis=-1, keepdims=True))
        den = jnp.sum(p, axis=-1, keepdims=True)
        o = _dot_nt(p.astype(BF16), vt_ref[g * HEAD_DIM:(g + 1) * HEAD_DIM, 0:l])
        o_ref[:, h * HEAD_DIM:(h + 1) * HEAD_DIM] = o / den


def _dsa_prompt_body(q_ref, iq_ref, iw_ref, kt_ref, vt_ref, ik_ref, o_ref, *, topk, n_len):
    qi = pl.program_id(1)
    nq = pl.num_programs(1)
    is_fill = pl.program_id(0) == pl.num_programs(0) - 1
    qb = q_ref.shape[0]
    seq = kt_ref.shape[1]
    per = nq // n_len
    for v in range(n_len):
        l = (v + 1) * per * qb

        @pl.when(jnp.logical_and(jnp.logical_not(is_fill), qi // per == v))
        def _(l=l):
            _dsa_prompt_tile(q_ref, iq_ref, iw_ref, kt_ref, vt_ref, ik_ref, o_ref,
                             q0=qi * qb, l=min(l, seq), topk=topk)

    @pl.when(is_fill)
    def _():
        o_ref[...] = jnp.zeros_like(o_ref)


def _dsa_prompt(q16, iq16, ikw, ikw16, kt16, vt16, *, batch, seq, qb, n_len):
    topk = min(TOPK_MAX, seq // 4)
    nq = seq // qb
    n_len = n_len if nq % n_len == 0 else 1
    last_blk = q16.shape[0] // qb - 1
    assert q16.shape[0] % qb == 0 and last_blk < (batch + 1) * nq
    tok_map = lambda bi, qi: (jnp.minimum(bi * nq + qi, last_blk), 0)
    seq_map = lambda bi, qi: (jnp.minimum(bi, batch - 1), 0, 0)
    vmem = 2 * (2 * qb * 512 * 2 + qb * LANES * 4 + qb * 512 * 4 + seq * (2 * KV_WIDTH + LANES) * 2) \
        + 16 * qb * seq * 4 + (4 << 20)
    return pl.pallas_call(
        functools.partial(_dsa_prompt_body, topk=topk, n_len=n_len),
        grid=(batch + 1, nq),
        in_specs=[
            pl.BlockSpec((qb, ATT_WIDTH), tok_map),
            pl.BlockSpec((qb, IDX_HEADS * IDX_DIM), tok_map),
            pl.BlockSpec((qb, LANES), tok_map),
            pl.BlockSpec((None, KV_WIDTH, seq), seq_map),
            pl.BlockSpec((None, KV_WIDTH, seq), seq_map),
            pl.BlockSpec((seq, LANES), lambda bi, qi: (jnp.minimum(bi, batch - 1), 0)),
        ],
        out_specs=pl.BlockSpec((qb, ATT_WIDTH), tok_map),
        out_shape=jax.ShapeDtypeStruct((q16.shape[0], ATT_WIDTH), F32),
        compiler_params=_cparams(("arbitrary", "arbitrary"), vmem),
        name="dsa_prompt",
    )(q16, iq16, ikw, kt16, vt16, ikw16)


def _dsa_sample_body(pt_ref, q_ref, iq_ref, ikw_ref, kn_ref, vn_ref, ck_hbm, cv_hbm, ci_hbm, att_hbm, o_ref,
                     kbuf, vbuf, ibuf, sems, *, layer, n_pages, page, tq, nb, topk):
    del att_hbm
    g = pl.program_id(0)
    slot = g % 2
    past = n_pages * page
    l = past + LANES
    group = N_HEADS // KV_HEADS

    def page_copies(step, s):
        out = []
        for bj in range(nb):
            for p in range(n_pages):
                idx = pt_ref[(step * nb + bj) * n_pages + p]
                cols = pl.ds(p * page, page)
                out.append(pltpu.make_async_copy(ck_hbm.at[layer, idx], kbuf.at[s, bj, :, cols], sems.at[s, 0]))
                out.append(pltpu.make_async_copy(cv_hbm.at[layer, idx], vbuf.at[s, bj, :, cols], sems.at[s, 1]))
                out.append(pltpu.make_async_copy(ci_hbm.at[layer, idx], ibuf.at[s, bj, :, cols], sems.at[s, 2]))
        return out

    @pl.when(g == 0)
    def _():
        for c in page_copies(0, 0):
            c.start()

    @pl.when(g + 1 < pl.num_programs(0))
    def _():
        for c in page_copies(g + 1, 1 - slot):
            c.start()

    for c in page_copies(g, slot):
        c.wait()

    def pad_rows(x):
        return jnp.concatenate([x, jnp.zeros((LANES - tq, x.shape[1]), x.dtype)], axis=0)

    iq_all = iq_ref[...].astype(F32)
    q_all = q_ref[...].astype(F32)
    ikw_all = ikw_ref[...]

    scores = []
    for bj in range(nb):
        rows = slice(bj * tq, (bj + 1) * tq)
        iq = iq_all[rows]
        iq_hm = jnp.concatenate([iq[:, h * IDX_DIM:(h + 1) * IDX_DIM] for h in range(IDX_HEADS)],
                                axis=0).astype(BF16)
        ik_new = pad_rows(ikw_all[rows, 0:IDX_DIM]).astype(BF16)
        lg = jnp.concatenate([_dot(iq_hm, ibuf[slot, bj].astype(BF16)), _dot_nt(iq_hm, ik_new)], axis=1)
        lg = jnp.maximum(lg, 0.0)
        w = ikw_all[rows, IDX_DIM:IDX_DIM + IDX_HEADS]
        sc = jnp.zeros((tq, l), F32)
        for h in range(IDX_HEADS):
            sc = sc + w[:, h:h + 1] * lg[h * tq:(h + 1) * tq, :]
        scores.append(sc)
    nq = nb * tq
    score_t = jnp.concatenate(scores + [jnp.zeros((LANES - nq, l), F32)], axis=0).T
    kidx = lax.broadcasted_iota(jnp.int32, (l, LANES), 0)
    qidx = lax.broadcasted_iota(jnp.int32, (l, LANES), 1)
    adm = kidx <= past + qidx % tq
    adm = jnp.logical_and(adm, kidx < past + tq)
    adm = jnp.logical_and(adm, qidx < nq)
    sel_t = _topk_mask_t(score_t, adm, topk)
    bias = ((sel_t - 1.0) * (-NEG_BIG)).T

    for bj in range(nb):
        rows = slice(bj * tq, (bj + 1) * tq)
        q = q_all[rows]
        bias_g = jnp.concatenate([bias[rows]] * group, axis=0)
        k_new = pad_rows(kn_ref[rows, :]).astype(BF16)
        v_new = pad_rows(vn_ref[rows, :]).astype(BF16)
        for hk in range(KV_HEADS):
            fs = slice(hk * HEAD_DIM, (hk + 1) * HEAD_DIM)
            qg = jnp.concatenate([q[:, h * HEAD_DIM:(h + 1) * HEAD_DIM]
                                  for h in range(hk * group, (hk + 1) * group)], axis=0).astype(BF16)
            kt = kbuf[slot, bj, fs, :].astype(BF16)
            vt = vbuf[slot, bj, fs, :].astype(BF16)
            s = jnp.concatenate([_dot(qg, kt), _dot_nt(qg, k_new[:, fs])], axis=1) + bias_g
            p = jnp.exp(s - jnp.max(s, axis=-1, keepdims=True))
            den = jnp.sum(p, axis=-1, keepdims=True)
            p = p.astype(BF16)
            o = (_dot_nt(p[:, 0:past], vt) + _dot(p[:, past:l], v_new[:, fs])) / den
            for j in range(group):
                h = hk * group + j
                o_ref[rows, h * HEAD_DIM:(h + 1) * HEAD_DIM] = o[j * tq:(j + 1) * tq, :]


def _dsa_sample(page_table, q16, iq16, ikw, k, v, cache_kt, cache_vt, cache_ikt, att, *, layer, row0, batch, tq, nb):
    n_pages = page_table.shape[1]
    page = cache_kt.shape[3]
    past = n_pages * page
    topk = min(TOPK_MAX, (past + tq) // 4)
    blk0 = row0 // (nb * tq)
    pt = page_table.reshape(-1)

    def tok(w):
        return pl.BlockSpec((nb * tq, w), lambda gi, pt_ref: (blk0 + gi, 0))

    hbm = pl.BlockSpec(memory_space=pl.ANY)
    in_specs = [tok(ATT_WIDTH), tok(IDX_HEADS * IDX_DIM), tok(LANES), tok(KV_WIDTH), tok(KV_WIDTH),
                hbm, hbm, hbm, hbm]
    buf_bytes = 2 * nb * past * (2 * KV_WIDTH + IDX_DIM) * 4
    vmem = buf_bytes + 16 * nb * (past + LANES) * LANES * 4 + (8 << 20)
    return pl.pallas_call(
        functools.partial(_dsa_sample_body, layer=layer, n_pages=n_pages, page=page, tq=tq, nb=nb, topk=topk),
        grid_spec=pltpu.PrefetchScalarGridSpec(
            num_scalar_prefetch=1,
            grid=(batch // nb,),
            in_specs=in_specs,
            out_specs=tok(ATT_WIDTH),
            scratch_shapes=[pltpu.VMEM((2, nb, KV_WIDTH, past), F32),
                            pltpu.VMEM((2, nb, KV_WIDTH, past), F32),
                            pltpu.VMEM((2, nb, IDX_DIM, past), F32),
                            pltpu.SemaphoreType.DMA((2, 3))],
        ),
        out_shape=jax.ShapeDtypeStruct(att.shape, att.dtype),
        input_output_aliases={len(in_specs): 0},
        compiler_params=_cparams(("arbitrary",), vmem),
        name="dsa_sample",
    )(pt, q16, iq16, ikw, k, v, cache_kt, cache_vt, cache_ikt, att)


def _hgrn_levels(c):
    out = []
    n = c // 2
    while n >= 1:
        out.append(n)
        n //= 2
    return out


def _group_row(x, j, g):
    c, n = x.shape
    x3 = x.reshape(c // g, g, n)
    return jnp.broadcast_to(x3[:, j:j + 1, :], (c // g, g, n)).reshape(c, n)


def _boundary_rows(b, n, row):
    g = 2 * n
    if g >= SUBLANES:
        return _group_row(b, n - 1, g)
    out = _group_row(b, n - 1, SUBLANES)
    for j in range(1, SUBLANES // g):
        out = jnp.where((row % SUBLANES) // g == j, _group_row(b, j * g + n - 1, SUBLANES), out)
    return out


def _hgrn_chunk(xq, xf, xi, lb, state_t):
    c, dk = xq.shape
    f = lb + (1.0 - lb) * jax.nn.sigmoid(xf)
    logf = jnp.log(f)
    kk = 1.0 - f
    qf = _silu(xq) * (HG_KDIM ** -0.5)
    ta, tb = _tri_incl(c)
    row = lax.broadcasted_iota(jnp.int32, (c, 1), 0)
    b = logf
    step = 1
    while step < c:
        b = b + jnp.where(row >= step, pltpu.roll(b, step, 0), 0.0)
        step *= 2

    o = _dot_nt((qf * jnp.exp(b)).astype(BF16), state_t.astype(BF16))
    o = o + jnp.sum(qf * kk, axis=-1, keepdims=True) * xi
    xi16 = xi.astype(BF16)
    att = jnp.zeros((c, c), F32)
    for n in _hgrn_levels(c):
        upper = (row % (2 * n)) >= n
        e = jnp.exp(-jnp.abs(b - _boundary_rows(b, n, row)))
        m = (jnp.where(upper, qf, kk) * e).astype(BF16)
        pair = jnp.logical_and(ta // (2 * n) == tb // (2 * n),
                               jnp.logical_and((ta % (2 * n)) >= n, (tb % (2 * n)) < n))
        att = att + jnp.where(pair, _dot_nt(m, m), 0.0)
    o = o + _dot(att.astype(BF16), xi16)

    blast = b[c - 1:c]
    kd = (kk * jnp.exp(blast - b)).astype(BF16)
    new_state_t = state_t * jnp.exp(blast) + _dot_tn(xi16, kd)
    return o, new_state_t


def _hgrn_gate(o, gnorm, xg):
    return _rms(o, gnorm) * _silu(xg)


def _hgrn_prompt_body(hq_ref, hf_ref, hi_ref, hg_ref, lb_ref, gn_ref, o_ref, s_ref, st_ref, *, chunk):
    gi = pl.program_id(1)
    is_fill = pl.program_id(0) == pl.num_programs(0) - 1

    @pl.when(is_fill)
    def _():
        o_ref[...] = jnp.zeros_like(o_ref)

    @pl.when(jnp.logical_not(is_fill))
    def _():
        @pl.when(gi == 0)
        def _():
            st_ref[...] = jnp.zeros_like(st_ref)

        n_chunks = hq_ref.shape[0] // chunk
        gn = gn_ref[...]

        def step(ci, carry):
            rows = pl.ds(pl.multiple_of(ci * chunk, chunk), chunk)
            for h in range(HG_HEADS):
                cols = slice(h * HG_KDIM, (h + 1) * HG_KDIM)
                o, st = _hgrn_chunk(hq_ref[rows, cols], hf_ref[rows, cols], hi_ref[rows, cols],
                                    lb_ref[:, cols], st_ref[h])
                st_ref[h] = st
                o_ref[rows, cols] = _hgrn_gate(o, gn, hg_ref[rows, cols])
            return carry

        lax.fori_loop(0, n_chunks, step, 0, unroll=2)

        @pl.when(gi == pl.num_programs(1) - 1)
        def _():
            for h in range(HG_HEADS):
                s_ref[h] = st_ref[h].T


def _hgrn_prompt(hq, hf, hi, hg, lb, gnorm, *, batch, seq, rows_per_step):
    ng = seq // rows_per_step
    chunk = min(HG_CHUNK, seq)
    last_blk = hq.shape[0] // rows_per_step - 1
    assert hq.shape[0] % rows_per_step == 0 and last_blk < (batch + 1) * ng

    def tok():
        return pl.BlockSpec((rows_per_step, HG_WIDTH), lambda bi, gi: (jnp.minimum(bi * ng + gi, last_blk), 0))

    vmem = 2 * 5 * rows_per_step * HG_WIDTH * 4 + 3 * HG_HEADS * HG_KDIM * HG_VDIM * 4 + (8 << 20)
    return pl.pallas_call(
        functools.partial(_hgrn_prompt_body, chunk=chunk),
        grid=(batch + 1, ng),
        in_specs=[tok(), tok(), tok(), tok(),
                  pl.BlockSpec((1, HG_WIDTH), lambda bi, gi: (0, 0)),
                  pl.BlockSpec((1, HG_VDIM), lambda bi, gi: (0, 0))],
        out_specs=[tok(), pl.BlockSpec((None, HG_HEADS, HG_KDIM, HG_VDIM),
                                       lambda bi, gi: (jnp.minimum(bi, batch - 1), 0, 0, 0))],
        out_shape=[jax.ShapeDtypeStruct((hq.shape[0], HG_WIDTH), F32),
                   jax.ShapeDtypeStruct((batch, HG_HEADS, HG_KDIM, HG_VDIM), F32)],
        scratch_shapes=[pltpu.VMEM((HG_HEADS, HG_KDIM, HG_VDIM), F32)],
        compiler_params=_cparams(("arbitrary", "arbitrary"), vmem),
        name="hgrn_prompt",
    )(hq, hf, hi, hg, lb, gnorm)


def _hgrn_sample_body(hq_ref, hf_ref, hi_ref, hg_ref, lb_ref, gn_ref, s0_ref, base_ref, o_ref, s_ref, *, tq, nb):
    del base_ref
    gn = gn_ref[...]

    def step(bi, carry):
        rows = pl.ds(pl.multiple_of(bi * tq, tq), tq)
        for h in range(HG_HEADS):
            cols = slice(h * HG_KDIM, (h + 1) * HG_KDIM)
            o, st = _hgrn_chunk(hq_ref[rows, cols], hf_ref[rows, cols], hi_ref[rows, cols],
                                lb_ref[:, cols], s0_ref[bi, h].T)
            s_ref[bi, h] = st.T
            o_ref[rows, cols] = _hgrn_gate(o, gn, hg_ref[rows, cols])
        return carry

    lax.fori_loop(0, nb, step, 0, unroll=2)


def _hgrn_sample(hq, hf, hi, hg, lb, gnorm, state, ohg, *, layer, row0, batch, tq, nb):
    assert tq <= HG_CHUNK
    blk0 = row0 // (nb * tq)

    def tok():
        return pl.BlockSpec((nb * tq, HG_WIDTH), lambda gi: (blk0 + gi, 0))

    st_blk = (nb, HG_HEADS, HG_KDIM, HG_VDIM)
    vmem = 2 * (5 * nb * tq * HG_WIDTH * 4 + 2 * nb * HG_HEADS * HG_KDIM * HG_VDIM * 4) + (8 << 20)
    return pl.pallas_call(
        functools.partial(_hgrn_sample_body, tq=tq, nb=nb),
        grid=(batch // nb,),
        in_specs=[tok(), tok(), tok(), tok(),
                  pl.BlockSpec((1, HG_WIDTH), lambda gi: (0, 0)),
                  pl.BlockSpec((1, HG_VDIM), lambda gi: (0, 0)),
                  pl.BlockSpec((None,) + st_blk, lambda gi: (layer, gi, 0, 0, 0)),
                  pl.BlockSpec(memory_space=pl.ANY)],
        out_specs=[tok(), pl.BlockSpec(st_blk, lambda gi: (gi, 0, 0, 0))],
        input_output_aliases={7: 0},
        out_shape=[jax.ShapeDtypeStruct(ohg.shape, ohg.dtype),
                   jax.ShapeDtypeStruct((batch, HG_HEADS, HG_KDIM, HG_VDIM), F32)],
        compiler_params=_cparams(("parallel",), vmem),
        name="hgrn_sample",
    )(hq, hf, hi, hg, lb, gnorm, state, ohg)


def _ple_body(h_ref, pp_ref, ps_ref, gpre_ref, gpost_ref, wgate_ref, wproj_ref, yp_ref, ys_ref, *, n_first_tiles):
    y = _ple_rows(h_ref[...], _pick(pp_ref, ps_ref, n_first_tiles), gpre_ref[...], gpost_ref[...],
                  wgate_ref[...], wproj_ref[...])
    is_first = pl.program_id(0) < n_first_tiles

    @pl.when(is_first)
    def _():
        yp_ref[...] = y

    @pl.when(jnp.logical_not(is_first))
    def _():
        ys_ref[...] = y


def _ple(h, p_first, p_rest, g_pre, g_post, w_gate, w_proj, *, tm):
    n, d = h.shape
    pd = p_first.shape[1]
    n_first = p_first.shape[0] // tm
    assert p_first.shape[0] % tm == 0 and p_rest.shape[0] % tm == 0
    first_map, rest_map = _split_maps(n_first)
    vmem = 2 * (3 * tm * d * 4 + 2 * tm * pd * 4 + d * d * 2 + pd * d * 2) + 3 * tm * d * 4 + (4 << 20)
    return pl.pallas_call(
        functools.partial(_ple_body, n_first_tiles=n_first),
        grid=(n // tm,),
        in_specs=[pl.BlockSpec((tm, d), lambda i: (i, 0)),
                  pl.BlockSpec((tm, pd), first_map),
                  pl.BlockSpec((tm, pd), rest_map),
                  pl.BlockSpec((1, d), lambda i: (0, 0)),
                  pl.BlockSpec((1, d), lambda i: (0, 0)),
                  pl.BlockSpec((d, d), lambda i: (0, 0)),
                  pl.BlockSpec((pd, d), lambda i: (0, 0))],
        out_specs=[pl.BlockSpec((tm, d), first_map), pl.BlockSpec((tm, d), rest_map)],
        out_shape=[jax.ShapeDtypeStruct((p_first.shape[0], d), F32), jax.ShapeDtypeStruct((p_rest.shape[0], d), F32)],
        compiler_params=_cparams(("arbitrary",), vmem),
        name="ple",
    )(h, p_first, p_rest, g_pre, g_post, w_gate, w_proj)


def _rot_cols(w, heads):
    d = w.shape[0]
    w4 = w.reshape(d, heads, 2, HEAD_DIM // 2)
    return jnp.concatenate([-w4[:, :, 1:2], w4[:, :, 0:1]], axis=2).reshape(d, heads * HEAD_DIM)


def _pack_w_in(w):
    d = w.shape[0]
    widths = (ATT_WIDTH, KV_WIDTH, KV_WIDTH, IDX_HEADS * IDX_DIM, IDX_DIM, IDX_HEADS,
              HG_HEADS * HG_KDIM, HG_HEADS * HG_KDIM, HG_WIDTH, HG_WIDTH)
    parts = []
    acc = 0
    for wd in widths:
        parts.append(w[:, acc:acc + wd])
        acc += wd
    q, k, v, iq, ik, iw, hq, hf, hi, hg = parts
    z = lambda n: jnp.zeros((d, n), w.dtype)
    ikw = jnp.concatenate([ik, iw, z(LANES - IDX_DIM - IDX_HEADS)], axis=1)
    ikr = jnp.concatenate([_rot_cols(ik, 1), z(LANES - IDX_DIM)], axis=1)
    cat = jnp.concatenate([q, k, iq, ikw, v, hq, hf, hi, hg,
                           _rot_cols(q, N_HEADS), _rot_cols(k, KV_HEADS), _rot_cols(iq, IDX_HEADS), ikr], axis=1)
    assert cat.shape[1] == _C_END
    w_t = jnp.concatenate([k, _rot_cols(k, KV_HEADS), v, ik, _rot_cols(ik, 1)], axis=1).T
    assert w_t.shape[0] == _R_END
    return cat.astype(BF16), w_t.astype(BF16)


def _rope_tables(seq, past_len, tq, tm):
    half = HEAD_DIM // 2
    inv_freq = ROPE_THETA ** (-jnp.arange(half, dtype=F32) / half)
    pos_p = jnp.arange(seq, dtype=jnp.int32)
    pos_s = past_len + (jnp.arange(tm, dtype=jnp.int32) % tq)
    pos = jnp.concatenate([pos_p, pos_s]).astype(F32)
    ang = pos[:, None] * inv_freq[None, :]
    reps = LANES // half
    cos = jnp.tile(jnp.cos(ang), (1, reps)).reshape(seq // tm + 1, tm, LANES)
    sin = jnp.tile(jnp.sin(ang), (1, reps)).reshape(seq // tm + 1, tm, LANES)
    nt = seq // tm
    cos_t = jnp.swapaxes(cos[:nt], 1, 2)
    sin_t = jnp.swapaxes(sin[:nt], 1, 2)
    return cos, sin, cos_t, sin_t


def _lower_bounds(logits):
    sm = jax.nn.softmax(logits.astype(F32), axis=0)
    return jnp.cumsum(sm, axis=0) - sm[0:1]


def _feature_major(cache, width):
    depth, pool, page = cache.shape[:3]
    c = cache.reshape(depth, pool, page, width)
    return jnp.swapaxes(c, 2, 3)


TM_DENSE = 1024
TM_PROJ = 512
TM_FFN = 1024
TF_FFN = 256
Q_BLOCK = 128
N_KEY_LEN = 4
DSA_SEQS = 4
HG_ROWS = 512
HG_SEQS = 8


def kernel(x_prompt, x_sample, p_prompt, p_sample, cache_k, cache_v, cache_idx_k, state_hgrn, page_table,
           n_f1_pre, n_f1_post, w_f1_gate, w_f1_up, w_f1_down,
           n_mix_pre, n_mix_post, w_in, hg_lb_logits, hg_norm, w_out,
           n_f2_pre, n_f2_post, w_f2_gate, w_f2_up, w_f2_down,
           n_ple_pre, n_ple_post, w_ple_proj, w_ple_gate):
    bp, tp, d = x_prompt.shape
    bs, ts, _ = x_sample.shape
    depth = w_in.shape[0]
    n_p, n_s = bp * tp, bs * ts
    page = cache_k.shape[2]
    past_len = page_table.shape[1] * page

    ckt = _feature_major(cache_k, KV_WIDTH)
    cvt = _feature_major(cache_v, KV_WIDTH)
    cikt = _feature_major(cache_idx_k, IDX_DIM)
    cos_tab, sin_tab, cos_t, sin_t = _rope_tables(tp, past_len, ts, TM_PROJ)
    lbs = _lower_bounds(hg_lb_logits)
    row = lambda a: a.reshape(1, -1)
    bf = lambda a: a.astype(BF16)

    outs = {name: [] for name in ("kp", "vp", "ikp", "sp", "ks", "vs", "iks", "ss")}
    ffn_kw = dict(n=n_p + n_s, tm=TM_FFN, tf=TF_FFN, n_first=n_p)
    ple_args = lambda i: (p_prompt[i].reshape(n_p, -1), p_sample[i].reshape(n_s, -1),
                          row(n_ple_pre[i]), row(n_ple_post[i]), bf(w_ple_gate[i]), bf(w_ple_proj[i]))
    h3 = None
    for i in range(depth):
        stage, ins = ("x2", (x_prompt.reshape(n_p, d), x_sample.reshape(n_s, d))) if i == 0 else \
            ("ple", (h3,) + ple_args(i - 1))
        h1 = _ffn(stage, ins, row(n_f1_pre[i]), row(n_f1_post[i]), bf(w_f1_gate[i]), bf(w_f1_up[i]),
                  bf(w_f1_down[i]), **ffn_kw)
        w_cat, w_t = _pack_w_in(w_in[i])
        g_mix = row(n_mix_pre[i])
        q16, k, v, iq16, ikw, hq, hf, hi, hg, ikw16 = _proj(
            h1, g_mix, w_cat, cos_tab, sin_tab,
            tm=TM_PROJ, tiles_per_seq=tp // TM_PROJ, n_prompt_tiles=n_p // TM_PROJ)
        kt, vt, ikt, kt16, vt16 = _proj_t(h1, g_mix, w_t, cos_t, sin_t, batch=bp, seq=tp, tm=TM_PROJ)

        att = _dsa_prompt(q16, iq16, ikw, ikw16, kt16, vt16, batch=bp, seq=tp, qb=min(Q_BLOCK, tp),
                          n_len=N_KEY_LEN)
        att = _dsa_sample(page_table, q16, iq16, ikw, k, v, ckt, cvt, cikt, att,
                          layer=i, row0=n_p, batch=bs, tq=ts, nb=DSA_SEQS)
        lb = row(lbs[i])
        gn = row(hg_norm[i])
        ohg, s_p = _hgrn_prompt(hq, hf, hi, hg, lb, gn, batch=bp, seq=tp, rows_per_step=HG_ROWS)
        ohg, s_s = _hgrn_sample(hq, hf, hi, hg, lb, gn, state_hgrn, ohg,
                                layer=i, row0=n_p, batch=bs, tq=ts, nb=HG_SEQS)

        h3 = _ffn("mix", (att, ohg, h1, row(n_mix_post[i]), bf(w_out[i])),
                  row(n_f2_pre[i]), row(n_f2_post[i]), bf(w_f2_gate[i]), bf(w_f2_up[i]), bf(w_f2_down[i]), **ffn_kw)

        outs["kp"].append(jnp.transpose(kt.reshape(bp, KV_HEADS, HEAD_DIM, tp), (0, 3, 1, 2)))
        outs["vp"].append(jnp.transpose(vt.reshape(bp, KV_HEADS, HEAD_DIM, tp), (0, 3, 1, 2)))
        outs["ikp"].append(jnp.swapaxes(ikt, 1, 2))
        outs["sp"].append(s_p)
        outs["ks"].append(k[n_p:].reshape(bs, ts, KV_HEADS, HEAD_DIM))
        outs["vs"].append(v[n_p:].reshape(bs, ts, KV_HEADS, HEAD_DIM))
        outs["iks"].append(ikw[n_p:, :IDX_DIM].reshape(bs, ts, IDX_DIM))
        outs["ss"].append(s_s.astype(state_hgrn.dtype))

    y_p, y_s = _ple(h3, *ple_args(depth - 1), tm=TM_DENSE)
    st = lambda name: jnp.stack(outs[name])
    return (y_p.reshape(bp, tp, d), y_s.reshape(bs, ts, d),
            st("kp"), st("vp"), st("ikp"), st("sp"), st("ks"), st("vs"), st("iks"), st("ss"))
```

```python
import functools

import jax
import jax.numpy as jnp
from jax import lax
from jax.experimental import pallas as pl
from jax.experimental.pallas import tpu as pltpu

HEAD_DIM = 64
N_HEADS = 8
KV_HEADS = 2
IDX_HEADS = 8
IDX_DIM = 64
TOPK_MAX = 256
HG_KDIM = 128
HG_VDIM = 128
HG_HEADS = 4
HG_CHUNK = 64
ROPE_THETA = 10000.0
EPS = 1e-6
ATT_WIDTH = N_HEADS * HEAD_DIM
KV_WIDTH = KV_HEADS * HEAD_DIM
HG_WIDTH = HG_HEADS * HG_VDIM
QK_SCALE = HEAD_DIM ** -0.5
IDX_SCALE = IDX_DIM ** -0.5

LANES = 128
SUBLANES = 8
VMEM_BUDGET_BYTES = 56 * 1024 * 1024

NEG_BIG = -1e30

F32 = jnp.float32
BF16 = jnp.bfloat16


def _cparams(sem, vmem_bytes):
    return pltpu.CompilerParams(dimension_semantics=sem,
                                vmem_limit_bytes=int(min(max(vmem_bytes, 16 << 20), VMEM_BUDGET_BYTES)))


def _rms(x, g):
    return x * lax.rsqrt(jnp.mean(x * x, axis=-1, keepdims=True) + EPS) * g


def _silu(x):
    return x * jax.nn.sigmoid(x)


def _dot(a, b):
    return jnp.dot(a, b, preferred_element_type=F32)


def _dot_nt(a, b):
    return lax.dot_general(a, b, (((1,), (1,)), ((), ())), preferred_element_type=F32)


def _dot_tn(a, b):
    return lax.dot_general(a, b, (((0,), (0,)), ((), ())), preferred_element_type=F32)


def _split_maps(n_first_tiles):
    return (lambda i, *_: (jnp.minimum(i, n_first_tiles - 1), 0),
            lambda i, *_: (jnp.maximum(i - n_first_tiles, 0), 0))


def _pick(first_ref, rest_ref, n_first_tiles):
    return jnp.where(pl.program_id(0) < n_first_tiles, first_ref[...], rest_ref[...])


def _mix_rows(att, ohg, h, g, w_ref):
    mix = _dot(att.astype(BF16), w_ref[0:ATT_WIDTH, :]) + \
        _dot(ohg.astype(BF16), w_ref[ATT_WIDTH:ATT_WIDTH + HG_WIDTH, :])
    return h + _rms(mix, g)


def _ple_rows(h, p, g_pre, g_post, w_gate, w_proj):
    gate = jax.nn.sigmoid(_dot(_rms(h, g_pre).astype(BF16), w_gate))
    return h + _rms(_dot(p.astype(BF16), w_proj) * gate, g_post)


_FFN_STAGE_INPUTS = {"x": 1, "x2": 2, "mix": 5, "ple": 7}
FFN_STAGE_ROWS = 256


def _ffn_body(*refs, stage, n_first_tiles):
    n_in = _FFN_STAGE_INPUTS[stage]
    ins = refs[:n_in]
    gpre_ref, gpost_ref, wg_ref, wu_ref, wd_ref, o_ref, x_ref, xn_ref, acc_ref = refs[n_in:]
    j = pl.program_id(1)
    is_first = pl.program_id(0) < n_first_tiles

    def stage_rows(r):
        if stage == "x":
            return ins[0][r, :]
        if stage == "x2":
            return jnp.where(is_first, ins[0][r, :], ins[1][r, :])
        if stage == "mix":
            att, ohg, h, g, w = ins
            return _mix_rows(att[r, :], ohg[r, :], h[r, :], g[...], w)
        h, pp, ps, g1, g2, wgate, wproj = ins
        return _ple_rows(h[r, :], jnp.where(is_first, pp[r, :], ps[r, :]), g1[...], g2[...], wgate[...], wproj[...])

    @pl.when(j == 0)
    def _():
        tm = x_ref.shape[0]
        rc = min(FFN_STAGE_ROWS, tm)
        for r0 in range(0, tm, rc):
            r = slice(r0, r0 + rc)
            x = stage_rows(r)
            x_ref[r, :] = x
            xn_ref[r, :] = _rms(x, gpre_ref[...]).astype(BF16)
        acc_ref[...] = jnp.zeros_like(acc_ref)

    xn = xn_ref[...]
    a = _dot(xn, wg_ref[...])
    b = _dot(xn, wu_ref[...])
    acc_ref[...] += _dot((_silu(a) * b).astype(BF16), wd_ref[...])

    @pl.when(j == pl.num_programs(1) - 1)
    def _():
        o_ref[...] = x_ref[...] + 0.5 * _rms(acc_ref[...], gpost_ref[...])


def _ffn(stage, ins, g_pre, g_post, wg, wu, wd, *, n, tm, tf, n_first):
    d = wg.shape[0]
    dff = wg.shape[1]
    assert n % tm == 0 and n_first % tm == 0 and dff % tf == 0 and len(ins) == _FFN_STAGE_INPUTS[stage]
    first_map, rest_map = _split_maps(n_first // tm)
    tok = lambda w: pl.BlockSpec((tm, w), lambda i, j: (i, 0))
    whole = lambda a: pl.BlockSpec(a.shape, lambda i, j: (0,) * a.ndim)
    if stage == "x":
        in_specs = [tok(d)]
    elif stage == "x2":
        in_specs = [pl.BlockSpec((tm, d), first_map), pl.BlockSpec((tm, d), rest_map)]
    elif stage == "mix":
        in_specs = [tok(ATT_WIDTH), tok(HG_WIDTH), tok(d), whole(ins[3]), whole(ins[4])]
    else:
        pd = ins[1].shape[1]
        in_specs = [tok(d), pl.BlockSpec((tm, pd), first_map), pl.BlockSpec((tm, pd), rest_map),
                    whole(ins[3]), whole(ins[4]), whole(ins[5]), whole(ins[6])]
    tile_bytes = sum(s.block_shape[0] * s.block_shape[1] * a.dtype.itemsize for s, a in zip(in_specs, ins))
    vmem = 2 * (tile_bytes + tm * d * 4 + 3 * d * tf * 2) + tm * d * 10 + 4 * tm * tf * 4 \
        + 6 * FFN_STAGE_ROWS * d * 4 + (4 << 20)
    return pl.pallas_call(
        functools.partial(_ffn_body, stage=stage, n_first_tiles=n_first // tm),
        grid=(n // tm, dff // tf),
        in_specs=in_specs + [
            pl.BlockSpec((1, d), lambda i, j: (0, 0)),
            pl.BlockSpec((1, d), lambda i, j: (0, 0)),
            pl.BlockSpec((d, tf), lambda i, j: (0, j)),
            pl.BlockSpec((d, tf), lambda i, j: (0, j)),
            pl.BlockSpec((tf, d), lambda i, j: (j, 0)),
        ],
        out_specs=pl.BlockSpec((tm, d), lambda i, j: (i, 0)),
        out_shape=jax.ShapeDtypeStruct((n, d), F32),
        scratch_shapes=[pltpu.VMEM((tm, d), F32), pltpu.VMEM((tm, d), BF16), pltpu.VMEM((tm, d), F32)],
        compiler_params=_cparams(("parallel", "arbitrary"), vmem),
        name="ffn_" + stage,
    )(*ins, g_pre, g_post, wg, wu, wd)


_C_Q, _C_K, _C_IQ, _C_IKW, _C_V, _C_HQ, _C_HF, _C_HI, _C_HG = 0, 512, 640, 1152, 1280, 1408, 1920, 2432, 2944
_C_QR, _C_KR, _C_IQR, _C_IKR, _C_END = 3456, 3968, 4096, 4608, 4736


def _proj_body(h_ref, g_ref, w_ref, cos_ref, sin_ref,
               q_ref, k_ref, v_ref, iq_ref, ikw_ref, hq_ref, hf_ref, hi_ref, hg_ref, ikw16_ref):
    u = _rms(h_ref[...], g_ref[...]).astype(BF16)

    def mm(lo, hi):
        return _dot(u, w_ref[:, lo:hi])

    cos = cos_ref[...]
    sin = sin_ref[...]
    cos4 = jnp.concatenate([cos] * 4, axis=1)
    sin4 = jnp.concatenate([sin] * 4, axis=1)
    q_ref[...] = ((mm(_C_Q, _C_K) * cos4 + mm(_C_QR, _C_KR) * sin4) * QK_SCALE).astype(BF16)
    k_ref[...] = mm(_C_K, _C_IQ) * cos + mm(_C_KR, _C_IQR) * sin
    iq_ref[...] = ((mm(_C_IQ, _C_IKW) * cos4 + mm(_C_IQR, _C_IKR) * sin4) * IDX_SCALE).astype(BF16)
    lane = lax.broadcasted_iota(jnp.int32, cos.shape, 1)
    cos_ikw = jnp.where(lane < IDX_DIM, cos, IDX_HEADS ** -0.5)
    ikw = mm(_C_IKW, _C_V) * cos_ikw + mm(_C_IKR, _C_END) * sin
    ikw_ref[...] = ikw
    ikw16_ref[...] = ikw.astype(BF16)
    v_ref[...] = mm(_C_V, _C_HQ)
    hq_ref[...] = mm(_C_HQ, _C_HF)
    hf_ref[...] = mm(_C_HF, _C_HI)
    hi_ref[...] = mm(_C_HI, _C_HG)
    hg_ref[...] = mm(_C_HG, _C_QR)


def _proj(h, g, w_cat, cos_tab, sin_tab, *, tm, tiles_per_seq, n_prompt_tiles):
    n, d = h.shape
    outs = ((512, BF16), (128, F32), (128, F32), (512, BF16), (128, F32),
            (512, F32), (512, F32), (512, F32), (512, F32), (128, BF16))

    def tab_map(i):
        return (jnp.where(i < n_prompt_tiles, i % tiles_per_seq, tiles_per_seq), 0, 0)

    out_bytes = sum(w * jnp.dtype(t).itemsize for w, t in outs)
    vmem = 2 * (tm * d * 4 + d * _C_END * 2 + tm * out_bytes + 2 * tm * LANES * 4) + 6 * tm * 512 * 4 + (4 << 20)
    return pl.pallas_call(
        _proj_body,
        grid=(n // tm,),
        in_specs=[
            pl.BlockSpec((tm, d), lambda i: (i, 0)),
            pl.BlockSpec((1, d), lambda i: (0, 0)),
            pl.BlockSpec((d, _C_END), lambda i: (0, 0)),
            pl.BlockSpec((None, tm, LANES), tab_map),
            pl.BlockSpec((None, tm, LANES), tab_map),
        ],
        out_specs=[pl.BlockSpec((tm, w), lambda i: (i, 0)) for w, _ in outs],
        out_shape=[jax.ShapeDtypeStruct((n, w), t) for w, t in outs],
        compiler_params=_cparams(("parallel",), vmem),
        name="proj",
    )(h, g, w_cat, cos_tab, sin_tab)


_R_K, _R_KR, _R_V, _R_IK, _R_IKR, _R_END = 0, 128, 256, 384, 448, 512


def _proj_t_body(h_ref, g_ref, w_ref, cos_ref, sin_ref, k_ref, v_ref, ik_ref, k16_ref, v16_ref):
    u = _rms(h_ref[...], g_ref[...]).astype(BF16)
    r = _dot_nt(w_ref[...], u)
    cos = cos_ref[...]
    sin = sin_ref[...]
    k = r[_R_K:_R_KR] * cos + r[_R_KR:_R_V] * sin
    v = r[_R_V:_R_IK]
    ik = r[_R_IK:_R_IKR] * cos[0:IDX_DIM] + r[_R_IKR:_R_END] * sin[0:IDX_DIM]
    k_ref[...] = k
    v_ref[...] = v
    ik_ref[...] = ik
    k16_ref[...] = k.astype(BF16)
    v16_ref[...] = v.astype(BF16)


def _proj_t(h, g, w_t, cos_t, sin_t, *, batch, seq, tm):
    d = h.shape[1]
    nt = seq // tm
    outs = ((KV_WIDTH, F32), (KV_WIDTH, F32), (IDX_DIM, F32), (KV_WIDTH, BF16), (KV_WIDTH, BF16))
    vmem = 2 * (tm * d * 4 + _R_END * d * 2 + 2 * LANES * tm * 4 + 3 * LANES * tm * 6) + 4 * _R_END * tm * 4 + (4 << 20)
    return pl.pallas_call(
        _proj_t_body,
        grid=(batch, nt),
        in_specs=[
            pl.BlockSpec((tm, d), lambda b, t: (b * nt + t, 0)),
            pl.BlockSpec((1, d), lambda b, t: (0, 0)),
            pl.BlockSpec((_R_END, d), lambda b, t: (0, 0)),
            pl.BlockSpec((None, LANES, tm), lambda b, t: (t, 0, 0)),
            pl.BlockSpec((None, LANES, tm), lambda b, t: (t, 0, 0)),
        ],
        out_specs=[pl.BlockSpec((None, w, tm), lambda b, t: (b, 0, t)) for w, _ in outs],
        out_shape=[jax.ShapeDtypeStruct((batch, w, seq), t) for w, t in outs],
        compiler_params=_cparams(("parallel", "parallel"), vmem),
        name="proj_t",
    )(h, g, w_t, cos_t, sin_t)


def _tri_incl(n):
    a = lax.broadcasted_iota(jnp.int32, (n, n), 0)
    b = lax.broadcasted_iota(jnp.int32, (n, n), 1)
    return a, b


I16_MIN = -32768
I16_ROWS = 16


def _tree_sum(xs):
    xs = list(xs)
    while len(xs) > 1:
        nxt = [xs[i] + xs[i + 1] for i in range(0, len(xs) - 1, 2)]
        if len(xs) % 2:
            nxt.append(xs[-1])
        xs = nxt
    return xs[0]


def _count_cols(mask, dtype):
    rows = I16_ROWS if dtype == jnp.int16 else SUBLANES
    one = jnp.where(mask, jnp.ones((), dtype), jnp.zeros((), dtype))
    part = _tree_sum([one[c * rows:(c + 1) * rows] for c in range(mask.shape[0] // rows)])
    return jnp.sum(part.astype(F32), axis=0, keepdims=True)


def _search16(x16, k):
    def body(i, t):
        cand = t + lax.shift_left(jnp.int32(1), 15 - i)
        cnt = _count_cols(x16 >= cand.astype(jnp.int16), jnp.int16)
        return jnp.where(cnt >= k, cand, t)

    return lax.fori_loop(0, 16, body, jnp.full(k.shape, I16_MIN, jnp.int32))


def _topk_mask_t(score, adm, k):
    l, n = score.shape
    score = jnp.where(score == 0.0, 0.0, score)
    score = jnp.where(adm, score, -jnp.inf)
    bits = lax.bitcast_convert_type(score, jnp.int32)
    key = bits ^ ((bits >> 31) & 0x7FFFFFFF)
    kf = jnp.full((1, n), float(k), F32)
    hi = (key >> 16).astype(jnp.int16)
    t_hi = _search16(hi, kf)
    t_hi16 = t_hi.astype(jnp.int16)
    above = _count_cols(hi > t_hi16, jnp.int16)
    lo = ((key & 0xFFFF) + I16_MIN).astype(jnp.int16)
    lo = jnp.where(hi == t_hi16, lo, jnp.int16(I16_MIN))
    t_lo = _search16(lo, kf - above)
    t = lax.shift_left(t_hi, 16) | (t_lo - I16_MIN)
    ge = key >= t
    surplus = _count_cols(ge, jnp.float32) - kf

    def no_ties():
        return jnp.where(jnp.logical_and(ge, adm), 1.0, 0.0)

    def with_ties():
        gt = key > t
        eq = key == t
        need = kf - _count_cols(gt, jnp.float32)
        a, b = _tri_incl(LANES)
        tril = jnp.where(b <= a, 1.0, 0.0).astype(BF16)
        carry = jnp.zeros((1, n), F32)
        sel = []
        for c in range(l // LANES):
            rows = slice(c * LANES, (c + 1) * LANES)
            pref = _dot(tril, jnp.where(eq[rows], 1.0, 0.0).astype(BF16)) + carry
            carry = pref[LANES - 1:LANES]
            take = jnp.logical_or(gt[rows], jnp.logical_and(eq[rows], pref <= need))
            sel.append(jnp.where(jnp.logical_and(take, adm[rows]), 1.0, 0.0))
        return jnp.concatenate(sel, axis=0) if len(sel) > 1 else sel[0]

    return lax.cond(jnp.max(surplus) > 0.0, with_ties, no_ties)


def _dsa_prompt_tile(q_ref, iq_ref, iw_ref, kt_ref, vt_ref, ik_ref, o_ref, *, q0, l, topk):
    qb = q_ref.shape[0]
    iq_t = iq_ref[...].astype(F32).T.astype(BF16)
    w_t = iw_ref[...].T[IDX_DIM:IDX_DIM + IDX_HEADS]
    ik = ik_ref[0:l, 0:IDX_DIM]
    score = jnp.zeros((l, qb), F32)
    for h in range(IDX_HEADS):
        lg = _dot(ik, iq_t[h * IDX_DIM:(h + 1) * IDX_DIM])
        score = score + w_t[h:h + 1] * jnp.maximum(lg, 0.0)

    kpos = lax.broadcasted_iota(jnp.int32, (l, qb), 0)
    qpos = q0 + lax.broadcasted_iota(jnp.int32, (l, qb), 1)
    sel_t = _topk_mask_t(score, kpos <= qpos, topk)
    bias = ((sel_t - 1.0) * (-NEG_BIG)).T

    q = q_ref[...]
    group = N_HEADS // KV_HEADS
    for h in range(N_HEADS):
        g = h // group
        s = _dot(q[:, h * HEAD_DIM:(h + 1) * HEAD_DIM], kt_ref[g * HEAD_DIM:(g + 1) * HEAD_DIM, 0:l]) + bias
        p = jnp.exp(s - jnp.max(s, axis=-1, keepdims=True))
        den = jnp.sum(p, axis=-1, keepdims=True)
        o = _dot_nt(p.astype(BF16), vt_ref[g * HEAD_DIM:(g + 1) * HEAD_DIM, 0:l])
        o_ref[:, h * HEAD_DIM:(h + 1) * HEAD_DIM] = o / den


def _dsa_prompt_body(q_ref, iq_ref, iw_ref, kt_ref, vt_ref, ik_ref, o_ref, *, topk, n_len):
    qi = pl.program_id(1)
    nq = pl.num_programs(1)
    is_fill = pl.program_id(0) == pl.num_programs(0) - 1
    qb = q_ref.shape[0]
    seq = kt_ref.shape[1]
    per = nq // n_len
    for v in range(n_len):
        l = (v + 1) * per * qb

        @pl.when(jnp.logical_and(jnp.logical_not(is_fill), qi // per == v))
        def _(l=l):
            _dsa_prompt_tile(q_ref, iq_ref, iw_ref, kt_ref, vt_ref, ik_ref, o_ref,
                             q0=qi * qb, l=min(l, seq), topk=topk)

    @pl.when(is_fill)
    def _():
        o_ref[...] = jnp.zeros_like(o_ref)


def _dsa_prompt(q16, iq16, ikw, ikw16, kt16, vt16, *, batch, seq, qb, n_len):
    topk = min(TOPK_MAX, seq // 4)
    nq = seq // qb
    n_len = n_len if nq % n_len == 0 else 1
    last_blk = q16.shape[0] // qb - 1
    assert q16.shape[0] % qb == 0 and last_blk < (batch + 1) * nq
    tok_map = lambda bi, qi: (jnp.minimum(bi * nq + qi, last_blk), 0)
    seq_map = lambda bi, qi: (jnp.minimum(bi, batch - 1), 0, 0)
    vmem = 2 * (2 * qb * 512 * 2 + qb * LANES * 4 + qb * 512 * 4 + seq * (2 * KV_WIDTH + LANES) * 2) \
        + 16 * qb * seq * 4 + (4 << 20)
    return pl.pallas_call(
        functools.partial(_dsa_prompt_body, topk=topk, n_len=n_len),
        grid=(batch + 1, nq),
        in_specs=[
            pl.BlockSpec((qb, ATT_WIDTH), tok_map),
            pl.BlockSpec((qb, IDX_HEADS * IDX_DIM), tok_map),
            pl.BlockSpec((qb, LANES), tok_map),
            pl.BlockSpec((None, KV_WIDTH, seq), seq_map),
            pl.BlockSpec((None, KV_WIDTH, seq), seq_map),
            pl.BlockSpec((seq, LANES), lambda bi, qi: (jnp.minimum(bi, batch - 1), 0)),
        ],
        out_specs=pl.BlockSpec((qb, ATT_WIDTH), tok_map),
        out_shape=jax.ShapeDtypeStruct((q16.shape[0], ATT_WIDTH), F32),
        compiler_params=_cparams(("arbitrary", "arbitrary"), vmem),
        name="dsa_prompt",
    )(q16, iq16, ikw, kt16, vt16, ikw16)


def _dsa_sample_body(pt_ref, q_ref, iq_ref, ikw_ref, kn_ref, vn_ref, ck_hbm, cv_hbm, ci_hbm, att_hbm, o_ref,
                     kbuf, vbuf, ibuf, sems, *, layer, n_pages, page, tq, nb, topk):
    del att_hbm
    g = pl.program_id(0)
    slot = g % 2
    past = n_pages * page
    l = past + LANES
    group = N_HEADS // KV_HEADS

    def page_copies(step, s):
        out = []
        for bj in range(nb):
            for p in range(n_pages):
                idx = pt_ref[(step * nb + bj) * n_pages + p]
                cols = pl.ds(p * page, page)
                out.append(pltpu.make_async_copy(ck_hbm.at[layer, idx], kbuf.at[s, bj, :, cols], sems.at[s, 0]))
                out.append(pltpu.make_async_copy(cv_hbm.at[layer, idx], vbuf.at[s, bj, :, cols], sems.at[s, 1]))
                out.append(pltpu.make_async_copy(ci_hbm.at[layer, idx], ibuf.at[s, bj, :, cols], sems.at[s, 2]))
        return out

    @pl.when(g == 0)
    def _():
        for c in page_copies(0, 0):
            c.start()

    @pl.when(g + 1 < pl.num_programs(0))
    def _():
        for c in page_copies(g + 1, 1 - slot):
            c.start()

    for c in page_copies(g, slot):
        c.wait()

    def pad_rows(x):
        return jnp.concatenate([x, jnp.zeros((LANES - tq, x.shape[1]), x.dtype)], axis=0)

    iq_all = iq_ref[...].astype(F32)
    q_all = q_ref[...].astype(F32)
    ikw_all = ikw_ref[...]

    scores = []
    for bj in range(nb):
        rows = slice(bj * tq, (bj + 1) * tq)
        iq = iq_all[rows]
        iq_hm = jnp.concatenate([iq[:, h * IDX_DIM:(h + 1) * IDX_DIM] for h in range(IDX_HEADS)],
                                axis=0).astype(BF16)
        ik_new = pad_rows(ikw_all[rows, 0:IDX_DIM]).astype(BF16)
        lg = jnp.concatenate([_dot(iq_hm, ibuf[slot, bj].astype(BF16)), _dot_nt(iq_hm, ik_new)], axis=1)
        lg = jnp.maximum(lg, 0.0)
        w = ikw_all[rows, IDX_DIM:IDX_DIM + IDX_HEADS]
        sc = jnp.zeros((tq, l), F32)
        for h in range(IDX_HEADS):
            sc = sc + w[:, h:h + 1] * lg[h * tq:(h + 1) * tq, :]
        scores.append(sc)
    nq = nb * tq
    score_t = jnp.concatenate(scores + [jnp.zeros((LANES - nq, l), F32)], axis=0).T
    kidx = lax.broadcasted_iota(jnp.int32, (l, LANES), 0)
    qidx = lax.broadcasted_iota(jnp.int32, (l, LANES), 1)
    adm = kidx <= past + qidx % tq
    adm = jnp.logical_and(adm, kidx < past + tq)
    adm = jnp.logical_and(adm, qidx < nq)
    sel_t = _topk_mask_t(score_t, adm, topk)
    bias = ((sel_t - 1.0) * (-NEG_BIG)).T

    for bj in range(nb):
        rows = slice(bj * tq, (bj + 1) * tq)
        q = q_all[rows]
        bias_g = jnp.concatenate([bias[rows]] * group, axis=0)
        k_new = pad_rows(kn_ref[rows, :]).astype(BF16)
        v_new = pad_rows(vn_ref[rows, :]).astype(BF16)
        for hk in range(KV_HEADS):
            fs = slice(hk * HEAD_DIM, (hk + 1) * HEAD_DIM)
            qg = jnp.concatenate([q[:, h * HEAD_DIM:(h + 1) * HEAD_DIM]
                                  for h in range(hk * group, (hk + 1) * group)], axis=0).astype(BF16)
            kt = kbuf[slot, bj, fs, :].astype(BF16)
            vt = vbuf[slot, bj, fs, :].astype(BF16)
            s = jnp.concatenate([_dot(qg, kt), _dot_nt(qg, k_new[:, fs])], axis=1) + bias_g
            p = jnp.exp(s - jnp.max(s, axis=-1, keepdims=True))
            den = jnp.sum(p, axis=-1, keepdims=True)
            p = p.astype(BF16)
            o = (_dot_nt(p[:, 0:past], vt) + _dot(p[:, past:l], v_new[:, fs])) / den
            for j in range(group):
                h = hk * group + j
                o_ref[rows, h * HEAD_DIM:(h + 1) * HEAD_DIM] = o[j * tq:(j + 1) * tq, :]


def _dsa_sample(page_table, q16, iq16, ikw, k, v, cache_kt, cache_vt, cache_ikt, att, *, layer, row0, batch, tq, nb):
    n_pages = page_table.shape[1]
    page = cache_kt.shape[3]
    past = n_pages * page
    topk = min(TOPK_MAX, (past + tq) // 4)
    blk0 = row0 // (nb * tq)
    pt = page_table.reshape(-1)

    def tok(w):
        return pl.BlockSpec((nb * tq, w), lambda gi, pt_ref: (blk0 + gi, 0))

    hbm = pl.BlockSpec(memory_space=pl.ANY)
    in_specs = [tok(ATT_WIDTH), tok(IDX_HEADS * IDX_DIM), tok(LANES), tok(KV_WIDTH), tok(KV_WIDTH),
                hbm, hbm, hbm, hbm]
    buf_bytes = 2 * nb * past * (2 * KV_WIDTH + IDX_DIM) * 4
    vmem = buf_bytes + 16 * nb * (past + LANES) * LANES * 4 + (8 << 20)
    return pl.pallas_call(
        functools.partial(_dsa_sample_body, layer=layer, n_pages=n_pages, page=page, tq=tq, nb=nb, topk=topk),
        grid_spec=pltpu.PrefetchScalarGridSpec(
            num_scalar_prefetch=1,
            grid=(batch // nb,),
            in_specs=in_specs,
            out_specs=tok(ATT_WIDTH),
            scratch_shapes=[pltpu.VMEM((2, nb, KV_WIDTH, past), F32),
                            pltpu.VMEM((2, nb, KV_WIDTH, past), F32),
                            pltpu.VMEM((2, nb, IDX_DIM, past), F32),
                            pltpu.SemaphoreType.DMA((2, 3))],
        ),
        out_shape=jax.ShapeDtypeStruct(att.shape, att.dtype),
        input_output_aliases={len(in_specs): 0},
        compiler_params=_cparams(("arbitrary",), vmem),
        name="dsa_sample",
    )(pt, q16, iq16, ikw, k, v, cache_kt, cache_vt, cache_ikt, att)


def _hgrn_levels(c):
    out = []
    n = c // 2
    while n >= 1:
        out.append(n)
        n //= 2
    return out


def _group_row(x, j, g):
    c, n = x.shape
    x3 = x.reshape(c // g, g, n)
    return jnp.broadcast_to(x3[:, j:j + 1, :], (c // g, g, n)).reshape(c, n)


def _boundary_rows(b, n, row):
    g = 2 * n
    if g >= SUBLANES:
        return _group_row(b, n - 1, g)
    out = _group_row(b, n - 1, SUBLANES)
    for j in range(1, SUBLANES // g):
        out = jnp.where((row % SUBLANES) // g == j, _group_row(b, j * g + n - 1, SUBLANES), out)
    return out


def _hgrn_chunk(xq, xf, xi, lb, state_t):
    c, dk = xq.shape
    f = lb + (1.0 - lb) * jax.nn.sigmoid(xf)
    logf = jnp.log(f)
    kk = 1.0 - f
    qf = _silu(xq) * (HG_KDIM ** -0.5)
    ta, tb = _tri_incl(c)
    row = lax.broadcasted_iota(jnp.int32, (c, 1), 0)
    b = logf
    step = 1
    while step < c:
        b = b + jnp.where(row >= step, pltpu.roll(b, step, 0), 0.0)
        step *= 2

    o = _dot_nt((qf * jnp.exp(b)).astype(BF16), state_t.astype(BF16))
    o = o + jnp.sum(qf * kk, axis=-1, keepdims=True) * xi
    xi16 = xi.astype(BF16)
    att = jnp.zeros((c, c), F32)
    for n in _hgrn_levels(c):
        upper = (row % (2 * n)) >= n
        e = jnp.exp(-jnp.abs(b - _boundary_rows(b, n, row)))
        m = (jnp.where(upper, qf, kk) * e).astype(BF16)
        pair = jnp.logical_and(ta // (2 * n) == tb // (2 * n),
                               jnp.logical_and((ta % (2 * n)) >= n, (tb % (2 * n)) < n))
        att = att + jnp.where(pair, _dot_nt(m, m), 0.0)
    o = o + _dot(att.astype(BF16), xi16)

    blast = b[c - 1:c]
    kd = (kk * jnp.exp(blast - b)).astype(BF16)
    new_state_t = state_t * jnp.exp(blast) + _dot_tn(xi16, kd)
    return o, new_state_t


def _hgrn_gate(o, gnorm, xg):
    return _rms(o, gnorm) * _silu(xg)


def _hgrn_prompt_body(hq_ref, hf_ref, hi_ref, hg_ref, lb_ref, gn_ref, o_ref, s_ref, st_ref, *, chunk):
    gi = pl.program_id(1)
    is_fill = pl.program_id(0) == pl.num_programs(0) - 1

    @pl.when(is_fill)
    def _():
        o_ref[...] = jnp.zeros_like(o_ref)

    @pl.when(jnp.logical_not(is_fill))
    def _():
        @pl.when(gi == 0)
        def _():
            st_ref[...] = jnp.zeros_like(st_ref)

        n_chunks = hq_ref.shape[0] // chunk
        gn = gn_ref[...]

        def step(ci, carry):
            rows = pl.ds(pl.multiple_of(ci * chunk, chunk), chunk)
            for h in range(HG_HEADS):
                cols = slice(h * HG_KDIM, (h + 1) * HG_KDIM)
                o, st = _hgrn_chunk(hq_ref[rows, cols], hf_ref[rows, cols], hi_ref[rows, cols],
                                    lb_ref[:, cols], st_ref[h])
                st_ref[h] = st
                o_ref[rows, cols] = _hgrn_gate(o, gn, hg_ref[rows, cols])
            return carry

        lax.fori_loop(0, n_chunks, step, 0, unroll=2)

        @pl.when(gi == pl.num_programs(1) - 1)
        def _():
            for h in range(HG_HEADS):
                s_ref[h] = st_ref[h].T


def _hgrn_prompt(hq, hf, hi, hg, lb, gnorm, *, batch, seq, rows_per_step):
    ng = seq // rows_per_step
    chunk = min(HG_CHUNK, seq)
    last_blk = hq.shape[0] // rows_per_step - 1
    assert hq.shape[0] % rows_per_step == 0 and last_blk < (batch + 1) * ng

    def tok():
        return pl.BlockSpec((rows_per_step, HG_WIDTH), lambda bi, gi: (jnp.minimum(bi * ng + gi, last_blk), 0))

    vmem = 2 * 5 * rows_per_step * HG_WIDTH * 4 + 3 * HG_HEADS * HG_KDIM * HG_VDIM * 4 + (8 << 20)
    return pl.pallas_call(
        functools.partial(_hgrn_prompt_body, chunk=chunk),
        grid=(batch + 1, ng),
        in_specs=[tok(), tok(), tok(), tok(),
                  pl.BlockSpec((1, HG_WIDTH), lambda bi, gi: (0, 0)),
                  pl.BlockSpec((1, HG_VDIM), lambda bi, gi: (0, 0))],
        out_specs=[tok(), pl.BlockSpec((None, HG_HEADS, HG_KDIM, HG_VDIM),
                                       lambda bi, gi: (jnp.minimum(bi, batch - 1), 0, 0, 0))],
        out_shape=[jax.ShapeDtypeStruct((hq.shape[0], HG_WIDTH), F32),
                   jax.ShapeDtypeStruct((batch, HG_HEADS, HG_KDIM, HG_VDIM), F32)],
        scratch_shapes=[pltpu.VMEM((HG_HEADS, HG_KDIM, HG_VDIM), F32)],
        compiler_params=_cparams(("arbitrary", "arbitrary"), vmem),
        name="hgrn_prompt",
    )(hq, hf, hi, hg, lb, gnorm)


def _hgrn_sample_body(hq_ref, hf_ref, hi_ref, hg_ref, lb_ref, gn_ref, s0_ref, base_ref, o_ref, s_ref, *, tq, nb):
    del base_ref
    gn = gn_ref[...]

    def step(bi, carry):
        rows = pl.ds(pl.multiple_of(bi * tq, tq), tq)
        for h in range(HG_HEADS):
            cols = slice(h * HG_KDIM, (h + 1) * HG_KDIM)
            o, st = _hgrn_chunk(hq_ref[rows, cols], hf_ref[rows, cols], hi_ref[rows, cols],
                                lb_ref[:, cols], s0_ref[bi, h].T)
            s_ref[bi, h] = st.T
            o_ref[rows, cols] = _hgrn_gate(o, gn, hg_ref[rows, cols])
        return carry

    lax.fori_loop(0, nb, step, 0, unroll=2)


def _hgrn_sample(hq, hf, hi, hg, lb, gnorm, state, ohg, *, layer, row0, batch, tq, nb):
    assert tq <= HG_CHUNK
    blk0 = row0 // (nb * tq)

    def tok():
        return pl.BlockSpec((nb * tq, HG_WIDTH), lambda gi: (blk0 + gi, 0))

    st_blk = (nb, HG_HEADS, HG_KDIM, HG_VDIM)
    vmem = 2 * (5 * nb * tq * HG_WIDTH * 4 + 2 * nb * HG_HEADS * HG_KDIM * HG_VDIM * 4) + (8 << 20)
    return pl.pallas_call(
        functools.partial(_hgrn_sample_body, tq=tq, nb=nb),
        grid=(batch // nb,),
        in_specs=[tok(), tok(), tok(), tok(),
                  pl.BlockSpec((1, HG_WIDTH), lambda gi: (0, 0)),
                  pl.BlockSpec((1, HG_VDIM), lambda gi: (0, 0)),
                  pl.BlockSpec((None,) + st_blk, lambda gi: (layer, gi, 0, 0, 0)),
                  pl.BlockSpec(memory_space=pl.ANY)],
        out_specs=[tok(), pl.BlockSpec(st_blk, lambda gi: (gi, 0, 0, 0))],
        input_output_aliases={7: 0},
        out_shape=[jax.ShapeDtypeStruct(ohg.shape, ohg.dtype),
                   jax.ShapeDtypeStruct((batch, HG_HEADS, HG_KDIM, HG_VDIM), F32)],
        compiler_params=_cparams(("parallel",), vmem),
        name="hgrn_sample",
    )(hq, hf, hi, hg, lb, gnorm, state, ohg)


def _ple_body(h_ref, pp_ref, ps_ref, gpre_ref, gpost_ref, wgate_ref, wproj_ref, yp_ref, ys_ref, *, n_first_tiles):
    y = _ple_rows(h_ref[...], _pick(pp_ref, ps_ref, n_first_tiles), gpre_ref[...], gpost_ref[...],
                  wgate_ref[...], wproj_ref[...])
    is_first = pl.program_id(0) < n_first_tiles

    @pl.when(is_first)
    def _():
        yp_ref[...] = y

    @pl.when(jnp.logical_not(is_first))
    def _():
        ys_ref[...] = y


def _ple(h, p_first, p_rest, g_pre, g_post, w_gate, w_proj, *, tm):
    n, d = h.shape
    pd = p_first.shape[1]
    n_first = p_first.shape[0] // tm
    assert p_first.shape[0] % tm == 0 and p_rest.shape[0] % tm == 0
    first_map, rest_map = _split_maps(n_first)
    vmem = 2 * (3 * tm * d * 4 + 2 * tm * pd * 4 + d * d * 2 + pd * d * 2) + 3 * tm * d * 4 + (4 << 20)
    return pl.pallas_call(
        functools.partial(_ple_body, n_first_tiles=n_first),
        grid=(n // tm,),
        in_specs=[pl.BlockSpec((tm, d), lambda i: (i, 0)),
                  pl.BlockSpec((tm, pd), first_map),
                  pl.BlockSpec((tm, pd), rest_map),
                  pl.BlockSpec((1, d), lambda i: (0, 0)),
                  pl.BlockSpec((1, d), lambda i: (0, 0)),
                  pl.BlockSpec((d, d), lambda i: (0, 0)),
                  pl.BlockSpec((pd, d), lambda i: (0, 0))],
        out_specs=[pl.BlockSpec((tm, d), first_map), pl.BlockSpec((tm, d), rest_map)],
        out_shape=[jax.ShapeDtypeStruct((p_first.shape[0], d), F32), jax.ShapeDtypeStruct((p_rest.shape[0], d), F32)],
        compiler_params=_cparams(("arbitrary",), vmem),
        name="ple",
    )(h, p_first, p_rest, g_pre, g_post, w_gate, w_proj)


def _rot_cols(w, heads):
    d = w.shape[0]
    w4 = w.reshape(d, heads, 2, HEAD_DIM // 2)
    return jnp.concatenate([-w4[:, :, 1:2], w4[:, :, 0:1]], axis=2).reshape(d, heads * HEAD_DIM)


def _pack_w_in(w):
    d = w.shape[0]
    widths = (ATT_WIDTH, KV_WIDTH, KV_WIDTH, IDX_HEADS * IDX_DIM, IDX_DIM, IDX_HEADS,
              HG_HEADS * HG_KDIM, HG_HEADS * HG_KDIM, HG_WIDTH, HG_WIDTH)
    parts = []
    acc = 0
    for wd in widths:
        parts.append(w[:, acc:acc + wd])
        acc += wd
    q, k, v, iq, ik, iw, hq, hf, hi, hg = parts
    z = lambda n: jnp.zeros((d, n), w.dtype)
    ikw = jnp.concatenate([ik, iw, z(LANES - IDX_DIM - IDX_HEADS)], axis=1)
    ikr = jnp.concatenate([_rot_cols(ik, 1), z(LANES - IDX_DIM)], axis=1)
    cat = jnp.concatenate([q, k, iq, ikw, v, hq, hf, hi, hg,
                           _rot_cols(q, N_HEADS), _rot_cols(k, KV_HEADS), _rot_cols(iq, IDX_HEADS), ikr], axis=1)
    assert cat.shape[1] == _C_END
    w_t = jnp.concatenate([k, _rot_cols(k, KV_HEADS), v, ik, _rot_cols(ik, 1)], axis=1).T
    assert w_t.shape[0] == _R_END
    return cat.astype(BF16), w_t.astype(BF16)


def _rope_tables(seq, past_len, tq, tm):
    half = HEAD_DIM // 2
    inv_freq = ROPE_THETA ** (-jnp.arange(half, dtype=F32) / half)
    pos_p = jnp.arange(seq, dtype=jnp.int32)
    pos_s = past_len + (jnp.arange(tm, dtype=jnp.int32) % tq)
    pos = jnp.concatenate([pos_p, pos_s]).astype(F32)
    ang = pos[:, None] * inv_freq[None, :]
    reps = LANES // half
    cos = jnp.tile(jnp.cos(ang), (1, reps)).reshape(seq // tm + 1, tm, LANES)
    sin = jnp.tile(jnp.sin(ang), (1, reps)).reshape(seq // tm + 1, tm, LANES)
    nt = seq // tm
    cos_t = jnp.swapaxes(cos[:nt], 1, 2)
    sin_t = jnp.swapaxes(sin[:nt], 1, 2)
    return cos, sin, cos_t, sin_t


def _lower_bounds(logits):
    sm = jax.nn.softmax(logits.astype(F32), axis=0)
    return jnp.cumsum(sm, axis=0) - sm[0:1]


def _feature_major(cache, width):
    depth, pool, page = cache.shape[:3]
    c = cache.reshape(depth, pool, page, width)
    return jnp.swapaxes(c, 2, 3)


TM_DENSE = 1024
TM_PROJ = 512
TM_FFN = 1024
TF_FFN = 256
Q_BLOCK = 128
N_KEY_LEN = 8
DSA_SEQS = 4
HG_ROWS = 512
HG_SEQS = 8


def kernel(x_prompt, x_sample, p_prompt, p_sample, cache_k, cache_v, cache_idx_k, state_hgrn, page_table,
           n_f1_pre, n_f1_post, w_f1_gate, w_f1_up, w_f1_down,
           n_mix_pre, n_mix_post, w_in, hg_lb_logits, hg_norm, w_out,
           n_f2_pre, n_f2_post, w_f2_gate, w_f2_up, w_f2_down,
           n_ple_pre, n_ple_post, w_ple_proj, w_ple_gate):
    bp, tp, d = x_prompt.shape
    bs, ts, _ = x_sample.shape
    depth = w_in.shape[0]
    n_p, n_s = bp * tp, bs * ts
    page = cache_k.shape[2]
    past_len = page_table.shape[1] * page

    ckt = _feature_major(cache_k, KV_WIDTH)
    cvt = _feature_major(cache_v, KV_WIDTH)
    cikt = _feature_major(cache_idx_k, IDX_DIM)
    cos_tab, sin_tab, cos_t, sin_t = _rope_tables(tp, past_len, ts, TM_PROJ)
    lbs = _lower_bounds(hg_lb_logits)
    row = lambda a: a.reshape(1, -1)
    bf = lambda a: a.astype(BF16)

    outs = {name: [] for name in ("kp", "vp", "ikp", "sp", "ks", "vs", "iks", "ss")}
    ffn_kw = dict(n=n_p + n_s, tm=TM_FFN, tf=TF_FFN, n_first=n_p)
    ple_args = lambda i: (p_prompt[i].reshape(n_p, -1), p_sample[i].reshape(n_s, -1),
                          row(n_ple_pre[i]), row(n_ple_post[i]), bf(w_ple_gate[i]), bf(w_ple_proj[i]))
    h3 = None
    for i in range(depth):
        stage, ins = ("x2", (x_prompt.reshape(n_p, d), x_sample.reshape(n_s, d))) if i == 0 else \
            ("ple", (h3,) + ple_args(i - 1))
        h1 = _ffn(stage, ins, row(n_f1_pre[i]), row(n_f1_post[i]), bf(w_f1_gate[i]), bf(w_f1_up[i]),
                  bf(w_f1_down[i]), **ffn_kw)
        w_cat, w_t = _pack_w_in(w_in[i])
        g_mix = row(n_mix_pre[i])
        q16, k, v, iq16, ikw, hq, hf, hi, hg, ikw16 = _proj(
            h1, g_mix, w_cat, cos_tab, sin_tab,
            tm=TM_PROJ, tiles_per_seq=tp // TM_PROJ, n_prompt_tiles=n_p // TM_PROJ)
        kt, vt, ikt, kt16, vt16 = _proj_t(h1, g_mix, w_t, cos_t, sin_t, batch=bp, seq=tp, tm=TM_PROJ)

        att = _dsa_prompt(q16, iq16, ikw, ikw16, kt16, vt16, batch=bp, seq=tp, qb=min(Q_BLOCK, tp),
                          n_len=N_KEY_LEN)
        att = _dsa_sample(page_table, q16, iq16, ikw, k, v, ckt, cvt, cikt, att,
                          layer=i, row0=n_p, batch=bs, tq=ts, nb=DSA_SEQS)
        lb = row(lbs[i])
        gn = row(hg_norm[i])
        ohg, s_p = _hgrn_prompt(hq, hf, hi, hg, lb, gn, batch=bp, seq=tp, rows_per_step=HG_ROWS)
        ohg, s_s = _hgrn_sample(hq, hf, hi, hg, lb, gn, state_hgrn, ohg,
                                layer=i, row0=n_p, batch=bs, tq=ts, nb=HG_SEQS)

        h3 = _ffn("mix", (att, ohg, h1, row(n_mix_post[i]), bf(w_out[i])),
                  row(n_f2_pre[i]), row(n_f2_post[i]), bf(w_f2_gate[i]), bf(w_f2_up[i]), bf(w_f2_down[i]), **ffn_kw)

        outs["kp"].append(jnp.transpose(kt.reshape(bp, KV_HEADS, HEAD_DIM, tp), (0, 3, 1, 2)))
        outs["vp"].append(jnp.transpose(vt.reshape(bp, KV_HEADS, HEAD_DIM, tp), (0, 3, 1, 2)))
        outs["ikp"].append(jnp.swapaxes(ikt, 1, 2))
        outs["sp"].append(s_p)
        outs["ks"].append(k[n_p:].reshape(bs, ts, KV_HEADS, HEAD_DIM))
        outs["vs"].append(v[n_p:].reshape(bs, ts, KV_HEADS, HEAD_DIM))
        outs["iks"].append(ikw[n_p:, :IDX_DIM].reshape(bs, ts, IDX_DIM))
        outs["ss"].append(s_s.astype(state_hgrn.dtype))

    y_p, y_s = _ple(h3, *ple_args(depth - 1), tm=TM_DENSE)
    st = lambda name: jnp.stack(outs[name])
    return (y_p.reshape(bp, tp, d), y_s.reshape(bs, ts, d),
            st("kp"), st("vp"), st("ikp"), st("sp"), st("ks"), st("vs"), st("iks"), st("ss"))
```

```python
import functools

import jax
import jax.numpy as jnp
from jax import lax
from jax.experimental import pallas as pl
from jax.experimental.pallas import tpu as pltpu

HEAD_DIM = 64
N_HEADS = 8
KV_HEADS = 2
IDX_HEADS = 8
IDX_DIM = 64
TOPK_MAX = 256
HG_KDIM = 128
HG_VDIM = 128
HG_HEADS = 4
HG_CHUNK = 64
ROPE_THETA = 10000.0
EPS = 1e-6
ATT_WIDTH = N_HEADS * HEAD_DIM
KV_WIDTH = KV_HEADS * HEAD_DIM
HG_WIDTH = HG_HEADS * HG_VDIM
QK_SCALE = HEAD_DIM ** -0.5
IDX_SCALE = IDX_DIM ** -0.5

LANES = 128
SUBLANES = 8
VMEM_BUDGET_BYTES = 56 * 1024 * 1024

NEG_BIG = -1e30

F32 = jnp.float32
BF16 = jnp.bfloat16


def _cparams(sem, vmem_bytes):
    return pltpu.CompilerParams(dimension_semantics=sem,
                                vmem_limit_bytes=int(min(max(vmem_bytes, 16 << 20), VMEM_BUDGET_BYTES)))


def _rms(x, g):
    return x * lax.rsqrt(jnp.mean(x * x, axis=-1, keepdims=True) + EPS) * g


def _silu(x):
    return x * jax.nn.sigmoid(x)


def _dot(a, b):
    return jnp.dot(a, b, preferred_element_type=F32)


def _dot_nt(a, b):
    return lax.dot_general(a, b, (((1,), (1,)), ((), ())), preferred_element_type=F32)


def _dot_tn(a, b):
    return lax.dot_general(a, b, (((0,), (0,)), ((), ())), preferred_element_type=F32)


def _split_maps(n_first_tiles):
    return (lambda i, *_: (jnp.minimum(i, n_first_tiles - 1), 0),
            lambda i, *_: (jnp.maximum(i - n_first_tiles, 0), 0))


def _pick(first_ref, rest_ref, n_first_tiles):
    return jnp.where(pl.program_id(0) < n_first_tiles, first_ref[...], rest_ref[...])


def _mix_rows(att, ohg, h, g, w_ref):
    mix = _dot(att.astype(BF16), w_ref[0:ATT_WIDTH, :]) + \
        _dot(ohg.astype(BF16), w_ref[ATT_WIDTH:ATT_WIDTH + HG_WIDTH, :])
    return h + _rms(mix, g)


def _ple_rows(h, p, g_pre, g_post, w_gate, w_proj):
    gate = jax.nn.sigmoid(_dot(_rms(h, g_pre).astype(BF16), w_gate))
    return h + _rms(_dot(p.astype(BF16), w_proj) * gate, g_post)


_FFN_STAGE_INPUTS = {"x": 1, "x2": 2, "mix": 5, "ple": 7}
FFN_STAGE_ROWS = 256


def _ffn_body(*refs, stage, n_first_tiles):
    n_in = _FFN_STAGE_INPUTS[stage]
    ins = refs[:n_in]
    gpre_ref, gpost_ref, wg_ref, wu_ref, wd_ref, o_ref, x_ref, xn_ref, acc_ref = refs[n_in:]
    j = pl.program_id(1)
    is_first = pl.program_id(0) < n_first_tiles

    def stage_rows(r):
        if stage == "x":
            return ins[0][r, :]
        if stage == "x2":
            return jnp.where(is_first, ins[0][r, :], ins[1][r, :])
        if stage == "mix":
            att, ohg, h, g, w = ins
            return _mix_rows(att[r, :], ohg[r, :], h[r, :], g[...], w)
        h, pp, ps, g1, g2, wgate, wproj = ins
        return _ple_rows(h[r, :], jnp.where(is_first, pp[r, :], ps[r, :]), g1[...], g2[...], wgate[...], wproj[...])

    @pl.when(j == 0)
    def _():
        tm = x_ref.shape[0]
        rc = min(FFN_STAGE_ROWS, tm)
        for r0 in range(0, tm, rc):
            r = slice(r0, r0 + rc)
            x = stage_rows(r)
            x_ref[r, :] = x
            xn_ref[r, :] = _rms(x, gpre_ref[...]).astype(BF16)
        acc_ref[...] = jnp.zeros_like(acc_ref)

    xn = xn_ref[...]
    a = _dot(xn, wg_ref[...])
    b = _dot(xn, wu_ref[...])
    acc_ref[...] += _dot((_silu(a) * b).astype(BF16), wd_ref[...])

    @pl.when(j == pl.num_programs(1) - 1)
    def _():
        o_ref[...] = x_ref[...] + 0.5 * _rms(acc_ref[...], gpost_ref[...])


def _ffn(stage, ins, g_pre, g_post, wg, wu, wd, *, n, tm, tf, n_first):
    d = wg.shape[0]
    dff = wg.shape[1]
    assert n % tm == 0 and n_first % tm == 0 and dff % tf == 0 and len(ins) == _FFN_STAGE_INPUTS[stage]
    first_map, rest_map = _split_maps(n_first // tm)
    tok = lambda w: pl.BlockSpec((tm, w), lambda i, j: (i, 0))
    whole = lambda a: pl.BlockSpec(a.shape, lambda i, j: (0,) * a.ndim)
    if stage == "x":
        in_specs = [tok(d)]
    elif stage == "x2":
        in_specs = [pl.BlockSpec((tm, d), first_map), pl.BlockSpec((tm, d), rest_map)]
    elif stage == "mix":
        in_specs = [tok(ATT_WIDTH), tok(HG_WIDTH), tok(d), whole(ins[3]), whole(ins[4])]
    else:
        pd = ins[1].shape[1]
        in_specs = [tok(d), pl.BlockSpec((tm, pd), first_map), pl.BlockSpec((tm, pd), rest_map),
                    whole(ins[3]), whole(ins[4]), whole(ins[5]), whole(ins[6])]
    tile_bytes = sum(s.block_shape[0] * s.block_shape[1] * a.dtype.itemsize for s, a in zip(in_specs, ins))
    vmem = 2 * (tile_bytes + tm * d * 4 + 3 * d * tf * 2) + tm * d * 10 + 4 * tm * tf * 4 \
        + 6 * FFN_STAGE_ROWS * d * 4 + (4 << 20)
    return pl.pallas_call(
        functools.partial(_ffn_body, stage=stage, n_first_tiles=n_first // tm),
        grid=(n // tm, dff // tf),
        in_specs=in_specs + [
            pl.BlockSpec((1, d), lambda i, j: (0, 0)),
            pl.BlockSpec((1, d), lambda i, j: (0, 0)),
            pl.BlockSpec((d, tf), lambda i, j: (0, j)),
            pl.BlockSpec((d, tf), lambda i, j: (0, j)),
            pl.BlockSpec((tf, d), lambda i, j: (j, 0)),
        ],
        out_specs=pl.BlockSpec((tm, d), lambda i, j: (i, 0)),
        out_shape=jax.ShapeDtypeStruct((n, d), F32),
        scratch_shapes=[pltpu.VMEM((tm, d), F32), pltpu.VMEM((tm, d), BF16), pltpu.VMEM((tm, d), F32)],
        compiler_params=_cparams(("parallel", "arbitrary"), vmem),
        name="ffn_" + stage,
    )(*ins, g_pre, g_post, wg, wu, wd)


_C_Q, _C_K, _C_IQ, _C_IKW, _C_V, _C_HQ, _C_HF, _C_HI, _C_HG = 0, 512, 640, 1152, 1280, 1408, 1920, 2432, 2944
_C_QR, _C_KR, _C_IQR, _C_IKR, _C_END = 3456, 3968, 4096, 4608, 4736


def _proj_body(h_ref, g_ref, w_ref, cos_ref, sin_ref,
               q_ref, k_ref, v_ref, iq_ref, ikw_ref, hq_ref, hf_ref, hi_ref, hg_ref, ikw16_ref):
    u = _rms(h_ref[...], g_ref[...]).astype(BF16)

    def mm(lo, hi):
        return _dot(u, w_ref[:, lo:hi])

    cos = cos_ref[...]
    sin = sin_ref[...]
    cos4 = jnp.concatenate([cos] * 4, axis=1)
    sin4 = jnp.concatenate([sin] * 4, axis=1)
    q_ref[...] = ((mm(_C_Q, _C_K) * cos4 + mm(_C_QR, _C_KR) * sin4) * QK_SCALE).astype(BF16)
    k_ref[...] = mm(_C_K, _C_IQ) * cos + mm(_C_KR, _C_IQR) * sin
    iq_ref[...] = ((mm(_C_IQ, _C_IKW) * cos4 + mm(_C_IQR, _C_IKR) * sin4) * IDX_SCALE).astype(BF16)
    lane = lax.broadcasted_iota(jnp.int32, cos.shape, 1)
    cos_ikw = jnp.where(lane < IDX_DIM, cos, IDX_HEADS ** -0.5)
    ikw = mm(_C_IKW, _C_V) * cos_ikw + mm(_C_IKR, _C_END) * sin
    ikw_ref[...] = ikw
    ikw16_ref[...] = ikw.astype(BF16)
    v_ref[...] = mm(_C_V, _C_HQ)
    hq_ref[...] = mm(_C_HQ, _C_HF)
    hf_ref[...] = mm(_C_HF, _C_HI)
    hi_ref[...] = mm(_C_HI, _C_HG)
    hg_ref[...] = mm(_C_HG, _C_QR)


def _proj(h, g, w_cat, cos_tab, sin_tab, *, tm, tiles_per_seq, n_prompt_tiles):
    n, d = h.shape
    outs = ((512, BF16), (128, F32), (128, F32), (512, BF16), (128, F32),
            (512, F32), (512, F32), (512, F32), (512, F32), (128, BF16))

    def tab_map(i):
        return (jnp.where(i < n_prompt_tiles, i % tiles_per_seq, tiles_per_seq), 0, 0)

    out_bytes = sum(w * jnp.dtype(t).itemsize for w, t in outs)
    vmem = 2 * (tm * d * 4 + d * _C_END * 2 + tm * out_bytes + 2 * tm * LANES * 4) + 6 * tm * 512 * 4 + (4 << 20)
    return pl.pallas_call(
        _proj_body,
        grid=(n // tm,),
        in_specs=[
            pl.BlockSpec((tm, d), lambda i: (i, 0)),
            pl.BlockSpec((1, d), lambda i: (0, 0)),
            pl.BlockSpec((d, _C_END), lambda i: (0, 0)),
            pl.BlockSpec((None, tm, LANES), tab_map),
            pl.BlockSpec((None, tm, LANES), tab_map),
        ],
        out_specs=[pl.BlockSpec((tm, w), lambda i: (i, 0)) for w, _ in outs],
        out_shape=[jax.ShapeDtypeStruct((n, w), t) for w, t in outs],
        compiler_params=_cparams(("parallel",), vmem),
        name="proj",
    )(h, g, w_cat, cos_tab, sin_tab)


_R_K, _R_KR, _R_V, _R_IK, _R_IKR, _R_END = 0, 128, 256, 384, 448, 512


def _proj_t_body(h_ref, g_ref, w_ref, cos_ref, sin_ref, k_ref, v_ref, ik_ref, k16_ref, v16_ref):
    u = _rms(h_ref[...], g_ref[...]).astype(BF16)
    r = _dot_nt(w_ref[...], u)
    cos = cos_ref[...]
    sin = sin_ref[...]
    k = r[_R_K:_R_KR] * cos + r[_R_KR:_R_V] * sin
    v = r[_R_V:_R_IK]
    ik = r[_R_IK:_R_IKR] * cos[0:IDX_DIM] + r[_R_IKR:_R_END] * sin[0:IDX_DIM]
    k_ref[...] = k
    v_ref[...] = v
    ik_ref[...] = ik
    k16_ref[...] = k.astype(BF16)
    v16_ref[...] = v.astype(BF16)


def _proj_t(h, g, w_t, cos_t, sin_t, *, batch, seq, tm):
    d = h.shape[1]
    nt = seq // tm
    outs = ((KV_WIDTH, F32), (KV_WIDTH, F32), (IDX_DIM, F32), (KV_WIDTH, BF16), (KV_WIDTH, BF16))
    vmem = 2 * (tm * d * 4 + _R_END * d * 2 + 2 * LANES * tm * 4 + 3 * LANES * tm * 6) + 4 * _R_END * tm * 4 + (4 << 20)
    return pl.pallas_call(
        _proj_t_body,
        grid=(batch, nt),
        in_specs=[
            pl.BlockSpec((tm, d), lambda b, t: (b * nt + t, 0)),
            pl.BlockSpec((1, d), lambda b, t: (0, 0)),
            pl.BlockSpec((_R_END, d), lambda b, t: (0, 0)),
            pl.BlockSpec((None, LANES, tm), lambda b, t: (t, 0, 0)),
            pl.BlockSpec((None, LANES, tm), lambda b, t: (t, 0, 0)),
        ],
        out_specs=[pl.BlockSpec((None, w, tm), lambda b, t: (b, 0, t)) for w, _ in outs],
        out_shape=[jax.ShapeDtypeStruct((batch, w, seq), t) for w, t in outs],
        compiler_params=_cparams(("parallel", "parallel"), vmem),
        name="proj_t",
    )(h, g, w_t, cos_t, sin_t)


def _tri_incl(n):
    a = lax.broadcasted_iota(jnp.int32, (n, n), 0)
    b = lax.broadcasted_iota(jnp.int32, (n, n), 1)
    return a, b


I16_MIN = -32768
I16_ROWS = 16


def _tree_sum(xs):
    xs = list(xs)
    while len(xs) > 1:
        nxt = [xs[i] + xs[i + 1] for i in range(0, len(xs) - 1, 2)]
        if len(xs) % 2:
            nxt.append(xs[-1])
        xs = nxt
    return xs[0]


def _count_cols(mask, dtype):
    rows = I16_ROWS if dtype == jnp.int16 else SUBLANES
    one = jnp.where(mask, jnp.ones((), dtype), jnp.zeros((), dtype))
    part = _tree_sum([one[c * rows:(c + 1) * rows] for c in range(mask.shape[0] // rows)])
    return jnp.sum(part.astype(F32), axis=0, keepdims=True)


def _search16(x16, k):
    def body(i, t):
        cand = t + lax.shift_left(jnp.int32(1), 15 - i)
        cnt = _count_cols(x16 >= cand.astype(jnp.int16), jnp.int16)
        return jnp.where(cnt >= k, cand, t)

    return lax.fori_loop(0, 16, body, jnp.full(k.shape, I16_MIN, jnp.int32))


def _topk_mask_t(score, adm, k):
    l, n = score.shape
    score = jnp.where(score == 0.0, 0.0, score)
    score = jnp.where(adm, score, -jnp.inf)
    bits = lax.bitcast_convert_type(score, jnp.int32)
    key = bits ^ ((bits >> 31) & 0x7FFFFFFF)
    kf = jnp.full((1, n), float(k), F32)
    hi = (key >> 16).astype(jnp.int16)
    t_hi = _search16(hi, kf)
    t_hi16 = t_hi.astype(jnp.int16)
    above = _count_cols(hi > t_hi16, jnp.int16)
    lo = ((key & 0xFFFF) + I16_MIN).astype(jnp.int16)
    lo = jnp.where(hi == t_hi16, lo, jnp.int16(I16_MIN))
    t_lo = _search16(lo, kf - above)
    t = lax.shift_left(t_hi, 16) | (t_lo - I16_MIN)
    gt = key > t
    eq = key == t
    need = kf - _count_cols(gt, jnp.float32)
    a, b = _tri_incl(LANES)
    tril = jnp.where(b <= a, 1.0, 0.0).astype(BF16)
    carry = jnp.zeros((1, n), F32)
    sel = []
    for c in range(l // LANES):
        rows = slice(c * LANES, (c + 1) * LANES)
        pref = _dot(tril, jnp.where(eq[rows], 1.0, 0.0).astype(BF16)) + carry
        carry = pref[LANES - 1:LANES]
        take = jnp.logical_or(gt[rows], jnp.logical_and(eq[rows], pref <= need))
        sel.append(jnp.where(jnp.logical_and(take, adm[rows]), 1.0, 0.0))
    return jnp.concatenate(sel, axis=0) if len(sel) > 1 else sel[0]


def _dsa_prompt_tile(q_ref, iq_ref, iw_ref, kt_ref, vt_ref, ik_ref, o_ref, *, q0, l, topk):
    qb = q_ref.shape[0]
    iq_t = iq_ref[...].astype(F32).T.astype(BF16)
    w_t = iw_ref[...].T[IDX_DIM:IDX_DIM + IDX_HEADS]
    ik = ik_ref[0:l, 0:IDX_DIM]
    score = jnp.zeros((l, qb), F32)
    for h in range(IDX_HEADS):
        lg = _dot(ik, iq_t[h * IDX_DIM:(h + 1) * IDX_DIM])
        score = score + w_t[h:h + 1] * jnp.maximum(lg, 0.0)

    kpos = lax.broadcasted_iota(jnp.int32, (l, qb), 0)
    qpos = q0 + lax.broadcasted_iota(jnp.int32, (l, qb), 1)
    sel_t = _topk_mask_t(score, kpos <= qpos, topk)
    bias = ((sel_t - 1.0) * (-NEG_BIG)).T

    q = q_ref[...]
    group = N_HEADS // KV_HEADS
    for h in range(N_HEADS):
        g = h // group
        s = _dot(q[:, h * HEAD_DIM:(h + 1) * HEAD_DIM], kt_ref[g * HEAD_DIM:(g + 1) * HEAD_DIM, 0:l]) + bias
        p = jnp.exp(s - jnp.max(s, axis=-1, keepdims=True))
        den = jnp.sum(p, axis=-1, keepdims=True)
        o = _dot_nt(p.astype(BF16), vt_ref[g * HEAD_DIM:(g + 1) * HEAD_DIM, 0:l])
        o_ref[:, h * HEAD_DIM:(h + 1) * HEAD_DIM] = o / den


def _dsa_prompt_body(q_ref, iq_ref, iw_ref, kt_ref, vt_ref, ik_ref, o_ref, *, topk, n_len):
    qi = pl.program_id(1)
    nq = pl.num_programs(1)
    is_fill = pl.program_id(0) == pl.num_programs(0) - 1
    qb = q_ref.shape[0]
    seq = kt_ref.shape[1]
    per = nq // n_len
    for v in range(n_len):
        l = (v + 1) * per * qb

        @pl.when(jnp.logical_and(jnp.logical_not(is_fill), qi // per == v))
        def _(l=l):
            _dsa_prompt_tile(q_ref, iq_ref, iw_ref, kt_ref, vt_ref, ik_ref, o_ref,
                             q0=qi * qb, l=min(l, seq), topk=topk)

    @pl.when(is_fill)
    def _():
        o_ref[...] = jnp.zeros_like(o_ref)


def _dsa_prompt(q16, iq16, ikw, ikw16, kt16, vt16, *, batch, seq, qb, n_len):
    topk = min(TOPK_MAX, seq // 4)
    nq = seq // qb
    n_len = n_len if nq % n_len == 0 else 1
    last_blk = q16.shape[0] // qb - 1
    assert q16.shape[0] % qb == 0 and last_blk < (batch + 1) * nq
    tok_map = lambda bi, qi: (jnp.minimum(bi * nq + qi, last_blk), 0)
    seq_map = lambda bi, qi: (jnp.minimum(bi, batch - 1), 0, 0)
    vmem = 2 * (2 * qb * 512 * 2 + qb * LANES * 4 + qb * 512 * 4 + seq * (2 * KV_WIDTH + LANES) * 2) \
        + 16 * qb * seq * 4 + (4 << 20)
    return pl.pallas_call(
        functools.partial(_dsa_prompt_body, topk=topk, n_len=n_len),
        grid=(batch + 1, nq),
        in_specs=[
            pl.BlockSpec((qb, ATT_WIDTH), tok_map),
            pl.BlockSpec((qb, IDX_HEADS * IDX_DIM), tok_map),
            pl.BlockSpec((qb, LANES), tok_map),
            pl.BlockSpec((None, KV_WIDTH, seq), seq_map),
            pl.BlockSpec((None, KV_WIDTH, seq), seq_map),
            pl.BlockSpec((seq, LANES), lambda bi, qi: (jnp.minimum(bi, batch - 1), 0)),
        ],
        out_specs=pl.BlockSpec((qb, ATT_WIDTH), tok_map),
        out_shape=jax.ShapeDtypeStruct((q16.shape[0], ATT_WIDTH), F32),
        compiler_params=_cparams(("arbitrary", "arbitrary"), vmem),
        name="dsa_prompt",
    )(q16, iq16, ikw, kt16, vt16, ikw16)


def _dsa_sample_body(pt_ref, q_ref, iq_ref, ikw_ref, kn_ref, vn_ref, ck_hbm, cv_hbm, ci_hbm, att_hbm, o_ref,
                     kbuf, vbuf, ibuf, sems, *, layer, n_pages, page, tq, nb, topk):
    del att_hbm
    g = pl.program_id(0)
    slot = g % 2
    past = n_pages * page
    l = past + LANES
    group = N_HEADS // KV_HEADS

    def page_copies(step, s):
        out = []
        for bj in range(nb):
            for p in range(n_pages):
                idx = pt_ref[(step * nb + bj) * n_pages + p]
                cols = pl.ds(p * page, page)
                out.append(pltpu.make_async_copy(ck_hbm.at[layer, idx], kbuf.at[s, bj, :, cols], sems.at[s, 0]))
                out.append(pltpu.make_async_copy(cv_hbm.at[layer, idx], vbuf.at[s, bj, :, cols], sems.at[s, 1]))
                out.append(pltpu.make_async_copy(ci_hbm.at[layer, idx], ibuf.at[s, bj, :, cols], sems.at[s, 2]))
        return out

    @pl.when(g == 0)
    def _():
        for i, c in enumerate(page_copies(0, 0)):
            c.start(priority=(i // 3) % 2)

    @pl.when(g + 1 < pl.num_programs(0))
    def _():
        for i, c in enumerate(page_copies(g + 1, 1 - slot)):
            c.start(priority=(i // 3) % 2)

    for c in page_copies(g, slot):
        c.wait()

    def pad_rows(x):
        return jnp.concatenate([x, jnp.zeros((LANES - tq, x.shape[1]), x.dtype)], axis=0)

    iq_all = iq_ref[...].astype(F32)
    q_all = q_ref[...].astype(F32)
    ikw_all = ikw_ref[...]

    scores = []
    for bj in range(nb):
        rows = slice(bj * tq, (bj + 1) * tq)
        iq = iq_all[rows]
        iq_hm = jnp.concatenate([iq[:, h * IDX_DIM:(h + 1) * IDX_DIM] for h in range(IDX_HEADS)],
                                axis=0).astype(BF16)
        ik_new = pad_rows(ikw_all[rows, 0:IDX_DIM]).astype(BF16)
        lg = jnp.concatenate([_dot(iq_hm, ibuf[slot, bj].astype(BF16)), _dot_nt(iq_hm, ik_new)], axis=1)
        lg = jnp.maximum(lg, 0.0)
        w = ikw_all[rows, IDX_DIM:IDX_DIM + IDX_HEADS]
        sc = jnp.zeros((tq, l), F32)
        for h in range(IDX_HEADS):
            sc = sc + w[:, h:h + 1] * lg[h * tq:(h + 1) * tq, :]
        scores.append(sc)
    nq = nb * tq
    score_t = jnp.concatenate(scores + [jnp.zeros((LANES - nq, l), F32)], axis=0).T
    kidx = lax.broadcasted_iota(jnp.int32, (l, LANES), 0)
    qidx = lax.broadcasted_iota(jnp.int32, (l, LANES), 1)
    adm = kidx <= past + qidx % tq
    adm = jnp.logical_and(adm, kidx < past + tq)
    adm = jnp.logical_and(adm, qidx < nq)
    sel_t = _topk_mask_t(score_t, adm, topk)
    bias = ((sel_t - 1.0) * (-NEG_BIG)).T

    for bj in range(nb):
        rows = slice(bj * tq, (bj + 1) * tq)
        q = q_all[rows]
        bias_g = jnp.concatenate([bias[rows]] * group, axis=0)
        k_new = pad_rows(kn_ref[rows, :]).astype(BF16)
        v_new = pad_rows(vn_ref[rows, :]).astype(BF16)
        for hk in range(KV_HEADS):
            fs = slice(hk * HEAD_DIM, (hk + 1) * HEAD_DIM)
            qg = jnp.concatenate([q[:, h * HEAD_DIM:(h + 1) * HEAD_DIM]
                                  for h in range(hk * group, (hk + 1) * group)], axis=0).astype(BF16)
            kt = kbuf[slot, bj, fs, :].astype(BF16)
            vt = vbuf[slot, bj, fs, :].astype(BF16)
            s = jnp.concatenate([_dot(qg, kt), _dot_nt(qg, k_new[:, fs])], axis=1) + bias_g
            p = jnp.exp(s - jnp.max(s, axis=-1, keepdims=True))
            den = jnp.sum(p, axis=-1, keepdims=True)
            p = p.astype(BF16)
            o = (_dot_nt(p[:, 0:past], vt) + _dot(p[:, past:l], v_new[:, fs])) / den
            for j in range(group):
                h = hk * group + j
                o_ref[rows, h * HEAD_DIM:(h + 1) * HEAD_DIM] = o[j * tq:(j + 1) * tq, :]


def _dsa_sample(page_table, q16, iq16, ikw, k, v, cache_kt, cache_vt, cache_ikt, att, *, layer, row0, batch, tq, nb):
    n_pages = page_table.shape[1]
    page = cache_kt.shape[3]
    past = n_pages * page
    topk = min(TOPK_MAX, (past + tq) // 4)
    blk0 = row0 // (nb * tq)
    pt = page_table.reshape(-1)

    def tok(w):
        return pl.BlockSpec((nb * tq, w), lambda gi, pt_ref: (blk0 + gi, 0))

    hbm = pl.BlockSpec(memory_space=pl.ANY)
    in_specs = [tok(ATT_WIDTH), tok(IDX_HEADS * IDX_DIM), tok(LANES), tok(KV_WIDTH), tok(KV_WIDTH),
                hbm, hbm, hbm, hbm]
    buf_bytes = 2 * nb * past * (2 * KV_WIDTH + IDX_DIM) * 4
    vmem = buf_bytes + 16 * nb * (past + LANES) * LANES * 4 + (8 << 20)
    return pl.pallas_call(
        functools.partial(_dsa_sample_body, layer=layer, n_pages=n_pages, page=page, tq=tq, nb=nb, topk=topk),
        grid_spec=pltpu.PrefetchScalarGridSpec(
            num_scalar_prefetch=1,
            grid=(batch // nb,),
            in_specs=in_specs,
            out_specs=tok(ATT_WIDTH),
            scratch_shapes=[pltpu.VMEM((2, nb, KV_WIDTH, past), F32),
                            pltpu.VMEM((2, nb, KV_WIDTH, past), F32),
                            pltpu.VMEM((2, nb, IDX_DIM, past), F32),
                            pltpu.SemaphoreType.DMA((2, 3))],
        ),
        out_shape=jax.ShapeDtypeStruct(att.shape, att.dtype),
        input_output_aliases={len(in_specs): 0},
        compiler_params=_cparams(("arbitrary",), vmem),
        name="dsa_sample",
    )(pt, q16, iq16, ikw, k, v, cache_kt, cache_vt, cache_ikt, att)


def _hgrn_levels(c):
    out = []
    n = c // 2
    while n >= 1:
        out.append(n)
        n //= 2
    return out


def _group_row(x, j, g):
    c, n = x.shape
    x3 = x.reshape(c // g, g, n)
    return jnp.broadcast_to(x3[:, j:j + 1, :], (c // g, g, n)).reshape(c, n)


def _boundary_rows(b, n, row):
    g = 2 * n
    if g >= SUBLANES:
        return _group_row(b, n - 1, g)
    out = _group_row(b, n - 1, SUBLANES)
    for j in range(1, SUBLANES // g):
        out = jnp.where((row % SUBLANES) // g == j, _group_row(b, j * g + n - 1, SUBLANES), out)
    return out


def _hgrn_chunk(xq, xf, xi, lb, state_t):
    c, dk = xq.shape
    f = lb + (1.0 - lb) * jax.nn.sigmoid(xf)
    logf = jnp.log(f)
    kk = 1.0 - f
    qf = _silu(xq) * (HG_KDIM ** -0.5)
    ta, tb = _tri_incl(c)
    row = lax.broadcasted_iota(jnp.int32, (c, 1), 0)
    b = logf
    step = 1
    while step < c:
        b = b + jnp.where(row >= step, pltpu.roll(b, step, 0), 0.0)
        step *= 2

    o = _dot_nt((qf * jnp.exp(b)).astype(BF16), state_t.astype(BF16))
    o = o + jnp.sum(qf * kk, axis=-1, keepdims=True) * xi
    xi16 = xi.astype(BF16)
    att = jnp.zeros((c, c), F32)
    for n in _hgrn_levels(c):
        upper = (row % (2 * n)) >= n
        e = jnp.exp(-jnp.abs(b - _boundary_rows(b, n, row)))
        m = (jnp.where(upper, qf, kk) * e).astype(BF16)
        pair = jnp.logical_and(ta // (2 * n) == tb // (2 * n),
                               jnp.logical_and((ta % (2 * n)) >= n, (tb % (2 * n)) < n))
        att = att + jnp.where(pair, _dot_nt(m, m), 0.0)
    o = o + _dot(att.astype(BF16), xi16)

    blast = b[c - 1:c]
    kd = (kk * jnp.exp(blast - b)).astype(BF16)
    new_state_t = state_t * jnp.exp(blast) + _dot_tn(xi16, kd)
    return o, new_state_t


def _hgrn_gate(o, gnorm, xg):
    return _rms(o, gnorm) * _silu(xg)


def _hgrn_prompt_body(hq_ref, hf_ref, hi_ref, hg_ref, lb_ref, gn_ref, o_ref, s_ref, st_ref, *, chunk):
    gi = pl.program_id(1)
    is_fill = pl.program_id(0) == pl.num_programs(0) - 1

    @pl.when(is_fill)
    def _():
        o_ref[...] = jnp.zeros_like(o_ref)

    @pl.when(jnp.logical_not(is_fill))
    def _():
        @pl.when(gi == 0)
        def _():
            st_ref[...] = jnp.zeros_like(st_ref)

        n_chunks = hq_ref.shape[0] // chunk
        gn = gn_ref[...]

        def step(ci, carry):
            rows = pl.ds(pl.multiple_of(ci * chunk, chunk), chunk)
            for h in range(HG_HEADS):
                cols = slice(h * HG_KDIM, (h + 1) * HG_KDIM)
                o, st = _hgrn_chunk(hq_ref[rows, cols], hf_ref[rows, cols], hi_ref[rows, cols],
                                    lb_ref[:, cols], st_ref[h])
                st_ref[h] = st
                o_ref[rows, cols] = _hgrn_gate(o, gn, hg_ref[rows, cols])
            return carry

        lax.fori_loop(0, n_chunks, step, 0, unroll=2)

        @pl.when(gi == pl.num_programs(1) - 1)
        def _():
            for h in range(HG_HEADS):
                s_ref[h] = st_ref[h].T


def _hgrn_prompt(hq, hf, hi, hg, lb, gnorm, *, batch, seq, rows_per_step):
    ng = seq // rows_per_step
    chunk = min(HG_CHUNK, seq)
    last_blk = hq.shape[0] // rows_per_step - 1
    assert hq.shape[0] % rows_per_step == 0 and last_blk < (batch + 1) * ng

    def tok():
        return pl.BlockSpec((rows_per_step, HG_WIDTH), lambda bi, gi: (jnp.minimum(bi * ng + gi, last_blk), 0))

    vmem = 2 * 5 * rows_per_step * HG_WIDTH * 4 + 3 * HG_HEADS * HG_KDIM * HG_VDIM * 4 + (8 << 20)
    return pl.pallas_call(
        functools.partial(_hgrn_prompt_body, chunk=chunk),
        grid=(batch + 1, ng),
        in_specs=[tok(), tok(), tok(), tok(),
                  pl.BlockSpec((1, HG_WIDTH), lambda bi, gi: (0, 0)),
                  pl.BlockSpec((1, HG_VDIM), lambda bi, gi: (0, 0))],
        out_specs=[tok(), pl.BlockSpec((None, HG_HEADS, HG_KDIM, HG_VDIM),
                                       lambda bi, gi: (jnp.minimum(bi, batch - 1), 0, 0, 0))],
        out_shape=[jax.ShapeDtypeStruct((hq.shape[0], HG_WIDTH), F32),
                   jax.ShapeDtypeStruct((batch, HG_HEADS, HG_KDIM, HG_VDIM), F32)],
        scratch_shapes=[pltpu.VMEM((HG_HEADS, HG_KDIM, HG_VDIM), F32)],
        compiler_params=_cparams(("arbitrary", "arbitrary"), vmem),
        name="hgrn_prompt",
    )(hq, hf, hi, hg, lb, gnorm)


def _hgrn_sample_body(hq_ref, hf_ref, hi_ref, hg_ref, lb_ref, gn_ref, s0_ref, base_ref, o_ref, s_ref, *, tq, nb):
    del base_ref
    gn = gn_ref[...]

    def step(bi, carry):
        rows = pl.ds(pl.multiple_of(bi * tq, tq), tq)
        for h in range(HG_HEADS):
            cols = slice(h * HG_KDIM, (h + 1) * HG_KDIM)
            o, st = _hgrn_chunk(hq_ref[rows, cols], hf_ref[rows, cols], hi_ref[rows, cols],
                                lb_ref[:, cols], s0_ref[bi, h].T)
            s_ref[bi, h] = st.T
            o_ref[rows, cols] = _hgrn_gate(o, gn, hg_ref[rows, cols])
        return carry

    lax.fori_loop(0, nb, step, 0, unroll=2)


def _hgrn_sample(hq, hf, hi, hg, lb, gnorm, state, ohg, *, layer, row0, batch, tq, nb):
    assert tq <= HG_CHUNK
    blk0 = row0 // (nb * tq)

    def tok():
        return pl.BlockSpec((nb * tq, HG_WIDTH), lambda gi: (blk0 + gi, 0))

    st_blk = (nb, HG_HEADS, HG_KDIM, HG_VDIM)
    vmem = 2 * (5 * nb * tq * HG_WIDTH * 4 + 2 * nb * HG_HEADS * HG_KDIM * HG_VDIM * 4) + (8 << 20)
    return pl.pallas_call(
        functools.partial(_hgrn_sample_body, tq=tq, nb=nb),
        grid=(batch // nb,),
        in_specs=[tok(), tok(), tok(), tok(),
                  pl.BlockSpec((1, HG_WIDTH), lambda gi: (0, 0)),
                  pl.BlockSpec((1, HG_VDIM), lambda gi: (0, 0)),
                  pl.BlockSpec((None,) + st_blk, lambda gi: (layer, gi, 0, 0, 0)),
                  pl.BlockSpec(memory_space=pl.ANY)],
        out_specs=[tok(), pl.BlockSpec(st_blk, lambda gi: (gi, 0, 0, 0))],
        input_output_aliases={7: 0},
        out_shape=[jax.ShapeDtypeStruct(ohg.shape, ohg.dtype),
                   jax.ShapeDtypeStruct((batch, HG_HEADS, HG_KDIM, HG_VDIM), F32)],
        compiler_params=_cparams(("parallel",), vmem),
        name="hgrn_sample",
    )(hq, hf, hi, hg, lb, gnorm, state, ohg)


def _ple_body(h_ref, pp_ref, ps_ref, gpre_ref, gpost_ref, wgate_ref, wproj_ref, yp_ref, ys_ref, *, n_first_tiles):
    y = _ple_rows(h_ref[...], _pick(pp_ref, ps_ref, n_first_tiles), gpre_ref[...], gpost_ref[...],
                  wgate_ref[...], wproj_ref[...])
    is_first = pl.program_id(0) < n_first_tiles

    @pl.when(is_first)
    def _():
        yp_ref[...] = y

    @pl.when(jnp.logical_not(is_first))
    def _():
        ys_ref[...] = y


def _ple(h, p_first, p_rest, g_pre, g_post, w_gate, w_proj, *, tm):
    n, d = h.shape
    pd = p_first.shape[1]
    n_first = p_first.shape[0] // tm
    assert p_first.shape[0] % tm == 0 and p_rest.shape[0] % tm == 0
    first_map, rest_map = _split_maps(n_first)
    vmem = 2 * (3 * tm * d * 4 + 2 * tm * pd * 4 + d * d * 2 + pd * d * 2) + 3 * tm * d * 4 + (4 << 20)
    return pl.pallas_call(
        functools.partial(_ple_body, n_first_tiles=n_first),
        grid=(n // tm,),
        in_specs=[pl.BlockSpec((tm, d), lambda i: (i, 0)),
                  pl.BlockSpec((tm, pd), first_map),
                  pl.BlockSpec((tm, pd), rest_map),
                  pl.BlockSpec((1, d), lambda i: (0, 0)),
                  pl.BlockSpec((1, d), lambda i: (0, 0)),
                  pl.BlockSpec((d, d), lambda i: (0, 0)),
                  pl.BlockSpec((pd, d), lambda i: (0, 0))],
        out_specs=[pl.BlockSpec((tm, d), first_map), pl.BlockSpec((tm, d), rest_map)],
        out_shape=[jax.ShapeDtypeStruct((p_first.shape[0], d), F32), jax.ShapeDtypeStruct((p_rest.shape[0], d), F32)],
        compiler_params=_cparams(("arbitrary",), vmem),
        name="ple",
    )(h, p_first, p_rest, g_pre, g_post, w_gate, w_proj)


def _rot_cols(w, heads):
    d = w.shape[0]
    w4 = w.reshape(d, heads, 2, HEAD_DIM // 2)
    return jnp.concatenate([-w4[:, :, 1:2], w4[:, :, 0:1]], axis=2).reshape(d, heads * HEAD_DIM)


def _pack_w_in(w):
    d = w.shape[0]
    widths = (ATT_WIDTH, KV_WIDTH, KV_WIDTH, IDX_HEADS * IDX_DIM, IDX_DIM, IDX_HEADS,
              HG_HEADS * HG_KDIM, HG_HEADS * HG_KDIM, HG_WIDTH, HG_WIDTH)
    parts = []
    acc = 0
    for wd in widths:
        parts.append(w[:, acc:acc + wd])
        acc += wd
    q, k, v, iq, ik, iw, hq, hf, hi, hg = parts
    z = lambda n: jnp.zeros((d, n), w.dtype)
    ikw = jnp.concatenate([ik, iw, z(LANES - IDX_DIM - IDX_HEADS)], axis=1)
    ikr = jnp.concatenate([_rot_cols(ik, 1), z(LANES - IDX_DIM)], axis=1)
    cat = jnp.concatenate([q, k, iq, ikw, v, hq, hf, hi, hg,
                           _rot_cols(q, N_HEADS), _rot_cols(k, KV_HEADS), _rot_cols(iq, IDX_HEADS), ikr], axis=1)
    assert cat.shape[1] == _C_END
    w_t = jnp.concatenate([k, _rot_cols(k, KV_HEADS), v, ik, _rot_cols(ik, 1)], axis=1).T
    assert w_t.shape[0] == _R_END
    return cat.astype(BF16), w_t.astype(BF16)


def _rope_tables(seq, past_len, tq, tm):
    half = HEAD_DIM // 2
    inv_freq = ROPE_THETA ** (-jnp.arange(half, dtype=F32) / half)
    pos_p = jnp.arange(seq, dtype=jnp.int32)
    pos_s = past_len + (jnp.arange(tm, dtype=jnp.int32) % tq)
    pos = jnp.concatenate([pos_p, pos_s]).astype(F32)
    ang = pos[:, None] * inv_freq[None, :]
    reps = LANES // half
    cos = jnp.tile(jnp.cos(ang), (1, reps)).reshape(seq // tm + 1, tm, LANES)
    sin = jnp.tile(jnp.sin(ang), (1, reps)).reshape(seq // tm + 1, tm, LANES)
    nt = seq // tm
    cos_t = jnp.swapaxes(cos[:nt], 1, 2)
    sin_t = jnp.swapaxes(sin[:nt], 1, 2)
    return cos, sin, cos_t, sin_t


def _lower_bounds(logits):
    sm = jax.nn.softmax(logits.astype(F32), axis=0)
    return jnp.cumsum(sm, axis=0) - sm[0:1]


def _feature_major(cache, width):
    depth, pool, page = cache.shape[:3]
    c = cache.reshape(depth, pool, page, width)
    return jnp.swapaxes(c, 2, 3)


TM_DENSE = 1024
TM_PROJ = 512
TM_FFN = 1024
TF_FFN = 256
Q_BLOCK = 128
N_KEY_LEN = 8
DSA_SEQS = 4
HG_ROWS = 512
HG_SEQS = 8


def kernel(x_prompt, x_sample, p_prompt, p_sample, cache_k, cache_v, cache_idx_k, state_hgrn, page_table,
           n_f1_pre, n_f1_post, w_f1_gate, w_f1_up, w_f1_down,
           n_mix_pre, n_mix_post, w_in, hg_lb_logits, hg_norm, w_out,
           n_f2_pre, n_f2_post, w_f2_gate, w_f2_up, w_f2_down,
           n_ple_pre, n_ple_post, w_ple_proj, w_ple_gate):
    bp, tp, d = x_prompt.shape
    bs, ts, _ = x_sample.shape
    depth = w_in.shape[0]
    n_p, n_s = bp * tp, bs * ts
    page = cache_k.shape[2]
    past_len = page_table.shape[1] * page

    ckt = _feature_major(cache_k, KV_WIDTH)
    cvt = _feature_major(cache_v, KV_WIDTH)
    cikt = _feature_major(cache_idx_k, IDX_DIM)
    cos_tab, sin_tab, cos_t, sin_t = _rope_tables(tp, past_len, ts, TM_PROJ)
    lbs = _lower_bounds(hg_lb_logits)
    row = lambda a: a.reshape(1, -1)
    bf = lambda a: a.astype(BF16)

    outs = {name: [] for name in ("kp", "vp", "ikp", "sp", "ks", "vs", "iks", "ss")}
    ffn_kw = dict(n=n_p + n_s, tm=TM_FFN, tf=TF_FFN, n_first=n_p)
    ple_args = lambda i: (p_prompt[i].reshape(n_p, -1), p_sample[i].reshape(n_s, -1),
                          row(n_ple_pre[i]), row(n_ple_post[i]), bf(w_ple_gate[i]), bf(w_ple_proj[i]))
    h3 = None
    for i in range(depth):
        stage, ins = ("x2", (x_prompt.reshape(n_p, d), x_sample.reshape(n_s, d))) if i == 0 else \
            ("ple", (h3,) + ple_args(i - 1))
        h1 = _ffn(stage, ins, row(n_f1_pre[i]), row(n_f1_post[i]), bf(w_f1_gate[i]), bf(w_f1_up[i]),
                  bf(w_f1_down[i]), **ffn_kw)
        w_cat, w_t = _pack_w_in(w_in[i])
        g_mix = row(n_mix_pre[i])
        q16, k, v, iq16, ikw, hq, hf, hi, hg, ikw16 = _proj(
            h1, g_mix, w_cat, cos_tab, sin_tab,
            tm=TM_PROJ, tiles_per_seq=tp // TM_PROJ, n_prompt_tiles=n_p // TM_PROJ)
        kt, vt, ikt, kt16, vt16 = _proj_t(h1, g_mix, w_t, cos_t, sin_t, batch=bp, seq=tp, tm=TM_PROJ)

        att = _dsa_prompt(q16, iq16, ikw, ikw16, kt16, vt16, batch=bp, seq=tp, qb=min(Q_BLOCK, tp),
                          n_len=N_KEY_LEN)
        att = _dsa_sample(page_table, q16, iq16, ikw, k, v, ckt, cvt, cikt, att,
                          layer=i, row0=n_p, batch=bs, tq=ts, nb=DSA_SEQS)
        lb = row(lbs[i])
        gn = row(hg_norm[i])
        ohg, s_p = _hgrn_prompt(hq, hf, hi, hg, lb, gn, batch=bp, seq=tp, rows_per_step=HG_ROWS)
        ohg, s_s = _hgrn_sample(hq, hf, hi, hg, lb, gn, state_hgrn, ohg,
                                layer=i, row0=n_p, batch=bs, tq=ts, nb=HG_SEQS)

        h3 = _ffn("mix", (att, ohg, h1, row(n_mix_post[i]), bf(w_out[i])),
                  row(n_f2_pre[i]), row(n_f2_post[i]), bf(w_f2_gate[i]), bf(w_f2_up[i]), bf(w_f2_down[i]), **ffn_kw)

        outs["kp"].append(jnp.transpose(kt.reshape(bp, KV_HEADS, HEAD_DIM, tp), (0, 3, 1, 2)))
        outs["vp"].append(jnp.transpose(vt.reshape(bp, KV_HEADS, HEAD_DIM, tp), (0, 3, 1, 2)))
        outs["ikp"].append(jnp.swapaxes(ikt, 1, 2))
        outs["sp"].append(s_p)
        outs["ks"].append(k[n_p:].reshape(bs, ts, KV_HEADS, HEAD_DIM))
        outs["vs"].append(v[n_p:].reshape(bs, ts, KV_HEADS, HEAD_DIM))
        outs["iks"].append(ikw[n_p:, :IDX_DIM].reshape(bs, ts, IDX_DIM))
        outs["ss"].append(s_s.astype(state_hgrn.dtype))

    y_p, y_s = _ple(h3, *ple_args(depth - 1), tm=TM_DENSE)
    st = lambda name: jnp.stack(outs[name])
    return (y_p.reshape(bp, tp, d), y_s.reshape(bs, ts, d),
            st("kp"), st("vp"), st("ikp"), st("sp"), st("ks"), st("vs"), st("iks"), st("ss"))
```
